```python
import jax, jax.numpy as jnp
from jax import lax
import numpy as np

D_MODEL = 2048
BATCH = 8
SEQ = 2048
DEPTH = 1

MEM_LEN = 256
POOL_WINDOWS = (2, 4, 8, 16)
N_POOL_GROUPS = 4
POOL_GROUP_DIM = D_MODEL // 8
POOL_WIDTH = N_POOL_GROUPS * POOL_GROUP_DIM
N_FOURIER_GROUPS = 4
FOURIER_GROUP_DIM = D_MODEL // 16
FOURIER_WIDTH = N_FOURIER_GROUPS * FOURIER_GROUP_DIM
N_MEM_HEADS = 4
MEM_HEAD_DIM = D_MODEL // 16
MEM_WIDTH = N_MEM_HEADS * MEM_HEAD_DIM
MIX_WIDTH = POOL_WIDTH + FOURIER_WIDTH + MEM_WIDTH
N_BRANCHES = 3
IN_WIDTH = MIX_WIDTH + N_BRANCHES * D_MODEL
N_EXPERTS = 16
CAPACITY_FACTOR = 2
D_EXPERT = D_MODEL
EPS = 1e-6

kernel_name = "hybrid_pool_fourier_memattn_ecmoe_block"


def _rmsnorm(x, g):
    x32 = x.astype(jnp.float32)
    inv = lax.rsqrt(jnp.mean(x32 * x32, axis=-1, keepdims=True) + EPS)
    return (x32 * inv).astype(x.dtype) * g


def _centred_mean_minus_self(u, window):
    S = u.shape[1]
    u32 = u.astype(jnp.float32)
    csum = jnp.concatenate([jnp.zeros_like(u32[:, :1]), jnp.cumsum(u32, axis=1)], axis=1)
    pos = jnp.arange(S)
    lo = jnp.clip(pos - window // 2, 0, S)
    hi = jnp.clip(pos + window - window // 2, 0, S)
    total = jnp.take(csum, hi, axis=1) - jnp.take(csum, lo, axis=1)
    count = (hi - lo).astype(jnp.float32)[None, :, None]
    return (total / count - u32).astype(u.dtype)


def _fourier_2d_real(u):
    spec = jnp.fft.fft2(u.astype(jnp.float32), axes=(1, 3), norm="ortho")
    return spec.real.astype(u.dtype)


def _mixing_sublayer(x, mem, norm_mix_g, norm_mem_g, w_in, pool_w, pool_scale, fourier_w,
                     w_kv_mem, proj_pool, proj_fourier, proj_mem, w_out):
    B, S, D = x.shape
    h = _rmsnorm(x, norm_mix_g)
    z = h @ w_in
    u_pool = z[..., :POOL_WIDTH]
    u_four = z[..., POOL_WIDTH:POOL_WIDTH + FOURIER_WIDTH]
    q = z[..., POOL_WIDTH + FOURIER_WIDTH:MIX_WIDTH]
    gate_pre = z[..., MIX_WIDTH:]

    u_pool = u_pool.reshape(B, S, N_POOL_GROUPS, POOL_GROUP_DIM)
    pooled = jnp.stack([_centred_mean_minus_self(u_pool[:, :, g], w)
                        for g, w in enumerate(POOL_WINDOWS)], axis=2)
    y_a = jnp.einsum('bsgc,gcd->bsgd', pooled, pool_w).reshape(B, S, POOL_WIDTH) * pool_scale
    y_a = y_a @ proj_pool

    u_four = u_four.reshape(B, S, N_FOURIER_GROUPS, FOURIER_GROUP_DIM)
    f = _fourier_2d_real(u_four)
    y_b = jnp.einsum('bsgc,gcd->bsgd', f, fourier_w).reshape(B, S, FOURIER_WIDTH)
    y_b = y_b @ proj_fourier

    mem_n = _rmsnorm(mem, norm_mem_g)
    kv = mem_n @ w_kv_mem
    M = mem.shape[1]
    k = kv[..., :MEM_WIDTH].reshape(B, M, N_MEM_HEADS, MEM_HEAD_DIM)
    v = kv[..., MEM_WIDTH:].reshape(B, M, N_MEM_HEADS, MEM_HEAD_DIM)
    qh = q.reshape(B, S, N_MEM_HEADS, MEM_HEAD_DIM)
    scores = jnp.einsum('bshd,bmhd->bhsm', qh, k).astype(jnp.float32) * (MEM_HEAD_DIM ** -0.5)
    probs = jax.nn.softmax(scores, axis=-1).astype(v.dtype)
    o = jnp.einsum('bhsm,bmhd->bshd', probs, v).reshape(B, S, MEM_WIDTH)
    y_c = o @ proj_mem

    gates = jax.nn.sigmoid(gate_pre.astype(jnp.float32)).astype(x.dtype).reshape(B, S, N_BRANCHES, D)
    merged = gates[:, :, 0] * y_a + gates[:, :, 1] * y_b + gates[:, :, 2] * y_c
    return merged @ w_out


def _expert_choice_moe(h, w_router, w_expert_gate, w_expert_up, w_expert_down):
    B, S, D = h.shape
    affinity = jax.nn.softmax((h @ w_router).astype(jnp.float32), axis=-1)
    capacity = CAPACITY_FACTOR * S // N_EXPERTS
    top_aff, top_idx = lax.top_k(jnp.swapaxes(affinity, 1, 2), capacity)
    xin = jax.vmap(lambda hb, ib: hb[ib])(h, top_idx)
    hidden = jax.nn.silu(jnp.einsum('becd,edf->becf', xin, w_expert_gate)) * \
        jnp.einsum('becd,edf->becf', xin, w_expert_up)
    y = jnp.einsum('becf,efd->becd', hidden, w_expert_down) * top_aff[..., None].astype(h.dtype)
    return jax.vmap(lambda yb, ib: jnp.zeros((S, D), yb.dtype).at[ib.reshape(-1)].add(yb.reshape(-1, D)))(y, top_idx)


def setup_inputs(seed: int = 0) -> dict:
    key = jax.random.key(seed)
    ks = jax.random.split(key, 20)
    L, D, E, F = DEPTH, D_MODEL, N_EXPERTS, D_EXPERT
    nrm = lambda k, shape, fan_in: jax.random.normal(k, shape, jnp.float32) * (fan_in ** -0.5)
    gain = lambda k, shape: 1.0 + 0.02 * jax.random.normal(k, shape, jnp.float32)
    return {
        "x": jax.random.normal(ks[0], (BATCH, SEQ, D), jnp.float32),
        "mem": jax.random.normal(ks[1], (BATCH, MEM_LEN, D), jnp.float32),
        "norm_mix_g": gain(ks[2], (L, D)),
        "norm_mem_g": gain(ks[3], (L, D)),
        "w_in": nrm(ks[4], (L, D, IN_WIDTH), D),
        "pool_w": nrm(ks[5], (L, N_POOL_GROUPS, POOL_GROUP_DIM, POOL_GROUP_DIM), POOL_GROUP_DIM),
        "pool_scale": gain(ks[6], (L, POOL_WIDTH)),
        "fourier_w": nrm(ks[7], (L, N_FOURIER_GROUPS, FOURIER_GROUP_DIM, FOURIER_GROUP_DIM), FOURIER_GROUP_DIM),
        "w_kv_mem": nrm(ks[8], (L, D, 2 * MEM_WIDTH), D),
        "proj_pool": nrm(ks[9], (L, POOL_WIDTH, D), POOL_WIDTH),
        "proj_fourier": nrm(ks[10], (L, FOURIER_WIDTH, D), FOURIER_WIDTH),
        "proj_mem": nrm(ks[11], (L, MEM_WIDTH, D), MEM_WIDTH),
        "w_out": nrm(ks[12], (L, D, D), D),
        "norm_ffn_g": gain(ks[13], (L, D)),
        "w_router": nrm(ks[14], (L, D, E), D),
        "w_expert_gate": nrm(ks[15], (L, E, D, F), D),
        "w_expert_up": nrm(ks[16], (L, E, D, F), D),
        "w_expert_down": nrm(ks[17], (L, E, F, D), F),
        "norm_final_g": gain(ks[18], (D,)),
    }


def reference(x, mem, norm_mix_g, norm_mem_g, w_in, pool_w, pool_scale, fourier_w, w_kv_mem,
              proj_pool, proj_fourier, proj_mem, w_out, norm_ffn_g, w_router, w_expert_gate,
              w_expert_up, w_expert_down, norm_final_g):
    for l in range(DEPTH):
        x = x + _mixing_sublayer(x, mem, norm_mix_g[l], norm_mem_g[l], w_in[l], pool_w[l],
                                 pool_scale[l], fourier_w[l], w_kv_mem[l], proj_pool[l],
                                 proj_fourier[l], proj_mem[l], w_out[l])
        h2 = _rmsnorm(x, norm_ffn_g[l])
        x = x + _expert_choice_moe(h2, w_router[l], w_expert_gate[l], w_expert_up[l], w_expert_down[l])
    return _rmsnorm(x, norm_final_g)
```

```python
import functools

import jax
import jax.numpy as jnp
from jax import lax
from jax.experimental import pallas as pl
from jax.experimental.pallas import tpu as pltpu

F32 = jnp.float32
BF16 = jnp.bfloat16

EPS = 1e-6
POOL_WINDOWS = (2, 4, 8, 16)
N_EXPERTS = 16
CAPACITY_FACTOR = 2
LANES = 128
POOL_HALO = 16
MIB = 1024 * 1024


def _cparams(n_axes, vmem_mib):
    return pltpu.CompilerParams(
        dimension_semantics=("arbitrary",) * n_axes,
        vmem_limit_bytes=vmem_mib * MIB,
    )


def _rmsnorm_kernel(x_ref, g_ref, o_ref):
    x = x_ref[...]
    inv = lax.rsqrt(jnp.mean(x * x, axis=-1, keepdims=True) + EPS)
    o_ref[...] = ((x * inv) * g_ref[...]).astype(o_ref.dtype)


def _rmsnorm(x, g, *, tm, out_dtype):
    m, d = x.shape
    return pl.pallas_call(
        _rmsnorm_kernel,
        grid=(m // tm,),
        in_specs=[pl.BlockSpec((tm, d), lambda i: (i, 0)), pl.BlockSpec((1, d), lambda i: (0, 0))],
        out_specs=pl.BlockSpec((tm, d), lambda i: (i, 0)),
        out_shape=jax.ShapeDtypeStruct((m, d), out_dtype),
        compiler_params=_cparams(1, 40),
        name="rmsnorm",
    )(x, g.reshape(1, d))


def _add_rmsnorm_kernel(x_ref, y_ref, g_ref, o_ref):
    x = x_ref[...] + y_ref[...]
    inv = lax.rsqrt(jnp.mean(x * x, axis=-1, keepdims=True) + EPS)
    o_ref[...] = ((x * inv) * g_ref[...]).astype(o_ref.dtype)


def _add_rmsnorm(x, y, g, *, tm):
    m, d = x.shape
    return pl.pallas_call(
        _add_rmsnorm_kernel,
        grid=(m // tm,),
        in_specs=[pl.BlockSpec((tm, d), lambda i: (i, 0)), pl.BlockSpec((tm, d), lambda i: (i, 0)),
                  pl.BlockSpec((1, d), lambda i: (0, 0))],
        out_specs=pl.BlockSpec((tm, d), lambda i: (i, 0)),
        out_shape=jax.ShapeDtypeStruct((m, d), F32),
        compiler_params=_cparams(1, 48),
        name="add_rmsnorm",
    )(x, y, g.reshape(1, d))


def _mm_kernel(a_ref, w_ref, o_ref, wc_ref):
    @pl.when(pl.program_id(1) == 0)
    def _():
        wc_ref[...] = w_ref[...].astype(BF16)

    o_ref[...] = jnp.dot(a_ref[...], wc_ref[...], preferred_element_type=F32).astype(o_ref.dtype)


def _mm_res_kernel(a_ref, w_ref, r_ref, o_ref, wc_ref):
    @pl.when(pl.program_id(1) == 0)
    def _():
        wc_ref[...] = w_ref[...].astype(BF16)

    o_ref[...] = r_ref[...] + jnp.dot(a_ref[...], wc_ref[...], preferred_element_type=F32)


def _matmul(a, w, *, n_cols, tm, tn, out_dtype, residual=None, name):
    m, k = a.shape
    in_specs = [pl.BlockSpec((tm, k), lambda n, i: (i, 0)), pl.BlockSpec((k, tn), lambda n, i: (0, n))]
    args = [a, w]
    kern = _mm_kernel
    if residual is not None:
        in_specs.append(pl.BlockSpec((tm, tn), lambda n, i: (i, n)))
        args.append(residual)
        kern = _mm_res_kernel
    return pl.pallas_call(
        kern,
        grid=(n_cols // tn, m // tm),
        in_specs=in_specs,
        out_specs=pl.BlockSpec((tm, tn), lambda n, i: (i, n)),
        out_shape=jax.ShapeDtypeStruct((m, n_cols), out_dtype),
        scratch_shapes=[pltpu.VMEM((k, tn), BF16)],
        compiler_params=_cparams(2, 48),
        name=name,
    )(*args)


def _pool_kernel(u_ref, pw_ref, ps_ref, o_ref, pad_ref, *, chunk):
    s = u_ref.shape[1]
    c = pw_ref.shape[1]
    h = POOL_HALO
    zeros = jnp.zeros((h, c), F32)
    pad_ref[pl.ds(0, h), :] = zeros
    pad_ref[pl.ds(h + s, h), :] = zeros
    for g, w in enumerate(POOL_WINDOWS):
        cols = slice(g * c, (g + 1) * c)
        pad_ref[pl.ds(h, s), :] = u_ref[0, :, cols]
        wg = pw_ref[g].astype(BF16)
        scale = ps_ref[:, cols]
        for r0 in range(0, s, chunk):
            pos = r0 + lax.broadcasted_iota(jnp.int32, (chunk, 1), 0)
            lo = jnp.maximum(pos - w // 2, 0)
            hi = jnp.minimum(pos + (w - w // 2), s)
            cnt = (hi - lo).astype(F32)
            tot = pad_ref[pl.ds(h + r0 - w // 2, chunk), :]
            for k in range(1, w):
                tot = tot + pad_ref[pl.ds(h + r0 - w // 2 + k, chunk), :]
            pooled = tot / cnt - pad_ref[pl.ds(h + r0, chunk), :]
            y = jnp.dot(pooled.astype(BF16), wg, preferred_element_type=F32)
            o_ref[0, pl.ds(r0, chunk), cols] = (y * scale).astype(o_ref.dtype)


def _pool_mixer(z3, pool_w, pool_scale):
    b, s, _ = z3.shape
    g, c, _ = pool_w.shape
    width = g * c
    return pl.pallas_call(
        functools.partial(_pool_kernel, chunk=256),
        grid=(b,),
        in_specs=[pl.BlockSpec((1, s, width), lambda i: (i, 0, 0)),
                  pl.BlockSpec((g, c, c), lambda i: (0, 0, 0)),
                  pl.BlockSpec((1, width), lambda i: (0, 0))],
        out_specs=pl.BlockSpec((1, s, width), lambda i: (i, 0, 0)),
        out_shape=jax.ShapeDtypeStruct((b, s, width), BF16),
        scratch_shapes=[pltpu.VMEM((s + 2 * POOL_HALO, c), F32)],
        compiler_params=_cparams(1, 48),
        name="pool_mixer",
    )(z3, pool_w, pool_scale.reshape(1, width))


def _dft_tables(s, c):
    def tab(n):
        i = lax.broadcasted_iota(jnp.int32, (n, n), 0)
        j = lax.broadcasted_iota(jnp.int32, (n, n), 1)
        ang = ((i * j) % n).astype(F32) * (2.0 * jnp.pi / n)
        return jnp.cos(ang), jnp.sin(ang)
    cs, ss = tab(s)
    cc, sc = tab(c)
    return (jnp.concatenate([cs, -ss], axis=1).astype(BF16),
            jnp.concatenate([cc, sc], axis=1).astype(BF16))


def _fourier_kernel(u_ref, dft_ref, ccsc_ref, fw_ref, o_ref, t_ref, *, norm):
    s = u_ref.shape[1]
    ng, c, _ = fw_ref.shape

    @pl.when(pl.program_id(1) == 0)
    def _():
        for g in range(ng):
            ug = u_ref[0, :, g * c:(g + 1) * c].astype(BF16)
            ab = jnp.dot(ug, ccsc_ref[...], preferred_element_type=F32)
            t_ref[pl.ds(0, s), g * c:(g + 1) * c] = ab[:, :c].astype(BF16)
            t_ref[pl.ds(s, s), g * c:(g + 1) * c] = ab[:, c:].astype(BF16)

    f = jnp.dot(dft_ref[...], t_ref[...], preferred_element_type=F32) * norm
    for g in range(ng):
        y = jnp.dot(f[:, g * c:(g + 1) * c].astype(BF16), fw_ref[g].astype(BF16), preferred_element_type=F32)
        o_ref[0, :, g * c:(g + 1) * c] = y.astype(o_ref.dtype)


def _fourier_mixer(z3, fourier_w, *, col_block, tk):
    b, s, _ = z3.shape
    ng, c, _ = fourier_w.shape
    width = ng * c
    dft, ccsc = _dft_tables(s, c)
    return pl.pallas_call(
        functools.partial(_fourier_kernel, norm=float((s * c) ** -0.5)),
        grid=(b, s // tk),
        in_specs=[pl.BlockSpec((1, s, width), lambda i, k: (i, 0, col_block)),
                  pl.BlockSpec((tk, 2 * s), lambda i, k: (k, 0)),
                  pl.BlockSpec((c, 2 * c), lambda i, k: (0, 0)),
                  pl.BlockSpec((ng, c, c), lambda i, k: (0, 0, 0))],
        out_specs=pl.BlockSpec((1, tk, width), lambda i, k: (i, k, 0)),
        out_shape=jax.ShapeDtypeStruct((b, s, width), BF16),
        scratch_shapes=[pltpu.VMEM((2 * s, width), BF16)],
        compiler_params=_cparams(2, 48),
        name="fourier_mixer",
    )(z3, dft, ccsc, fourier_w)


def _attn_kernel(q_ref, kv_ref, o_ref, *, n_heads, scale):
    dh = q_ref.shape[2] // n_heads
    width = n_heads * dh
    for h in range(n_heads):
        q = q_ref[0, :, h * dh:(h + 1) * dh].astype(BF16)
        k = kv_ref[0, :, h * dh:(h + 1) * dh]
        v = kv_ref[0, :, width + h * dh:width + (h + 1) * dh]
        sc = lax.dot_general(q, k, (((1,), (1,)), ((), ())), preferred_element_type=F32) * scale
        mx = jnp.max(sc, axis=-1, keepdims=True)
        p = jnp.exp(sc - mx)
        p = p / jnp.sum(p, axis=-1, keepdims=True)
        o = jnp.dot(p.astype(BF16), v, preferred_element_type=F32)
        o_ref[0, :, h * dh:(h + 1) * dh] = o.astype(o_ref.dtype)


def _mem_attention(z3, kv3, *, col_block, n_heads, tm):
    b, s, _ = z3.shape
    _, m, kvw = kv3.shape
    width = kvw // 2
    return pl.pallas_call(
        functools.partial(_attn_kernel, n_heads=n_heads, scale=float((width // n_heads) ** -0.5)),
        grid=(b, s // tm),
        in_specs=[pl.BlockSpec((1, tm, width), lambda i, j: (i, j, col_block)),
                  pl.BlockSpec((1, m, kvw), lambda i, j: (i, 0, 0))],
        out_specs=pl.BlockSpec((1, tm, width), lambda i, j: (i, j, 0)),
        out_shape=jax.ShapeDtypeStruct((b, s, width), BF16),
        compiler_params=_cparams(2, 40),
        name="mem_attention",
    )(z3, kv3)


def _merge_kernel(ya_ref, yb_ref, yc_ref, h_ref, pp_ref, pf_ref, pm_ref, g0_ref, g1_ref, g2_ref, o_ref,
                  cpp, cpf, cpm, cg0, cg1, cg2):
    @pl.when(pl.program_id(1) == 0)
    def _():
        for src, dst in ((pp_ref, cpp), (pf_ref, cpf), (pm_ref, cpm), (g0_ref, cg0), (g1_ref, cg1), (g2_ref, cg2)):
            dst[...] = src[...].astype(BF16)

    h = h_ref[...]

    def branch(y_ref, proj, gate_w):
        gate = jax.nn.sigmoid(jnp.dot(h, gate_w[...], preferred_element_type=F32))
        return gate * jnp.dot(y_ref[...], proj[...], preferred_element_type=F32)

    acc = branch(ya_ref, cpp, cg0)
    acc = acc + branch(yb_ref, cpf, cg1)
    acc = acc + branch(yc_ref, cpm, cg2)
    o_ref[...] = acc.astype(o_ref.dtype)


def _gated_merge(ya, yb, yc, h, proj_pool, proj_fourier, proj_mem, w_in, *, gate_col0, tm, tn):
    m, d = h.shape
    gb = gate_col0 // tn
    nb = d // tn
    act = lambda width: pl.BlockSpec((tm, width), lambda n, i: (i, 0))
    wsp = lambda rows, off: pl.BlockSpec((rows, tn), lambda n, i: (0, n + off))
    return pl.pallas_call(
        _merge_kernel,
        grid=(d // tn, m // tm),
        in_specs=[act(ya.shape[1]), act(yb.shape[1]), act(yc.shape[1]), act(d),
                  wsp(proj_pool.shape[0], 0), wsp(proj_fourier.shape[0], 0), wsp(proj_mem.shape[0], 0),
                  wsp(d, gb), wsp(d, gb + nb), wsp(d, gb + 2 * nb)],
        out_specs=pl.BlockSpec((tm, tn), lambda n, i: (i, n)),
        out_shape=jax.ShapeDtypeStruct((m, d), BF16),
        scratch_shapes=[pltpu.VMEM((proj_pool.shape[0], tn), BF16), pltpu.VMEM((proj_fourier.shape[0], tn), BF16),
                        pltpu.VMEM((proj_mem.shape[0], tn), BF16), pltpu.VMEM((d, tn), BF16),
                        pltpu.VMEM((d, tn), BF16), pltpu.VMEM((d, tn), BF16)],
        compiler_params=_cparams(2, 56),
        name="gated_merge",
    )(ya, yb, yc, h, proj_pool, proj_fourier, proj_mem, w_in, w_in, w_in)


def _norm_router_kernel(x_ref, g_ref, wr_ref, h_ref, aff_ref, *, n_experts):
    x = x_ref[...]
    inv = lax.rsqrt(jnp.mean(x * x, axis=-1, keepdims=True) + EPS)
    hb = ((x * inv) * g_ref[...]).astype(BF16)
    h_ref[...] = hb
    logits = jnp.dot(hb, wr_ref[...].astype(BF16), preferred_element_type=F32)
    lane = lax.broadcasted_iota(jnp.int32, logits.shape, 1)
    logits = jnp.where(lane < n_experts, logits, -1e30)
    mx = jnp.max(logits, axis=-1, keepdims=True)
    p = jnp.exp(logits - mx)
    aff_ref[...] = p / jnp.sum(p, axis=-1, keepdims=True)


def _norm_router(x, g, w_router, *, tm):
    m, d = x.shape
    e = w_router.shape[1]
    wr = jnp.pad(w_router, ((0, 0), (0, LANES - e)))
    return pl.pallas_call(
        functools.partial(_norm_router_kernel, n_experts=e),
        grid=(m // tm,),
        in_specs=[pl.BlockSpec((tm, d), lambda i: (i, 0)), pl.BlockSpec((1, d), lambda i: (0, 0)),
                  pl.BlockSpec((d, LANES), lambda i: (0, 0))],
        out_specs=[pl.BlockSpec((tm, d), lambda i: (i, 0)), pl.BlockSpec((tm, LANES), lambda i: (i, 0))],
        out_shape=[jax.ShapeDtypeStruct((m, d), BF16), jax.ShapeDtypeStruct((m, LANES), F32)],
        compiler_params=_cparams(1, 40),
        name="norm_router",
    )(x, g.reshape(1, d), wr)


def _expert_up_kernel(x_ref, wg_ref, wu_ref, o_ref, *, chunk):
    wg = wg_ref[0].astype(BF16)
    wu = wu_ref[0].astype(BF16)
    for r0 in range(0, x_ref.shape[1], chunk):
        x = x_ref[0, pl.ds(r0, chunk), :]
        gate = jnp.dot(x, wg, preferred_element_type=F32)
        up = jnp.dot(x, wu, preferred_element_type=F32)
        o_ref[0, pl.ds(r0, chunk), :] = (jax.nn.silu(gate) * up).astype(o_ref.dtype)


def _expert_up(xin, w_gate, w_up, *, tf):
    e, m, d = xin.shape
    f = w_gate.shape[2]
    return pl.pallas_call(
        functools.partial(_expert_up_kernel, chunk=512),
        grid=(e, f // tf),
        in_specs=[pl.BlockSpec((1, m, d), lambda i, j: (i, 0, 0)),
                  pl.BlockSpec((1, d, tf), lambda i, j: (i, 0, j)),
                  pl.BlockSpec((1, d, tf), lambda i, j: (i, 0, j))],
        out_specs=pl.BlockSpec((1, m, tf), lambda i, j: (i, 0, j)),
        out_shape=jax.ShapeDtypeStruct((e, m, f), BF16),
        compiler_params=_cparams(2, 56),
        name="expert_up",
    )(xin, w_gate, w_up)


def _expert_down_kernel(h_ref, wd_ref, a_ref, o_ref, *, chunk):
    wd = wd_ref[0].astype(BF16)
    for r0 in range(0, h_ref.shape[1], chunk):
        y = jnp.dot(h_ref[0, pl.ds(r0, chunk), :], wd, preferred_element_type=F32)
        o_ref[0, pl.ds(r0, chunk), :] = y * a_ref[0, pl.ds(r0, chunk), :]


def _expert_down(hidden, w_down, aff, *, tn):
    e, m, f = hidden.shape
    d = w_down.shape[2]
    return pl.pallas_call(
        functools.partial(_expert_down_kernel, chunk=512),
        grid=(e, d // tn),
        in_specs=[pl.BlockSpec((1, m, f), lambda i, j: (i, 0, 0)),
                  pl.BlockSpec((1, f, tn), lambda i, j: (i, 0, j)),
                  pl.BlockSpec((1, m, 1), lambda i, j: (i, 0, 0))],
        out_specs=pl.BlockSpec((1, m, tn), lambda i, j: (i, 0, j)),
        out_shape=jax.ShapeDtypeStruct((e, m, d), F32),
        compiler_params=_cparams(2, 56),
        name="expert_down",
    )(hidden, w_down, aff)


def kernel(x, mem, norm_mix_g, norm_mem_g, w_in, pool_w, pool_scale, fourier_w, w_kv_mem, proj_pool, proj_fourier,
           proj_mem, w_out, norm_ffn_g, w_router, w_expert_gate, w_expert_up, w_expert_down, norm_final_g):
    b, s, d = x.shape
    depth = w_in.shape[0]
    n_tok = b * s
    pool_width = pool_w.shape[1] * pool_w.shape[2]
    four_width = fourier_w.shape[1] * fourier_w.shape[2]
    mem_width = proj_mem.shape[1]
    mix_width = pool_width + four_width + mem_width
    n_heads = 4
    e = w_router.shape[2]
    cap = CAPACITY_FACTOR * s // e

    xf = x.reshape(n_tok, d)
    for l in range(depth):
        h = _rmsnorm(xf, norm_mix_g[l], tm=512, out_dtype=BF16)
        z = _matmul(h, w_in[l], n_cols=mix_width, tm=1024, tn=512, out_dtype=F32, name="in_proj")
        z3 = z.reshape(b, s, mix_width)
        ya = _pool_mixer(z3, pool_w[l], pool_scale[l])
        yb = _fourier_mixer(z3, fourier_w[l], col_block=pool_width // four_width, tk=512)
        memn = _rmsnorm(mem.reshape(-1, d), norm_mem_g[l], tm=512, out_dtype=BF16)
        kv = _matmul(memn, w_kv_mem[l], n_cols=2 * mem_width, tm=1024, tn=512, out_dtype=BF16, name="kv_proj")
        yc = _mem_attention(z3, kv.reshape(b, -1, 2 * mem_width),
                            col_block=(pool_width + four_width) // mem_width, n_heads=n_heads, tm=512)
        merged = _gated_merge(ya.reshape(n_tok, -1), yb.reshape(n_tok, -1), yc.reshape(n_tok, -1), h,
                              proj_pool[l], proj_fourier[l], proj_mem[l], w_in[l],
                              gate_col0=mix_width, tm=1024, tn=256)
        x1 = _matmul(merged, w_out[l], n_cols=d, tm=1024, tn=512, out_dtype=F32, residual=xf, name="out_proj")

        h2, aff = _norm_router(x1, norm_ffn_g[l], w_router[l], tm=512)
        aff_t = jnp.swapaxes(aff[:, :e].reshape(b, s, e), 1, 2)
        top_aff, top_idx = lax.top_k(aff_t, cap)
        xin = jax.vmap(lambda hb, ib: hb[ib])(h2.reshape(b, s, d), top_idx)
        xin = jnp.swapaxes(xin, 0, 1).reshape(e, b * cap, d)
        hidden = _expert_up(xin, w_expert_gate[l], w_expert_up[l], tf=512)
        aff_e = jnp.swapaxes(top_aff, 0, 1).reshape(e, b * cap, 1)
        y = _expert_down(hidden, w_expert_down[l], aff_e, tn=512)
        y = jnp.swapaxes(y.reshape(e, b, cap, d), 0, 1).reshape(b, e * cap, d)
        moe = jax.vmap(lambda yb_, ib: jnp.zeros((s, d), F32).at[ib.reshape(-1)].add(yb_))(y, top_idx)
        if l + 1 < depth:
            xf = x1 + moe.reshape(n_tok, d)
        else:
            return _add_rmsnorm(x1, moe.reshape(n_tok, d), norm_final_g, tm=512).reshape(b, s, d)
```

```python
import functools

import jax
import jax.numpy as jnp
from jax import lax
from jax.experimental import pallas as pl
from jax.experimental.pallas import tpu as pltpu

F32 = jnp.float32
BF16 = jnp.bfloat16

EPS = 1e-6
POOL_WINDOWS = (2, 4, 8, 16)
N_EXPERTS = 16
CAPACITY_FACTOR = 2
LANES = 128
POOL_HALO = 16
MIB = 1024 * 1024


def _cparams(n_axes, vmem_mib):
    return pltpu.CompilerParams(
        dimension_semantics=("arbitrary",) * n_axes,
        vmem_limit_bytes=vmem_mib * MIB,
    )


def _rmsnorm_kernel(x_ref, g_ref, o_ref):
    x = x_ref[...]
    inv = lax.rsqrt(jnp.mean(x * x, axis=-1, keepdims=True) + EPS)
    o_ref[...] = ((x * inv) * g_ref[...]).astype(o_ref.dtype)


def _rmsnorm(x, g, *, tm, out_dtype):
    m, d = x.shape
    return pl.pallas_call(
        _rmsnorm_kernel,
        grid=(m // tm,),
        in_specs=[pl.BlockSpec((tm, d), lambda i: (i, 0)), pl.BlockSpec((1, d), lambda i: (0, 0))],
        out_specs=pl.BlockSpec((tm, d), lambda i: (i, 0)),
        out_shape=jax.ShapeDtypeStruct((m, d), out_dtype),
        compiler_params=_cparams(1, 40),
        name="rmsnorm",
    )(x, g.reshape(1, d))


def _mm_kernel(a_ref, w_ref, o_ref, wc_ref):
    @pl.when(pl.program_id(1) == 0)
    def _():
        wc_ref[...] = w_ref[...].astype(BF16)

    o_ref[...] = jnp.dot(a_ref[...], wc_ref[...], preferred_element_type=F32).astype(o_ref.dtype)


def _mm_res_kernel(a_ref, w_ref, r_ref, o_ref, wc_ref):
    @pl.when(pl.program_id(1) == 0)
    def _():
        wc_ref[...] = w_ref[...].astype(BF16)

    o_ref[...] = r_ref[...] + jnp.dot(a_ref[...], wc_ref[...], preferred_element_type=F32)


def _matmul(a, w, *, n_cols, tm, tn, out_dtype, residual=None, name):
    m, k = a.shape
    in_specs = [pl.BlockSpec((tm, k), lambda n, i: (i, 0)), pl.BlockSpec((k, tn), lambda n, i: (0, n))]
    args = [a, w]
    kern = _mm_kernel
    if residual is not None:
        in_specs.append(pl.BlockSpec((tm, tn), lambda n, i: (i, n)))
        args.append(residual)
        kern = _mm_res_kernel
    return pl.pallas_call(
        kern,
        grid=(n_cols // tn, m // tm),
        in_specs=in_specs,
        out_specs=pl.BlockSpec((tm, tn), lambda n, i: (i, n)),
        out_shape=jax.ShapeDtypeStruct((m, n_cols), out_dtype),
        scratch_shapes=[pltpu.VMEM((k, tn), BF16)],
        compiler_params=_cparams(2, 48),
        name=name,
    )(*args)


def _pool_kernel(u_ref, pw_ref, ps_ref, o_ref, pad_ref, *, chunk):
    s = u_ref.shape[1]
    c = pw_ref.shape[1]
    h = POOL_HALO
    zeros = jnp.zeros((h, c), F32)
    pad_ref[pl.ds(0, h), :] = zeros
    pad_ref[pl.ds(h + s, h), :] = zeros
    for g, w in enumerate(POOL_WINDOWS):
        cols = slice(g * c, (g + 1) * c)
        pad_ref[pl.ds(h, s), :] = u_ref[0, :, cols]
        wg = pw_ref[g].astype(BF16)
        scale = ps_ref[:, cols]
        for r0 in range(0, s, chunk):
            pos = r0 + lax.broadcasted_iota(jnp.int32, (chunk, 1), 0)
            lo = jnp.maximum(pos - w // 2, 0)
            hi = jnp.minimum(pos + (w - w // 2), s)
            cnt = (hi - lo).astype(F32)
            tot = pad_ref[pl.ds(h + r0 - w // 2, chunk), :]
            for k in range(1, w):
                tot = tot + pad_ref[pl.ds(h + r0 - w // 2 + k, chunk), :]
            pooled = tot / cnt - pad_ref[pl.ds(h + r0, chunk), :]
            y = jnp.dot(pooled.astype(BF16), wg, preferred_element_type=F32)
            o_ref[0, pl.ds(r0, chunk), cols] = (y * scale).astype(o_ref.dtype)


def _pool_mixer(z3, pool_w, pool_scale):
    b, s, _ = z3.shape
    g, c, _ = pool_w.shape
    width = g * c
    return pl.pallas_call(
        functools.partial(_pool_kernel, chunk=256),
        grid=(b,),
        in_specs=[pl.BlockSpec((1, s, width), lambda i: (i, 0, 0)),
                  pl.BlockSpec((g, c, c), lambda i: (0, 0, 0)),
                  pl.BlockSpec((1, width), lambda i: (0, 0))],
        out_specs=pl.BlockSpec((1, s, width), lambda i: (i, 0, 0)),
        out_shape=jax.ShapeDtypeStruct((b, s, width), BF16),
        scratch_shapes=[pltpu.VMEM((s + 2 * POOL_HALO, c), F32)],
        compiler_params=_cparams(1, 48),
        name="pool_mixer",
    )(z3, pool_w, pool_scale.reshape(1, width))


def _dft_tables(s, c):
    def tab(n):
        i = lax.broadcasted_iota(jnp.int32, (n, n), 0)
        j = lax.broadcasted_iota(jnp.int32, (n, n), 1)
        ang = ((i * j) % n).astype(F32) * (2.0 * jnp.pi / n)
        return jnp.cos(ang), jnp.sin(ang)
    cs, ss = tab(s)
    cc, sc = tab(c)
    return (jnp.concatenate([cs, -ss], axis=1).astype(BF16),
            jnp.concatenate([cc, sc], axis=1).astype(BF16))


def _fourier_kernel(u_ref, dft_ref, ccsc_ref, fw_ref, o_ref, t_ref, *, norm):
    s = u_ref.shape[1]
    ng, c, _ = fw_ref.shape

    @pl.when(pl.program_id(1) == 0)
    def _():
        for g in range(ng):
            ug = u_ref[0, :, g * c:(g + 1) * c].astype(BF16)
            ab = jnp.dot(ug, ccsc_ref[...], preferred_element_type=F32)
            t_ref[pl.ds(0, s), g * c:(g + 1) * c] = ab[:, :c].astype(BF16)
            t_ref[pl.ds(s, s), g * c:(g + 1) * c] = ab[:, c:].astype(BF16)

    f = jnp.dot(dft_ref[...], t_ref[...], preferred_element_type=F32) * norm
    for g in range(ng):
        y = jnp.dot(f[:, g * c:(g + 1) * c].astype(BF16), fw_ref[g].astype(BF16), preferred_element_type=F32)
        o_ref[0, :, g * c:(g + 1) * c] = y.astype(o_ref.dtype)


def _fourier_mixer(z3, fourier_w, *, col_block, tk):
    b, s, _ = z3.shape
    ng, c, _ = fourier_w.shape
    width = ng * c
    dft, ccsc = _dft_tables(s, c)
    return pl.pallas_call(
        functools.partial(_fourier_kernel, norm=float((s * c) ** -0.5)),
        grid=(b, s // tk),
        in_specs=[pl.BlockSpec((1, s, width), lambda i, k: (i, 0, col_block)),
                  pl.BlockSpec((tk, 2 * s), lambda i, k: (k, 0)),
                  pl.BlockSpec((c, 2 * c), lambda i, k: (0, 0)),
                  pl.BlockSpec((ng, c, c), lambda i, k: (0, 0, 0))],
        out_specs=pl.BlockSpec((1, tk, width), lambda i, k: (i, k, 0)),
        out_shape=jax.ShapeDtypeStruct((b, s, width), BF16),
        scratch_shapes=[pltpu.VMEM((2 * s, width), BF16)],
        compiler_params=_cparams(2, 48),
        name="fourier_mixer",
    )(z3, dft, ccsc, fourier_w)


def _attn_kernel(q_ref, kv_ref, o_ref, *, n_heads, scale):
    dh = q_ref.shape[2] // n_heads
    width = n_heads * dh
    for h in range(n_heads):
        q = q_ref[0, :, h * dh:(h + 1) * dh].astype(BF16)
        k = kv_ref[0, :, h * dh:(h + 1) * dh]
        v = kv_ref[0, :, width + h * dh:width + (h + 1) * dh]
        sc = lax.dot_general(q, k, (((1,), (1,)), ((), ())), preferred_element_type=F32) * scale
        mx = jnp.max(sc, axis=-1, keepdims=True)
        p = jnp.exp(sc - mx)
        p = p / jnp.sum(p, axis=-1, keepdims=True)
        o = jnp.dot(p.astype(BF16), v, preferred_element_type=F32)
        o_ref[0, :, h * dh:(h + 1) * dh] = o.astype(o_ref.dtype)


def _mem_attention(z3, kv3, *, col_block, n_heads, tm):
    b, s, _ = z3.shape
    _, m, kvw = kv3.shape
    width = kvw // 2
    return pl.pallas_call(
        functools.partial(_attn_kernel, n_heads=n_heads, scale=float((width // n_heads) ** -0.5)),
        grid=(b, s // tm),
        in_specs=[pl.BlockSpec((1, tm, width), lambda i, j: (i, j, col_block)),
                  pl.BlockSpec((1, m, kvw), lambda i, j: (i, 0, 0))],
        out_specs=pl.BlockSpec((1, tm, width), lambda i, j: (i, j, 0)),
        out_shape=jax.ShapeDtypeStruct((b, s, width), BF16),
        compiler_params=_cparams(2, 40),
        name="mem_attention",
    )(z3, kv3)


def _merge_kernel(ya_ref, yb_ref, yc_ref, h_ref, pp_ref, pf_ref, pm_ref, g0_ref, g1_ref, g2_ref, o_ref,
                  cpp, cpf, cpm, cg0, cg1, cg2):
    @pl.when(pl.program_id(1) == 0)
    def _():
        for src, dst in ((pp_ref, cpp), (pf_ref, cpf), (pm_ref, cpm), (g0_ref, cg0), (g1_ref, cg1), (g2_ref, cg2)):
            dst[...] = src[...].astype(BF16)

    h = h_ref[...]

    def branch(y_ref, proj, gate_w):
        gate = jax.nn.sigmoid(jnp.dot(h, gate_w[...], preferred_element_type=F32))
        return gate * jnp.dot(y_ref[...], proj[...], preferred_element_type=F32)

    acc = branch(ya_ref, cpp, cg0)
    acc = acc + branch(yb_ref, cpf, cg1)
    acc = acc + branch(yc_ref, cpm, cg2)
    o_ref[...] = acc.astype(o_ref.dtype)


def _gated_merge(ya, yb, yc, h, proj_pool, proj_fourier, proj_mem, w_in, *, gate_col0, tm, tn):
    m, d = h.shape
    gb = gate_col0 // tn
    nb = d // tn
    act = lambda width: pl.BlockSpec((tm, width), lambda n, i: (i, 0))
    wsp = lambda rows, off: pl.BlockSpec((rows, tn), lambda n, i: (0, n + off))
    return pl.pallas_call(
        _merge_kernel,
        grid=(d // tn, m // tm),
        in_specs=[act(ya.shape[1]), act(yb.shape[1]), act(yc.shape[1]), act(d),
                  wsp(proj_pool.shape[0], 0), wsp(proj_fourier.shape[0], 0), wsp(proj_mem.shape[0], 0),
                  wsp(d, gb), wsp(d, gb + nb), wsp(d, gb + 2 * nb)],
        out_specs=pl.BlockSpec((tm, tn), lambda n, i: (i, n)),
        out_shape=jax.ShapeDtypeStruct((m, d), BF16),
        scratch_shapes=[pltpu.VMEM((proj_pool.shape[0], tn), BF16), pltpu.VMEM((proj_fourier.shape[0], tn), BF16),
                        pltpu.VMEM((proj_mem.shape[0], tn), BF16), pltpu.VMEM((d, tn), BF16),
                        pltpu.VMEM((d, tn), BF16), pltpu.VMEM((d, tn), BF16)],
        compiler_params=_cparams(2, 56),
        name="gated_merge",
    )(ya, yb, yc, h, proj_pool, proj_fourier, proj_mem, w_in, w_in, w_in)


def _norm_router_kernel(x_ref, g_ref, wr_ref, h_ref, aff_ref, afft_ref, *, n_experts):
    x = x_ref[...]
    inv = lax.rsqrt(jnp.mean(x * x, axis=-1, keepdims=True) + EPS)
    hb = ((x * inv) * g_ref[...]).astype(BF16)
    h_ref[...] = hb
    logits = jnp.dot(hb, wr_ref[...].astype(BF16), preferred_element_type=F32)
    lane = lax.broadcasted_iota(jnp.int32, logits.shape, 1)
    logits = jnp.where(lane < n_experts, logits, -1e30)
    mx = jnp.max(logits, axis=-1, keepdims=True)
    p = jnp.exp(logits - mx)
    aff = p / jnp.sum(p, axis=-1, keepdims=True)
    aff_ref[...] = aff
    afft_ref[0] = aff.T[:n_experts, :]


def _norm_router(x, g, w_router, *, batch, tm):
    m, d = x.shape
    e = w_router.shape[1]
    s = m // batch
    tps = s // tm
    wr = jnp.pad(w_router, ((0, 0), (0, LANES - e)))
    return pl.pallas_call(
        functools.partial(_norm_router_kernel, n_experts=e),
        grid=(m // tm,),
        in_specs=[pl.BlockSpec((tm, d), lambda i: (i, 0)), pl.BlockSpec((1, d), lambda i: (0, 0)),
                  pl.BlockSpec((d, LANES), lambda i: (0, 0))],
        out_specs=[pl.BlockSpec((tm, d), lambda i: (i, 0)), pl.BlockSpec((tm, LANES), lambda i: (i, 0)),
                   pl.BlockSpec((1, e, tm), lambda i: (i // tps, 0, i % tps))],
        out_shape=[jax.ShapeDtypeStruct((m, d), BF16), jax.ShapeDtypeStruct((m, LANES), F32),
                   jax.ShapeDtypeStruct((batch, e, s), F32)],
        compiler_params=_cparams(1, 40),
        name="norm_router",
    )(x, g.reshape(1, d), wr)


def _select_kernel(aff_ref, tri_ref, slot_ref, slot_t_ref, *, cap):
    a = aff_ref[...]
    rows = a.shape[0]
    capf = float(cap)

    def count(pred):
        return jnp.sum(pred.astype(F32), axis=-1, keepdims=True)

    def body(i, t_bits):
        cand = t_bits | jnp.left_shift(jnp.int32(1), 30 - i)
        return jnp.where(count(a >= pltpu.bitcast(cand, F32)) >= capf, cand, t_bits)

    t = pltpu.bitcast(lax.fori_loop(0, 31, body, jnp.zeros((rows, 1), jnp.int32)), F32)
    gt = a > t
    eq = a == t
    need = capf - count(gt)
    tri = tri_ref[...]
    eq_rank = jnp.dot(eq.astype(BF16), tri, preferred_element_type=F32)
    sel = gt | (eq & (eq_rank < need))
    pos = jnp.dot(sel.astype(BF16), tri, preferred_element_type=F32)
    slot = jnp.where(sel, pos, -1.0)
    slot_ref[...] = slot
    slot_t_ref[...] = slot.T


def _select(aff_rows, *, cap):
    rows, s = aff_rows.shape
    idx = jnp.arange(s, dtype=jnp.int32)
    tri = (idx[:, None] < idx[None, :]).astype(BF16)
    return pl.pallas_call(
        functools.partial(_select_kernel, cap=cap),
        grid=(1,),
        in_specs=[pl.BlockSpec((rows, s), lambda i: (0, 0)), pl.BlockSpec((s, s), lambda i: (0, 0))],
        out_specs=[pl.BlockSpec((rows, s), lambda i: (0, 0)), pl.BlockSpec((s, rows), lambda i: (0, 0))],
        out_shape=[jax.ShapeDtypeStruct((rows, s), F32), jax.ShapeDtypeStruct((s, rows), F32)],
        compiler_params=_cparams(1, 48),
        name="expert_select",
    )(aff_rows, tri)


def _dispatch_kernel(slot_ref, h_ref, o_ref):
    cap = o_ref.shape[1]
    s = h_ref.shape[1]
    slot = slot_ref[0]
    j = lax.broadcasted_iota(jnp.int32, (cap, s), 0).astype(F32)
    onehot = (slot == j).astype(BF16)
    o_ref[0] = jnp.dot(onehot, h_ref[0], preferred_element_type=F32).astype(o_ref.dtype)


def _dispatch(slot_rows, h3, *, n_experts, cap):
    b, s, d = h3.shape
    return pl.pallas_call(
        _dispatch_kernel,
        grid=(b, n_experts),
        in_specs=[pl.BlockSpec((1, 1, s), lambda i, j: (i * n_experts + j, 0, 0)),
                  pl.BlockSpec((1, s, d), lambda i, j: (i, 0, 0))],
        out_specs=pl.BlockSpec((1, cap, d), lambda i, j: (j, i, 0)),
        out_shape=jax.ShapeDtypeStruct((n_experts, b * cap, d), BF16),
        compiler_params=_cparams(2, 48),
        name="dispatch",
    )(slot_rows.reshape(b * n_experts, 1, s), h3)


def _combine_kernel(slot_ref, aff_ref, y_ref, x_ref, g_ref, o_ref, acc_ref, *, n_experts, final_norm):
    bi = pl.program_id(0)
    e = pl.program_id(2)
    tt = x_ref.shape[0]
    cap = y_ref.shape[1]

    @pl.when(e == 0)
    def _():
        acc_ref[...] = x_ref[...]

    lane = lax.broadcasted_iota(jnp.int32, (tt, LANES), 1)
    slot_col = jnp.sum(jnp.where(lane == bi * n_experts + e, slot_ref[...], 0.0), axis=-1, keepdims=True)
    aff_col = jnp.sum(jnp.where(lane == e, aff_ref[...], 0.0), axis=-1, keepdims=True)
    j = lax.broadcasted_iota(jnp.int32, (tt, cap), 1).astype(F32)
    onehot = (slot_col == j).astype(BF16)
    acc_ref[...] += jnp.dot(onehot, y_ref[0], preferred_element_type=F32) * aff_col

    @pl.when(e == n_experts - 1)
    def _():
        x = acc_ref[...]
        if final_norm:
            inv = lax.rsqrt(jnp.mean(x * x, axis=-1, keepdims=True) + EPS)
            x = (x * inv) * g_ref[...]
        o_ref[...] = x


def _combine(slot_cols, aff, y, x1, g, *, batch, n_experts, cap, tt, final_norm):
    m, d = x1.shape
    s = m // batch
    tps = s // tt
    assert slot_cols.shape == (s, LANES) and batch * n_experts == LANES
    return pl.pallas_call(
        functools.partial(_combine_kernel, n_experts=n_experts, final_norm=final_norm),
        grid=(batch, tps, n_experts),
        in_specs=[pl.BlockSpec((tt, LANES), lambda b, t, e: (t, 0)),
                  pl.BlockSpec((tt, LANES), lambda b, t, e: (b * tps + t, 0)),
                  pl.BlockSpec((1, cap, d), lambda b, t, e: (e, b, 0)),
                  pl.BlockSpec((tt, d), lambda b, t, e: (b * tps + t, 0)),
                  pl.BlockSpec((1, d), lambda b, t, e: (0, 0))],
        out_specs=pl.BlockSpec((tt, d), lambda b, t, e: (b * tps + t, 0)),
        out_shape=jax.ShapeDtypeStruct((m, d), F32),
        scratch_shapes=[pltpu.VMEM((tt, d), F32)],
        compiler_params=_cparams(3, 48),
        name="combine",
    )(slot_cols, aff, y, x1, g.reshape(1, d))


def _expert_up_kernel(x_ref, wg_ref, wu_ref, o_ref, *, chunk):
    wg = wg_ref[0].astype(BF16)
    wu = wu_ref[0].astype(BF16)
    for r0 in range(0, x_ref.shape[1], chunk):
        x = x_ref[0, pl.ds(r0, chunk), :]
        gate = jnp.dot(x, wg, preferred_element_type=F32)
        up = jnp.dot(x, wu, preferred_element_type=F32)
        o_ref[0, pl.ds(r0, chunk), :] = (jax.nn.silu(gate) * up).astype(o_ref.dtype)


def _expert_up(xin, w_gate, w_up, *, tf):
    e, m, d = xin.shape
    f = w_gate.shape[2]
    return pl.pallas_call(
        functools.partial(_expert_up_kernel, chunk=512),
        grid=(e, f // tf),
        in_specs=[pl.BlockSpec((1, m, d), lambda i, j: (i, 0, 0)),
                  pl.BlockSpec((1, d, tf), lambda i, j: (i, 0, j)),
                  pl.BlockSpec((1, d, tf), lambda i, j: (i, 0, j))],
        out_specs=pl.BlockSpec((1, m, tf), lambda i, j: (i, 0, j)),
        out_shape=jax.ShapeDtypeStruct((e, m, f), BF16),
        compiler_params=_cparams(2, 56),
        name="expert_up",
    )(xin, w_gate, w_up)


def _expert_down_kernel(h_ref, wd_ref, o_ref, *, chunk):
    wd = wd_ref[0].astype(BF16)
    for r0 in range(0, h_ref.shape[1], chunk):
        y = jnp.dot(h_ref[0, pl.ds(r0, chunk), :], wd, preferred_element_type=F32)
        o_ref[0, pl.ds(r0, chunk), :] = y.astype(o_ref.dtype)


def _expert_down(hidden, w_down, *, tn):
    e, m, f = hidden.shape
    d = w_down.shape[2]
    return pl.pallas_call(
        functools.partial(_expert_down_kernel, chunk=512),
        grid=(e, d // tn),
        in_specs=[pl.BlockSpec((1, m, f), lambda i, j: (i, 0, 0)),
                  pl.BlockSpec((1, f, tn), lambda i, j: (i, 0, j))],
        out_specs=pl.BlockSpec((1, m, tn), lambda i, j: (i, 0, j)),
        out_shape=jax.ShapeDtypeStruct((e, m, d), BF16),
        compiler_params=_cparams(2, 56),
        name="expert_down",
    )(hidden, w_down)


def kernel(x, mem, norm_mix_g, norm_mem_g, w_in, pool_w, pool_scale, fourier_w, w_kv_mem, proj_pool, proj_fourier,
           proj_mem, w_out, norm_ffn_g, w_router, w_expert_gate, w_expert_up, w_expert_down, norm_final_g):
    b, s, d = x.shape
    depth = w_in.shape[0]
    n_tok = b * s
    pool_width = pool_w.shape[1] * pool_w.shape[2]
    four_width = fourier_w.shape[1] * fourier_w.shape[2]
    mem_width = proj_mem.shape[1]
    mix_width = pool_width + four_width + mem_width
    n_heads = 4
    e = w_router.shape[2]
    cap = CAPACITY_FACTOR * s // e

    xf = x.reshape(n_tok, d)
    for l in range(depth):
        h = _rmsnorm(xf, norm_mix_g[l], tm=512, out_dtype=BF16)
        z = _matmul(h, w_in[l], n_cols=mix_width, tm=1024, tn=512, out_dtype=F32, name="in_proj")
        z3 = z.reshape(b, s, mix_width)
        ya = _pool_mixer(z3, pool_w[l], pool_scale[l])
        yb = _fourier_mixer(z3, fourier_w[l], col_block=pool_width // four_width, tk=512)
        memn = _rmsnorm(mem.reshape(-1, d), norm_mem_g[l], tm=512, out_dtype=BF16)
        kv = _matmul(memn, w_kv_mem[l], n_cols=2 * mem_width, tm=1024, tn=512, out_dtype=BF16, name="kv_proj")
        yc = _mem_attention(z3, kv.reshape(b, -1, 2 * mem_width),
                            col_block=(pool_width + four_width) // mem_width, n_heads=n_heads, tm=512)
        merged = _gated_merge(ya.reshape(n_tok, -1), yb.reshape(n_tok, -1), yc.reshape(n_tok, -1), h,
                              proj_pool[l], proj_fourier[l], proj_mem[l], w_in[l],
                              gate_col0=mix_width, tm=1024, tn=256)
        x1 = _matmul(merged, w_out[l], n_cols=d, tm=1024, tn=512, out_dtype=F32, residual=xf, name="out_proj")

        h2, aff, aff_t = _norm_router(x1, norm_ffn_g[l], w_router[l], batch=b, tm=512)
        slot_rows, slot_cols = _select(aff_t.reshape(b * e, s), cap=cap)
        xin = _dispatch(slot_rows, h2.reshape(b, s, d), n_experts=e, cap=cap)
        hidden = _expert_up(xin, w_expert_gate[l], w_expert_up[l], tf=512)
        y = _expert_down(hidden, w_expert_down[l], tn=512)
        last = l + 1 == depth
        xf = _combine(slot_cols, aff, y, x1, norm_final_g, batch=b, n_experts=e, cap=cap, tt=512, final_norm=last)
    return xf.reshape(b, s, d)
```

```python
import functools

import numpy as np
import jax
import jax.numpy as jnp
from jax import lax
from jax.experimental import pallas as pl
from jax.experimental.pallas import tpu as pltpu

F32 = jnp.float32
BF16 = jnp.bfloat16

EPS = 1e-6
POOL_WINDOWS = (2, 4, 8, 16)
N_EXPERTS = 16
CAPACITY_FACTOR = 2
LANES = 128
POOL_HALO = 16
MIB = 1024 * 1024


def _cparams(n_axes, vmem_mib):
    return pltpu.CompilerParams(
        dimension_semantics=("arbitrary",) * n_axes,
        vmem_limit_bytes=vmem_mib * MIB,
    )


def _rmsnorm_kernel(x_ref, g_ref, o_ref):
    x = x_ref[...]
    inv = lax.rsqrt(jnp.mean(x * x, axis=-1, keepdims=True) + EPS)
    o_ref[...] = ((x * inv) * g_ref[...]).astype(o_ref.dtype)


def _rmsnorm(x, g, *, tm, out_dtype):
    m, d = x.shape
    return pl.pallas_call(
        _rmsnorm_kernel,
        grid=(m // tm,),
        in_specs=[pl.BlockSpec((tm, d), lambda i: (i, 0)), pl.BlockSpec((1, d), lambda i: (0, 0))],
        out_specs=pl.BlockSpec((tm, d), lambda i: (i, 0)),
        out_shape=jax.ShapeDtypeStruct((m, d), out_dtype),
        compiler_params=_cparams(1, 40),
        name="rmsnorm",
    )(x, g.reshape(1, d))


def _mm_kernel(a_ref, w_ref, o_ref, wc_ref):
    @pl.when(pl.program_id(1) == 0)
    def _():
        wc_ref[...] = w_ref[...].astype(BF16)

    o_ref[...] = jnp.dot(a_ref[...], wc_ref[...], preferred_element_type=F32).astype(o_ref.dtype)


def _mm_res_kernel(a_ref, w_ref, r_ref, o_ref, wc_ref):
    @pl.when(pl.program_id(1) == 0)
    def _():
        wc_ref[...] = w_ref[...].astype(BF16)

    o_ref[...] = r_ref[...] + jnp.dot(a_ref[...], wc_ref[...], preferred_element_type=F32)


def _matmul(a, w, *, n_cols, tm, tn, out_dtype, residual=None, name):
    m, k = a.shape
    in_specs = [pl.BlockSpec((tm, k), lambda n, i: (i, 0)), pl.BlockSpec((k, tn), lambda n, i: (0, n))]
    args = [a, w]
    kern = _mm_kernel
    if residual is not None:
        in_specs.append(pl.BlockSpec((tm, tn), lambda n, i: (i, n)))
        args.append(residual)
        kern = _mm_res_kernel
    return pl.pallas_call(
        kern,
        grid=(n_cols // tn, m // tm),
        in_specs=in_specs,
        out_specs=pl.BlockSpec((tm, tn), lambda n, i: (i, n)),
        out_shape=jax.ShapeDtypeStruct((m, n_cols), out_dtype),
        scratch_shapes=[pltpu.VMEM((k, tn), BF16)],
        compiler_params=_cparams(2, 48),
        name=name,
    )(*args)


def _pool_kernel(u_ref, pw_ref, ps_ref, o_ref, pad_ref, *, chunk):
    s = u_ref.shape[1]
    c = pw_ref.shape[1]
    h = POOL_HALO
    zeros = jnp.zeros((h, c), F32)
    pad_ref[pl.ds(0, h), :] = zeros
    pad_ref[pl.ds(h + s, h), :] = zeros
    for g, w in enumerate(POOL_WINDOWS):
        cols = slice(g * c, (g + 1) * c)
        pad_ref[pl.ds(h, s), :] = u_ref[0, :, cols]
        wg = pw_ref[g].astype(BF16)
        scale = ps_ref[:, cols]
        for r0 in range(0, s, chunk):
            pos = r0 + lax.broadcasted_iota(jnp.int32, (chunk, 1), 0)
            lo = jnp.maximum(pos - w // 2, 0)
            hi = jnp.minimum(pos + (w - w // 2), s)
            cnt = (hi - lo).astype(F32)
            tot = pad_ref[pl.ds(h + r0 - w // 2, chunk), :]
            for k in range(1, w):
                tot = tot + pad_ref[pl.ds(h + r0 - w // 2 + k, chunk), :]
            pooled = tot / cnt - pad_ref[pl.ds(h + r0, chunk), :]
            y = jnp.dot(pooled.astype(BF16), wg, preferred_element_type=F32)
            o_ref[0, pl.ds(r0, chunk), cols] = (y * scale).astype(o_ref.dtype)


def _pool_mixer(z3, pool_w, pool_scale):
    b, s, _ = z3.shape
    g, c, _ = pool_w.shape
    width = g * c
    return pl.pallas_call(
        functools.partial(_pool_kernel, chunk=256),
        grid=(b,),
        in_specs=[pl.BlockSpec((1, s, width), lambda i: (i, 0, 0)),
                  pl.BlockSpec((g, c, c), lambda i: (0, 0, 0)),
                  pl.BlockSpec((1, width), lambda i: (0, 0))],
        out_specs=pl.BlockSpec((1, s, width), lambda i: (i, 0, 0)),
        out_shape=jax.ShapeDtypeStruct((b, s, width), BF16),
        scratch_shapes=[pltpu.VMEM((s + 2 * POOL_HALO, c), F32)],
        compiler_params=_cparams(1, 48),
        name="pool_mixer",
    )(z3, pool_w, pool_scale.reshape(1, width))


def _dft_tables(s, c):
    def tab(n):
        r = np.outer(np.arange(n), np.arange(n)) % n
        ang = r * (2.0 * np.pi / n)
        return np.cos(ang), np.sin(ang)
    cs, ss = tab(s)
    cc, sc = tab(c)
    return (jnp.asarray(np.concatenate([cs, -ss], axis=1), dtype=F32).astype(BF16),
            jnp.asarray(np.concatenate([cc, sc], axis=1), dtype=F32).astype(BF16))


def _fourier_kernel(u_ref, dft_ref, ccsc_ref, fw_ref, o_ref, t_ref, *, norm):
    s = u_ref.shape[1]
    ng, c, _ = fw_ref.shape

    @pl.when(pl.program_id(1) == 0)
    def _():
        for g in range(ng):
            ug = u_ref[0, :, g * c:(g + 1) * c].astype(BF16)
            ab = jnp.dot(ug, ccsc_ref[...], preferred_element_type=F32)
            t_ref[pl.ds(0, s), g * c:(g + 1) * c] = ab[:, :c].astype(BF16)
            t_ref[pl.ds(s, s), g * c:(g + 1) * c] = ab[:, c:].astype(BF16)

    f = jnp.dot(dft_ref[...], t_ref[...], preferred_element_type=F32) * norm
    for g in range(ng):
        y = jnp.dot(f[:, g * c:(g + 1) * c].astype(BF16), fw_ref[g].astype(BF16), preferred_element_type=F32)
        o_ref[0, :, g * c:(g + 1) * c] = y.astype(o_ref.dtype)


def _fourier_mixer(z3, fourier_w, *, col_block, tk):
    b, s, _ = z3.shape
    ng, c, _ = fourier_w.shape
    width = ng * c
    dft, ccsc = _dft_tables(s, c)
    return pl.pallas_call(
        functools.partial(_fourier_kernel, norm=float((s * c) ** -0.5)),
        grid=(b, s // tk),
        in_specs=[pl.BlockSpec((1, s, width), lambda i, k: (i, 0, col_block)),
                  pl.BlockSpec((tk, 2 * s), lambda i, k: (k, 0)),
                  pl.BlockSpec((c, 2 * c), lambda i, k: (0, 0)),
                  pl.BlockSpec((ng, c, c), lambda i, k: (0, 0, 0))],
        out_specs=pl.BlockSpec((1, tk, width), lambda i, k: (i, k, 0)),
        out_shape=jax.ShapeDtypeStruct((b, s, width), BF16),
        scratch_shapes=[pltpu.VMEM((2 * s, width), BF16)],
        compiler_params=_cparams(2, 48),
        name="fourier_mixer",
    )(z3, dft, ccsc, fourier_w)


def _attn_kernel(q_ref, kv_ref, o_ref, *, n_heads, scale):
    dh = q_ref.shape[2] // n_heads
    width = n_heads * dh
    for h in range(n_heads):
        q = q_ref[0, :, h * dh:(h + 1) * dh].astype(BF16)
        k = kv_ref[0, :, h * dh:(h + 1) * dh]
        v = kv_ref[0, :, width + h * dh:width + (h + 1) * dh]
        sc = lax.dot_general(q, k, (((1,), (1,)), ((), ())), preferred_element_type=F32) * scale
        mx = jnp.max(sc, axis=-1, keepdims=True)
        p = jnp.exp(sc - mx)
        p = p / jnp.sum(p, axis=-1, keepdims=True)
        o = jnp.dot(p.astype(BF16), v, preferred_element_type=F32)
        o_ref[0, :, h * dh:(h + 1) * dh] = o.astype(o_ref.dtype)


def _mem_attention(z3, kv3, *, col_block, n_heads, tm):
    b, s, _ = z3.shape
    _, m, kvw = kv3.shape
    width = kvw // 2
    return pl.pallas_call(
        functools.partial(_attn_kernel, n_heads=n_heads, scale=float((width // n_heads) ** -0.5)),
        grid=(b, s // tm),
        in_specs=[pl.BlockSpec((1, tm, width), lambda i, j: (i, j, col_block)),
                  pl.BlockSpec((1, m, kvw), lambda i, j: (i, 0, 0))],
        out_specs=pl.BlockSpec((1, tm, width), lambda i, j: (i, j, 0)),
        out_shape=jax.ShapeDtypeStruct((b, s, width), BF16),
        compiler_params=_cparams(2, 40),
        name="mem_attention",
    )(z3, kv3)


def _merge_kernel(ya_ref, yb_ref, yc_ref, h_ref, pp_ref, pf_ref, pm_ref, g0_ref, g1_ref, g2_ref, o_ref,
                  cpp, cpf, cpm, cg0, cg1, cg2):
    @pl.when(pl.program_id(1) == 0)
    def _():
        for src, dst in ((pp_ref, cpp), (pf_ref, cpf), (pm_ref, cpm), (g0_ref, cg0), (g1_ref, cg1), (g2_ref, cg2)):
            dst[...] = src[...].astype(BF16)

    h = h_ref[...]

    def branch(y_ref, proj, gate_w):
        gate = jax.nn.sigmoid(jnp.dot(h, gate_w[...], preferred_element_type=F32))
        return gate * jnp.dot(y_ref[...], proj[...], preferred_element_type=F32)

    acc = branch(ya_ref, cpp, cg0)
    acc = acc + branch(yb_ref, cpf, cg1)
    acc = acc + branch(yc_ref, cpm, cg2)
    o_ref[...] = acc.astype(o_ref.dtype)


def _gated_merge(ya, yb, yc, h, proj_pool, proj_fourier, proj_mem, w_in, *, gate_col0, tm, tn):
    m, d = h.shape
    gb = gate_col0 // tn
    nb = d // tn
    act = lambda width: pl.BlockSpec((tm, width), lambda n, i: (i, 0))
    wsp = lambda rows, off: pl.BlockSpec((rows, tn), lambda n, i: (0, n + off))
    return pl.pallas_call(
        _merge_kernel,
        grid=(d // tn, m // tm),
        in_specs=[act(ya.shape[1]), act(yb.shape[1]), act(yc.shape[1]), act(d),
                  wsp(proj_pool.shape[0], 0), wsp(proj_fourier.shape[0], 0), wsp(proj_mem.shape[0], 0),
                  wsp(d, gb), wsp(d, gb + nb), wsp(d, gb + 2 * nb)],
        out_specs=pl.BlockSpec((tm, tn), lambda n, i: (i, n)),
        out_shape=jax.ShapeDtypeStruct((m, d), BF16),
        scratch_shapes=[pltpu.VMEM((proj_pool.shape[0], tn), BF16), pltpu.VMEM((proj_fourier.shape[0], tn), BF16),
                        pltpu.VMEM((proj_mem.shape[0], tn), BF16), pltpu.VMEM((d, tn), BF16),
                        pltpu.VMEM((d, tn), BF16), pltpu.VMEM((d, tn), BF16)],
        compiler_params=_cparams(2, 56),
        name="gated_merge",
    )(ya, yb, yc, h, proj_pool, proj_fourier, proj_mem, w_in, w_in, w_in)


def _norm_router_kernel(x_ref, g_ref, wr_ref, h_ref, afft_ref, *, n_experts):
    x = x_ref[...]
    inv = lax.rsqrt(jnp.mean(x * x, axis=-1, keepdims=True) + EPS)
    hb = ((x * inv) * g_ref[...]).astype(BF16)
    h_ref[...] = hb
    logits = jnp.dot(hb, wr_ref[...].astype(BF16), preferred_element_type=F32)
    lane = lax.broadcasted_iota(jnp.int32, logits.shape, 1)
    logits = jnp.where(lane < n_experts, logits, -1e30)
    mx = jnp.max(logits, axis=-1, keepdims=True)
    p = jnp.exp(logits - mx)
    aff = p / jnp.sum(p, axis=-1, keepdims=True)
    afft_ref[0] = aff.T[:n_experts, :]


def _norm_router(x, g, w_router, *, batch, tm):
    m, d = x.shape
    e = w_router.shape[1]
    s = m // batch
    tps = s // tm
    wr = jnp.pad(w_router, ((0, 0), (0, LANES - e)))
    return pl.pallas_call(
        functools.partial(_norm_router_kernel, n_experts=e),
        grid=(m // tm,),
        in_specs=[pl.BlockSpec((tm, d), lambda i: (i, 0)), pl.BlockSpec((1, d), lambda i: (0, 0)),
                  pl.BlockSpec((d, LANES), lambda i: (0, 0))],
        out_specs=[pl.BlockSpec((tm, d), lambda i: (i, 0)),
                   pl.BlockSpec((1, e, tm), lambda i: (i // tps, 0, i % tps))],
        out_shape=[jax.ShapeDtypeStruct((m, d), BF16), jax.ShapeDtypeStruct((batch, e, s), F32)],
        compiler_params=_cparams(1, 40),
        name="norm_router",
    )(x, g.reshape(1, d), wr)


def _select_kernel(aff_ref, tri_ref, slot_ref, slot_t_ref, *, cap):
    a = aff_ref[...]
    rows = a.shape[0]
    capf = float(cap)

    def count(pred):
        return jnp.sum(pred.astype(F32), axis=-1, keepdims=True)

    def body(i, t_bits):
        cand = t_bits | jnp.left_shift(jnp.int32(1), 30 - i)
        return jnp.where(count(a >= pltpu.bitcast(cand, F32)) >= capf, cand, t_bits)

    t = pltpu.bitcast(lax.fori_loop(0, 31, body, jnp.zeros((rows, 1), jnp.int32)), F32)
    gt = a > t
    eq = a == t
    need = capf - count(gt)
    tri = tri_ref[...]
    eq_rank = jnp.dot(eq.astype(BF16), tri, preferred_element_type=F32)
    sel = gt | (eq & (eq_rank < need))
    pos = jnp.dot(sel.astype(BF16), tri, preferred_element_type=F32)
    slot = jnp.where(sel, pos, -1.0)
    slot_ref[...] = slot
    slot_t_ref[...] = slot.T


def _select(aff_rows, *, cap):
    rows, s = aff_rows.shape
    idx = jnp.arange(s, dtype=jnp.int32)
    tri = (idx[:, None] < idx[None, :]).astype(BF16)
    return pl.pallas_call(
        functools.partial(_select_kernel, cap=cap),
        grid=(1,),
        in_specs=[pl.BlockSpec((rows, s), lambda i: (0, 0)), pl.BlockSpec((s, s), lambda i: (0, 0))],
        out_specs=[pl.BlockSpec((rows, s), lambda i: (0, 0)), pl.BlockSpec((s, rows), lambda i: (0, 0))],
        out_shape=[jax.ShapeDtypeStruct((rows, s), F32), jax.ShapeDtypeStruct((s, rows), F32)],
        compiler_params=_cparams(1, 48),
        name="expert_select",
    )(aff_rows, tri)


def _dispatch_kernel(slot_ref, aff_ref, h_ref, o_ref, oa_ref):
    cap = o_ref.shape[1]
    s = h_ref.shape[1]
    slot = slot_ref[0]
    j = lax.broadcasted_iota(jnp.int32, (cap, s), 0).astype(F32)
    hit = slot == j
    o_ref[0] = jnp.dot(hit.astype(BF16), h_ref[0], preferred_element_type=F32).astype(o_ref.dtype)
    oa_ref[0] = jnp.sum(jnp.where(hit, aff_ref[0], 0.0), axis=-1, keepdims=True)


def _dispatch(slot_rows, aff_rows, h3, *, n_experts, cap):
    b, s, d = h3.shape
    row = pl.BlockSpec((1, 1, s), lambda i, j: (i * n_experts + j, 0, 0))
    return pl.pallas_call(
        _dispatch_kernel,
        grid=(b, n_experts),
        in_specs=[row, row, pl.BlockSpec((1, s, d), lambda i, j: (i, 0, 0))],
        out_specs=[pl.BlockSpec((1, cap, d), lambda i, j: (j, i, 0)),
                   pl.BlockSpec((1, cap, 1), lambda i, j: (j, i, 0))],
        out_shape=[jax.ShapeDtypeStruct((n_experts, b * cap, d), BF16),
                   jax.ShapeDtypeStruct((n_experts, b * cap, 1), F32)],
        compiler_params=_cparams(2, 48),
        name="dispatch",
    )(slot_rows.reshape(b * n_experts, 1, s), aff_rows.reshape(b * n_experts, 1, s), h3)


def _combine_kernel(slot_ref, y_ref, x_ref, g_ref, o_ref, acc_ref, *, n_experts, final_norm):
    bi = pl.program_id(0)
    eh = pl.program_id(2)
    tt, d = x_ref.shape
    eg, cap, _ = y_ref.shape

    lane = lax.broadcasted_iota(jnp.int32, (tt, LANES), 1)
    slots = slot_ref[...]
    j = lax.broadcasted_iota(jnp.int32, (tt, cap), 1).astype(F32)
    pieces = []
    for k in range(eg):
        col = jnp.sum(jnp.where(lane == bi * n_experts + eh * eg + k, slots, 0.0), axis=-1, keepdims=True)
        pieces.append((col == j).astype(BF16))
    onehot = jnp.concatenate(pieces, axis=1)
    contrib = jnp.dot(onehot, y_ref[...].reshape(eg * cap, d), preferred_element_type=F32)

    @pl.when(eh == 0)
    def _():
        acc_ref[...] = x_ref[...] + contrib

    @pl.when(eh > 0)
    def _():
        acc_ref[...] += contrib

    @pl.when(eh == n_experts // eg - 1)
    def _():
        x = acc_ref[...]
        if final_norm:
            inv = lax.rsqrt(jnp.mean(x * x, axis=-1, keepdims=True) + EPS)
            x = (x * inv) * g_ref[...]
        o_ref[...] = x


def _combine(slot_cols, y, x1, g, *, batch, n_experts, cap, tt, eg, final_norm):
    m, d = x1.shape
    s = m // batch
    tps = s // tt
    assert slot_cols.shape == (s, LANES) and batch * n_experts == LANES
    return pl.pallas_call(
        functools.partial(_combine_kernel, n_experts=n_experts, final_norm=final_norm),
        grid=(batch, tps, n_experts // eg),
        in_specs=[pl.BlockSpec((tt, LANES), lambda b, t, e: (t, 0)),
                  pl.BlockSpec((eg, cap, d), lambda b, t, e: (e, b, 0)),
                  pl.BlockSpec((tt, d), lambda b, t, e: (b * tps + t, 0)),
                  pl.BlockSpec((1, d), lambda b, t, e: (0, 0))],
        out_specs=pl.BlockSpec((tt, d), lambda b, t, e: (b * tps + t, 0)),
        out_shape=jax.ShapeDtypeStruct((m, d), F32),
        scratch_shapes=[pltpu.VMEM((tt, d), F32)],
        compiler_params=_cparams(3, 56),
        name="combine",
    )(slot_cols, y, x1, g.reshape(1, d))


def _expert_up_kernel(x_ref, wg_ref, wu_ref, o_ref, *, chunk):
    wg = wg_ref[0].astype(BF16)
    wu = wu_ref[0].astype(BF16)
    for r0 in range(0, x_ref.shape[1], chunk):
        x = x_ref[0, pl.ds(r0, chunk), :]
        gate = jnp.dot(x, wg, preferred_element_type=F32)
        up = jnp.dot(x, wu, preferred_element_type=F32)
        o_ref[0, pl.ds(r0, chunk), :] = (jax.nn.silu(gate) * up).astype(o_ref.dtype)


def _expert_up(xin, w_gate, w_up, *, tf):
    e, m, d = xin.shape
    f = w_gate.shape[2]
    return pl.pallas_call(
        functools.partial(_expert_up_kernel, chunk=512),
        grid=(e, f // tf),
        in_specs=[pl.BlockSpec((1, m, d), lambda i, j: (i, 0, 0)),
                  pl.BlockSpec((1, d, tf), lambda i, j: (i, 0, j)),
                  pl.BlockSpec((1, d, tf), lambda i, j: (i, 0, j))],
        out_specs=pl.BlockSpec((1, m, tf), lambda i, j: (i, 0, j)),
        out_shape=jax.ShapeDtypeStruct((e, m, f), BF16),
        compiler_params=_cparams(2, 56),
        name="expert_up",
    )(xin, w_gate, w_up)


def _expert_down_kernel(h_ref, wd_ref, a_ref, o_ref, *, chunk):
    wd = wd_ref[0].astype(BF16)
    for r0 in range(0, h_ref.shape[1], chunk):
        y = jnp.dot(h_ref[0, pl.ds(r0, chunk), :], wd, preferred_element_type=F32)
        o_ref[0, pl.ds(r0, chunk), :] = (y * a_ref[0, pl.ds(r0, chunk), :]).astype(o_ref.dtype)


def _expert_down(hidden, w_down, aff, *, tn):
    e, m, f = hidden.shape
    d = w_down.shape[2]
    return pl.pallas_call(
        functools.partial(_expert_down_kernel, chunk=512),
        grid=(e, d // tn),
        in_specs=[pl.BlockSpec((1, m, f), lambda i, j: (i, 0, 0)),
                  pl.BlockSpec((1, f, tn), lambda i, j: (i, 0, j)),
                  pl.BlockSpec((1, m, 1), lambda i, j: (i, 0, 0))],
        out_specs=pl.BlockSpec((1, m, tn), lambda i, j: (i, 0, j)),
        out_shape=jax.ShapeDtypeStruct((e, m, d), BF16),
        compiler_params=_cparams(2, 56),
        name="expert_down",
    )(hidden, w_down, aff)


def kernel(x, mem, norm_mix_g, norm_mem_g, w_in, pool_w, pool_scale, fourier_w, w_kv_mem, proj_pool, proj_fourier,
           proj_mem, w_out, norm_ffn_g, w_router, w_expert_gate, w_expert_up, w_expert_down, norm_final_g):
    b, s, d = x.shape
    depth = w_in.shape[0]
    n_tok = b * s
    pool_width = pool_w.shape[1] * pool_w.shape[2]
    four_width = fourier_w.shape[1] * fourier_w.shape[2]
    mem_width = proj_mem.shape[1]
    mix_width = pool_width + four_width + mem_width
    n_heads = 4
    e = w_router.shape[2]
    cap = CAPACITY_FACTOR * s // e

    xf = x.reshape(n_tok, d)
    for l in range(depth):
        h = _rmsnorm(xf, norm_mix_g[l], tm=512, out_dtype=BF16)
        z = _matmul(h, w_in[l], n_cols=mix_width, tm=1024, tn=512, out_dtype=F32, name="in_proj")
        z3 = z.reshape(b, s, mix_width)
        ya = _pool_mixer(z3, pool_w[l], pool_scale[l])
        yb = _fourier_mixer(z3, fourier_w[l], col_block=pool_width // four_width, tk=512)
        memn = _rmsnorm(mem.reshape(-1, d), norm_mem_g[l], tm=512, out_dtype=BF16)
        kv = _matmul(memn, w_kv_mem[l], n_cols=2 * mem_width, tm=1024, tn=512, out_dtype=BF16, name="kv_proj")
        yc = _mem_attention(z3, kv.reshape(b, -1, 2 * mem_width),
                            col_block=(pool_width + four_width) // mem_width, n_heads=n_heads, tm=512)
        merged = _gated_merge(ya.reshape(n_tok, -1), yb.reshape(n_tok, -1), yc.reshape(n_tok, -1), h,
                              proj_pool[l], proj_fourier[l], proj_mem[l], w_in[l],
                              gate_col0=mix_width, tm=1024, tn=256)
        x1 = _matmul(merged, w_out[l], n_cols=d, tm=1024, tn=512, out_dtype=F32, residual=xf, name="out_proj")

        h2, aff_t = _norm_router(x1, norm_ffn_g[l], w_router[l], batch=b, tm=512)
        slot_rows, slot_cols = _select(aff_t.reshape(b * e, s), cap=cap)
        xin, aff_slot = _dispatch(slot_rows, aff_t, h2.reshape(b, s, d), n_experts=e, cap=cap)
        hidden = _expert_up(xin, w_expert_gate[l], w_expert_up[l], tf=512)
        y = _expert_down(hidden, w_expert_down[l], aff_slot, tn=512)
        last = l + 1 == depth
        xf = _combine(slot_cols, y, x1, norm_final_g, batch=b, n_experts=e, cap=cap, tt=512, eg=8, final_norm=last)
    return xf.reshape(b, s, d)
```

```python
import functools

import numpy as np
import jax
import jax.numpy as jnp
from jax import lax
from jax.experimental import pallas as pl
from jax.experimental.pallas import tpu as pltpu

F32 = jnp.float32
BF16 = jnp.bfloat16

EPS = 1e-6
POOL_WINDOWS = (2, 4, 8, 16)
N_EXPERTS = 16
CAPACITY_FACTOR = 2
LANES = 128
POOL_HALO = 16
MIB = 1024 * 1024


def _cparams(n_axes, vmem_mib):
    return pltpu.CompilerParams(
        dimension_semantics=("arbitrary",) * n_axes,
        vmem_limit_bytes=vmem_mib * MIB,
    )


def _rmsnorm_kernel(x_ref, g_ref, o_ref):
    x = x_ref[...]
    inv = lax.rsqrt(jnp.mean(x * x, axis=-1, keepdims=True) + EPS)
    o_ref[...] = ((x * inv) * g_ref[...]).astype(o_ref.dtype)


def _rmsnorm(x, g, *, tm, out_dtype):
    m, d = x.shape
    return pl.pallas_call(
        _rmsnorm_kernel,
        grid=(m // tm,),
        in_specs=[pl.BlockSpec((tm, d), lambda i: (i, 0)), pl.BlockSpec((1, d), lambda i: (0, 0))],
        out_specs=pl.BlockSpec((tm, d), lambda i: (i, 0)),
        out_shape=jax.ShapeDtypeStruct((m, d), out_dtype),
        compiler_params=_cparams(1, 40),
        name="rmsnorm",
    )(x, g.reshape(1, d))


def _mm_kernel(a_ref, w_ref, o_ref, wc_ref):
    @pl.when(pl.program_id(1) == 0)
    def _():
        wc_ref[...] = w_ref[...].astype(BF16)

    o_ref[...] = jnp.dot(a_ref[...], wc_ref[...], preferred_element_type=F32).astype(o_ref.dtype)


def _matmul(a, w, *, n_cols, tm, tn, out_dtype, name):
    m, k = a.shape
    return pl.pallas_call(
        _mm_kernel,
        grid=(n_cols // tn, m // tm),
        in_specs=[pl.BlockSpec((tm, k), lambda n, i: (i, 0)), pl.BlockSpec((k, tn), lambda n, i: (0, n))],
        out_specs=pl.BlockSpec((tm, tn), lambda n, i: (i, n)),
        out_shape=jax.ShapeDtypeStruct((m, n_cols), out_dtype),
        scratch_shapes=[pltpu.VMEM((k, tn), BF16)],
        compiler_params=_cparams(2, 48),
        name=name,
    )(a, w)


def _in_proj_kernel(x_ref, g_ref, w_ref, h_ref, zp_ref, zq_ref, wc_ref, *, chunk):
    @pl.when(pl.program_id(0) == 0)
    def _():
        wc_ref[...] = w_ref[...].astype(BF16)

    x = x_ref[...]
    inv = lax.rsqrt(jnp.mean(x * x, axis=-1, keepdims=True) + EPS)
    hb = ((x * inv) * g_ref[...]).astype(BF16)
    h_ref[...] = hb
    wp = zp_ref.shape[1]
    for c0 in range(0, wc_ref.shape[1], chunk):
        z = jnp.dot(hb, wc_ref[:, c0:c0 + chunk], preferred_element_type=F32)
        if c0 < wp:
            zp_ref[:, c0:c0 + chunk] = z
        else:
            zq_ref[:, c0 - wp:c0 - wp + chunk] = z.astype(BF16)


def _in_proj(x, g, w_in, *, pool_width, mix_width, tm):
    m, d = x.shape
    row = lambda width: pl.BlockSpec((tm, width), lambda i: (i, 0))
    return pl.pallas_call(
        functools.partial(_in_proj_kernel, chunk=512),
        grid=(m // tm,),
        in_specs=[row(d), pl.BlockSpec((1, d), lambda i: (0, 0)),
                  pl.BlockSpec((d, mix_width), lambda i: (0, 0), pipeline_mode=pl.Buffered(1))],
        out_specs=[row(d), row(pool_width), row(mix_width - pool_width)],
        out_shape=[jax.ShapeDtypeStruct((m, d), BF16), jax.ShapeDtypeStruct((m, pool_width), F32),
                   jax.ShapeDtypeStruct((m, mix_width - pool_width), BF16)],
        scratch_shapes=[pltpu.VMEM((d, mix_width), BF16)],
        compiler_params=_cparams(1, 56),
        name="in_proj",
    )(x, g.reshape(1, d), w_in)


def _out_proj_kernel(m_ref, w_ref, r_ref, g_ref, wr_ref, x1_ref, h2_ref, afft_ref, wc_ref, *, chunk, n_experts):
    @pl.when(pl.program_id(0) == 0)
    def _():
        wc_ref[...] = w_ref[...].astype(BF16)

    a = m_ref[...]
    for c0 in range(0, wc_ref.shape[1], chunk):
        cols = slice(c0, c0 + chunk)
        x1_ref[:, cols] = r_ref[:, cols] + jnp.dot(a, wc_ref[:, cols], preferred_element_type=F32)

    x = x1_ref[...]
    inv = lax.rsqrt(jnp.mean(x * x, axis=-1, keepdims=True) + EPS)
    hb = ((x * inv) * g_ref[...]).astype(BF16)
    h2_ref[...] = hb
    logits = jnp.dot(hb, wr_ref[...].astype(BF16), preferred_element_type=F32)
    lane = lax.broadcasted_iota(jnp.int32, logits.shape, 1)
    logits = jnp.where(lane < n_experts, logits, -1e30)
    mx = jnp.max(logits, axis=-1, keepdims=True)
    p = jnp.exp(logits - mx)
    aff = p / jnp.sum(p, axis=-1, keepdims=True)
    afft_ref[0] = aff.T[:n_experts, :]


def _out_proj(merged, w_out, resid, g, w_router, *, batch, tm):
    m, d = resid.shape
    e = w_router.shape[1]
    s = m // batch
    tps = s // tm
    wr = jnp.pad(w_router, ((0, 0), (0, LANES - e)))
    row = pl.BlockSpec((tm, d), lambda i: (i, 0))
    return pl.pallas_call(
        functools.partial(_out_proj_kernel, chunk=512, n_experts=e),
        grid=(m // tm,),
        in_specs=[row, pl.BlockSpec((d, d), lambda i: (0, 0), pipeline_mode=pl.Buffered(1)), row,
                  pl.BlockSpec((1, d), lambda i: (0, 0)), pl.BlockSpec((d, LANES), lambda i: (0, 0))],
        out_specs=[row, row, pl.BlockSpec((1, e, tm), lambda i: (i // tps, 0, i % tps))],
        out_shape=[jax.ShapeDtypeStruct((m, d), F32), jax.ShapeDtypeStruct((m, d), BF16),
                   jax.ShapeDtypeStruct((batch, e, s), F32)],
        scratch_shapes=[pltpu.VMEM((d, d), BF16)],
        compiler_params=_cparams(1, 56),
        name="out_proj",
    )(merged, w_out, resid, g.reshape(1, d), wr)


def _pool_kernel(u_ref, pw_ref, ps_ref, o_ref, pad_ref, *, chunk):
    s = u_ref.shape[1]
    c = pw_ref.shape[1]
    h = POOL_HALO
    zeros = jnp.zeros((h, c), F32)
    pad_ref[pl.ds(0, h), :] = zeros
    pad_ref[pl.ds(h + s, h), :] = zeros
    for g, w in enumerate(POOL_WINDOWS):
        cols = slice(g * c, (g + 1) * c)
        pad_ref[pl.ds(h, s), :] = u_ref[0, :, cols]
        wg = pw_ref[g].astype(BF16)
        scale = ps_ref[:, cols]
        for r0 in range(0, s, chunk):
            pos = r0 + lax.broadcasted_iota(jnp.int32, (chunk, 1), 0)
            lo = jnp.maximum(pos - w // 2, 0)
            hi = jnp.minimum(pos + (w - w // 2), s)
            cnt = (hi - lo).astype(F32)
            tot = pad_ref[pl.ds(h + r0 - w // 2, chunk), :]
            for k in range(1, w):
                tot = tot + pad_ref[pl.ds(h + r0 - w // 2 + k, chunk), :]
            pooled = tot / cnt - pad_ref[pl.ds(h + r0, chunk), :]
            y = jnp.dot(pooled.astype(BF16), wg, preferred_element_type=F32)
            o_ref[0, pl.ds(r0, chunk), cols] = (y * scale).astype(o_ref.dtype)


def _pool_mixer(z3, pool_w, pool_scale):
    b, s, _ = z3.shape
    g, c, _ = pool_w.shape
    width = g * c
    return pl.pallas_call(
        functools.partial(_pool_kernel, chunk=256),
        grid=(b,),
        in_specs=[pl.BlockSpec((1, s, width), lambda i: (i, 0, 0)),
                  pl.BlockSpec((g, c, c), lambda i: (0, 0, 0)),
                  pl.BlockSpec((1, width), lambda i: (0, 0))],
        out_specs=pl.BlockSpec((1, s, width), lambda i: (i, 0, 0)),
        out_shape=jax.ShapeDtypeStruct((b, s, width), BF16),
        scratch_shapes=[pltpu.VMEM((s + 2 * POOL_HALO, c), F32)],
        compiler_params=_cparams(1, 48),
        name="pool_mixer",
    )(z3, pool_w, pool_scale.reshape(1, width))


def _dft_tables(s, c):
    def tab(n):
        r = np.outer(np.arange(n), np.arange(n)) % n
        ang = r * (2.0 * np.pi / n)
        return np.cos(ang), np.sin(ang)
    cs, ss = tab(s)
    cc, sc = tab(c)
    return (jnp.asarray(np.concatenate([cs, -ss], axis=1), dtype=F32).astype(BF16),
            jnp.asarray(np.concatenate([cc, sc], axis=1), dtype=F32).astype(BF16))


def _fourier_kernel(u_ref, dft_ref, ccsc_ref, fw_ref, o_ref, t_ref, *, norm):
    s = u_ref.shape[1]
    ng, c, _ = fw_ref.shape

    @pl.when(pl.program_id(1) == 0)
    def _():
        for g in range(ng):
            ug = u_ref[0, :, g * c:(g + 1) * c]
            ab = jnp.dot(ug, ccsc_ref[...], preferred_element_type=F32)
            t_ref[pl.ds(0, s), g * c:(g + 1) * c] = ab[:, :c].astype(BF16)
            t_ref[pl.ds(s, s), g * c:(g + 1) * c] = ab[:, c:].astype(BF16)

    f = jnp.dot(dft_ref[...], t_ref[...], preferred_element_type=F32) * norm
    for g in range(ng):
        y = jnp.dot(f[:, g * c:(g + 1) * c].astype(BF16), fw_ref[g].astype(BF16), preferred_element_type=F32)
        o_ref[0, :, g * c:(g + 1) * c] = y.astype(o_ref.dtype)


def _fourier_mixer(z3, fourier_w, *, col_block, tk):
    b, s, _ = z3.shape
    ng, c, _ = fourier_w.shape
    width = ng * c
    dft, ccsc = _dft_tables(s, c)
    return pl.pallas_call(
        functools.partial(_fourier_kernel, norm=float((s * c) ** -0.5)),
        grid=(b, s // tk),
        in_specs=[pl.BlockSpec((1, s, width), lambda i, k: (i, 0, col_block)),
                  pl.BlockSpec((tk, 2 * s), lambda i, k: (k, 0)),
                  pl.BlockSpec((c, 2 * c), lambda i, k: (0, 0)),
                  pl.BlockSpec((ng, c, c), lambda i, k: (0, 0, 0))],
        out_specs=pl.BlockSpec((1, tk, width), lambda i, k: (i, k, 0)),
        out_shape=jax.ShapeDtypeStruct((b, s, width), BF16),
        scratch_shapes=[pltpu.VMEM((2 * s, width), BF16)],
        compiler_params=_cparams(2, 48),
        name="fourier_mixer",
    )(z3, dft, ccsc, fourier_w)


def _attn_kernel(q_ref, kv_ref, o_ref, *, n_heads, scale):
    dh = q_ref.shape[2] // n_heads
    width = n_heads * dh
    for h in range(n_heads):
        q = q_ref[0, :, h * dh:(h + 1) * dh]
        k = kv_ref[0, :, h * dh:(h + 1) * dh]
        v = kv_ref[0, :, width + h * dh:width + (h + 1) * dh]
        sc = lax.dot_general(q, k, (((1,), (1,)), ((), ())), preferred_element_type=F32) * scale
        mx = jnp.max(sc, axis=-1, keepdims=True)
        p = jnp.exp(sc - mx)
        p = p / jnp.sum(p, axis=-1, keepdims=True)
        o = jnp.dot(p.astype(BF16), v, preferred_element_type=F32)
        o_ref[0, :, h * dh:(h + 1) * dh] = o.astype(o_ref.dtype)


def _mem_attention(z3, kv3, *, col_block, n_heads, tm):
    b, s, _ = z3.shape
    _, m, kvw = kv3.shape
    width = kvw // 2
    return pl.pallas_call(
        functools.partial(_attn_kernel, n_heads=n_heads, scale=float((width // n_heads) ** -0.5)),
        grid=(b, s // tm),
        in_specs=[pl.BlockSpec((1, tm, width), lambda i, j: (i, j, col_block)),
                  pl.BlockSpec((1, m, kvw), lambda i, j: (i, 0, 0))],
        out_specs=pl.BlockSpec((1, tm, width), lambda i, j: (i, j, 0)),
        out_shape=jax.ShapeDtypeStruct((b, s, width), BF16),
        compiler_params=_cparams(2, 40),
        name="mem_attention",
    )(z3, kv3)


def _merge_kernel(ya_ref, yb_ref, yc_ref, h_ref, pp_ref, pf_ref, pm_ref, g0_ref, g1_ref, g2_ref, o_ref,
                  cpp, cpf, cpm, cg0, cg1, cg2):
    @pl.when(pl.program_id(1) == 0)
    def _():
        for src, dst in ((pp_ref, cpp), (pf_ref, cpf), (pm_ref, cpm), (g0_ref, cg0), (g1_ref, cg1), (g2_ref, cg2)):
            dst[...] = src[...].astype(BF16)

    h = h_ref[...]

    def branch(y_ref, proj, gate_w):
        gate = jax.nn.sigmoid(jnp.dot(h, gate_w[...], preferred_element_type=F32))
        return gate * jnp.dot(y_ref[...], proj[...], preferred_element_type=F32)

    acc = branch(ya_ref, cpp, cg0)
    acc = acc + branch(yb_ref, cpf, cg1)
    acc = acc + branch(yc_ref, cpm, cg2)
    o_ref[...] = acc.astype(o_ref.dtype)


def _gated_merge(ya, yb, yc, h, proj_pool, proj_fourier, proj_mem, w_in, *, gate_col0, tm, tn):
    m, d = h.shape
    gb = gate_col0 // tn
    nb = d // tn
    act = lambda width: pl.BlockSpec((tm, width), lambda n, i: (i, 0))
    wsp = lambda rows, off: pl.BlockSpec((rows, tn), lambda n, i: (0, n + off))
    return pl.pallas_call(
        _merge_kernel,
        grid=(d // tn, m // tm),
        in_specs=[act(ya.shape[1]), act(yb.shape[1]), act(yc.shape[1]), act(d),
                  wsp(proj_pool.shape[0], 0), wsp(proj_fourier.shape[0], 0), wsp(proj_mem.shape[0], 0),
                  wsp(d, gb), wsp(d, gb + nb), wsp(d, gb + 2 * nb)],
        out_specs=pl.BlockSpec((tm, tn), lambda n, i: (i, n)),
        out_shape=jax.ShapeDtypeStruct((m, d), BF16),
        scratch_shapes=[pltpu.VMEM((proj_pool.shape[0], tn), BF16), pltpu.VMEM((proj_fourier.shape[0], tn), BF16),
                        pltpu.VMEM((proj_mem.shape[0], tn), BF16), pltpu.VMEM((d, tn), BF16),
                        pltpu.VMEM((d, tn), BF16), pltpu.VMEM((d, tn), BF16)],
        compiler_params=_cparams(2, 56),
        name="gated_merge",
    )(ya, yb, yc, h, proj_pool, proj_fourier, proj_mem, w_in, w_in, w_in)


def _select_kernel(aff_ref, tri_ref, slot_ref, slot_t_ref, *, cap):
    a = aff_ref[...]
    rows = a.shape[0]
    capf = float(cap)

    def count(pred):
        return jnp.sum(pred.astype(F32), axis=-1, keepdims=True)

    def body(i, t_bits):
        cand = t_bits | jnp.left_shift(jnp.int32(1), 30 - i)
        return jnp.where(count(a >= pltpu.bitcast(cand, F32)) >= capf, cand, t_bits)

    t = pltpu.bitcast(lax.fori_loop(0, 31, body, jnp.zeros((rows, 1), jnp.int32)), F32)
    gt = a > t
    eq = a == t
    need = capf - count(gt)
    tri = tri_ref[...]
    eq_rank = jnp.dot(eq.astype(BF16), tri, preferred_element_type=F32)
    sel = gt | (eq & (eq_rank < need))
    pos = jnp.dot(sel.astype(BF16), tri, preferred_element_type=F32)
    slot = jnp.where(sel, pos, -1.0)
    slot_ref[...] = slot
    slot_t_ref[...] = slot.T


def _select(aff_rows, *, cap):
    rows, s = aff_rows.shape
    idx = jnp.arange(s, dtype=jnp.int32)
    tri = (idx[:, None] < idx[None, :]).astype(BF16)
    return pl.pallas_call(
        functools.partial(_select_kernel, cap=cap),
        grid=(1,),
        in_specs=[pl.BlockSpec((rows, s), lambda i: (0, 0)), pl.BlockSpec((s, s), lambda i: (0, 0))],
        out_specs=[pl.BlockSpec((rows, s), lambda i: (0, 0)), pl.BlockSpec((s, rows), lambda i: (0, 0))],
        out_shape=[jax.ShapeDtypeStruct((rows, s), F32), jax.ShapeDtypeStruct((s, rows), F32)],
        compiler_params=_cparams(1, 48),
        name="expert_select",
    )(aff_rows, tri)


def _dispatch_kernel(slot_ref, aff_ref, h_ref, o_ref, oa_ref):
    cap = o_ref.shape[1]
    s = h_ref.shape[1]
    slot = slot_ref[0]
    j = lax.broadcasted_iota(jnp.int32, (cap, s), 0).astype(F32)
    hit = slot == j
    o_ref[0] = jnp.dot(hit.astype(BF16), h_ref[0], preferred_element_type=F32).astype(o_ref.dtype)
    oa_ref[0] = jnp.sum(jnp.where(hit, aff_ref[0], 0.0), axis=-1, keepdims=True)


def _dispatch(slot_rows, aff_rows, h3, *, n_experts, cap):
    b, s, d = h3.shape
    row = pl.BlockSpec((1, 1, s), lambda i, j: (i * n_experts + j, 0, 0))
    return pl.pallas_call(
        _dispatch_kernel,
        grid=(b, n_experts),
        in_specs=[row, row, pl.BlockSpec((1, s, d), lambda i, j: (i, 0, 0))],
        out_specs=[pl.BlockSpec((1, cap, d), lambda i, j: (j, i, 0)),
                   pl.BlockSpec((1, cap, 1), lambda i, j: (j, i, 0))],
        out_shape=[jax.ShapeDtypeStruct((n_experts, b * cap, d), BF16),
                   jax.ShapeDtypeStruct((n_experts, b * cap, 1), F32)],
        compiler_params=_cparams(2, 48),
        name="dispatch",
    )(slot_rows.reshape(b * n_experts, 1, s), aff_rows.reshape(b * n_experts, 1, s), h3)


def _combine_kernel(slot_ref, y_ref, x_ref, g_ref, o_ref, acc_ref, *, n_experts, final_norm):
    bi = pl.program_id(0)
    eh = pl.program_id(2)
    tt, d = x_ref.shape
    eg, cap, _ = y_ref.shape

    lane = lax.broadcasted_iota(jnp.int32, (tt, LANES), 1)
    slots = slot_ref[...]
    j = lax.broadcasted_iota(jnp.int32, (tt, cap), 1).astype(F32)
    pieces = []
    for k in range(eg):
        col = jnp.sum(jnp.where(lane == bi * n_experts + eh * eg + k, slots, 0.0), axis=-1, keepdims=True)
        pieces.append((col == j).astype(BF16))
    onehot = jnp.concatenate(pieces, axis=1)
    contrib = jnp.dot(onehot, y_ref[...].reshape(eg * cap, d), preferred_element_type=F32)

    @pl.when(eh == 0)
    def _():
        acc_ref[...] = x_ref[...] + contrib

    @pl.when(eh > 0)
    def _():
        acc_ref[...] += contrib

    @pl.when(eh == n_experts // eg - 1)
    def _():
        x = acc_ref[...]
        if final_norm:
            inv = lax.rsqrt(jnp.mean(x * x, axis=-1, keepdims=True) + EPS)
            x = (x * inv) * g_ref[...]
        o_ref[...] = x


def _combine(slot_cols, y, x1, g, *, batch, n_experts, cap, tt, eg, final_norm):
    m, d = x1.shape
    s = m // batch
    tps = s // tt
    assert slot_cols.shape == (s, LANES) and batch * n_experts == LANES
    return pl.pallas_call(
        functools.partial(_combine_kernel, n_experts=n_experts, final_norm=final_norm),
        grid=(batch, tps, n_experts // eg),
        in_specs=[pl.BlockSpec((tt, LANES), lambda b, t, e: (t, 0)),
                  pl.BlockSpec((eg, cap, d), lambda b, t, e: (e, b, 0)),
                  pl.BlockSpec((tt, d), lambda b, t, e: (b * tps + t, 0)),
                  pl.BlockSpec((1, d), lambda b, t, e: (0, 0))],
        out_specs=pl.BlockSpec((tt, d), lambda b, t, e: (b * tps + t, 0)),
        out_shape=jax.ShapeDtypeStruct((m, d), F32),
        scratch_shapes=[pltpu.VMEM((tt, d), F32)],
        compiler_params=_cparams(3, 56),
        name="combine",
    )(slot_cols, y, x1, g.reshape(1, d))


def _expert_up_kernel(x_ref, wg_ref, wu_ref, o_ref, *, chunk):
    wg = wg_ref[0].astype(BF16)
    wu = wu_ref[0].astype(BF16)
    for r0 in range(0, x_ref.shape[1], chunk):
        x = x_ref[0, pl.ds(r0, chunk), :]
        gate = jnp.dot(x, wg, preferred_element_type=F32)
        up = jnp.dot(x, wu, preferred_element_type=F32)
        o_ref[0, pl.ds(r0, chunk), :] = (jax.nn.silu(gate) * up).astype(o_ref.dtype)


def _expert_up(xin, w_gate, w_up, *, tf):
    e, m, d = xin.shape
    f = w_gate.shape[2]
    return pl.pallas_call(
        functools.partial(_expert_up_kernel, chunk=512),
        grid=(e, f // tf),
        in_specs=[pl.BlockSpec((1, m, d), lambda i, j: (i, 0, 0)),
                  pl.BlockSpec((1, d, tf), lambda i, j: (i, 0, j)),
                  pl.BlockSpec((1, d, tf), lambda i, j: (i, 0, j))],
        out_specs=pl.BlockSpec((1, m, tf), lambda i, j: (i, 0, j)),
        out_shape=jax.ShapeDtypeStruct((e, m, f), BF16),
        compiler_params=_cparams(2, 56),
        name="expert_up",
    )(xin, w_gate, w_up)


def _expert_down_kernel(h_ref, wd_ref, a_ref, o_ref, *, chunk):
    wd = wd_ref[0].astype(BF16)
    for r0 in range(0, h_ref.shape[1], chunk):
        y = jnp.dot(h_ref[0, pl.ds(r0, chunk), :], wd, preferred_element_type=F32)
        o_ref[0, pl.ds(r0, chunk), :] = (y * a_ref[0, pl.ds(r0, chunk), :]).astype(o_ref.dtype)


def _expert_down(hidden, w_down, aff, *, tn):
    e, m, f = hidden.shape
    d = w_down.shape[2]
    return pl.pallas_call(
        functools.partial(_expert_down_kernel, chunk=512),
        grid=(e, d // tn),
        in_specs=[pl.BlockSpec((1, m, f), lambda i, j: (i, 0, 0)),
                  pl.BlockSpec((1, f, tn), lambda i, j: (i, 0, j)),
                  pl.BlockSpec((1, m, 1), lambda i, j: (i, 0, 0))],
        out_specs=pl.BlockSpec((1, m, tn), lambda i, j: (i, 0, j)),
        out_shape=jax.ShapeDtypeStruct((e, m, d), BF16),
        compiler_params=_cparams(2, 56),
        name="expert_down",
    )(hidden, w_down, aff)


def kernel(x, mem, norm_mix_g, norm_mem_g, w_in, pool_w, pool_scale, fourier_w, w_kv_mem, proj_pool, proj_fourier,
           proj_mem, w_out, norm_ffn_g, w_router, w_expert_gate, w_expert_up, w_expert_down, norm_final_g):
    b, s, d = x.shape
    depth = w_in.shape[0]
    n_tok = b * s
    pool_width = pool_w.shape[1] * pool_w.shape[2]
    four_width = fourier_w.shape[1] * fourier_w.shape[2]
    mem_width = proj_mem.shape[1]
    mix_width = pool_width + four_width + mem_width
    n_heads = 4
    e = w_router.shape[2]
    cap = CAPACITY_FACTOR * s // e

    xf = x.reshape(n_tok, d)
    for l in range(depth):
        h, zp, zq = _in_proj(xf, norm_mix_g[l], w_in[l], pool_width=pool_width, mix_width=mix_width, tm=512)
        zq3 = zq.reshape(b, s, mix_width - pool_width)
        ya = _pool_mixer(zp.reshape(b, s, pool_width), pool_w[l], pool_scale[l])
        yb = _fourier_mixer(zq3, fourier_w[l], col_block=0, tk=512)
        memn = _rmsnorm(mem.reshape(-1, d), norm_mem_g[l], tm=512, out_dtype=BF16)
        kv = _matmul(memn, w_kv_mem[l], n_cols=2 * mem_width, tm=1024, tn=512, out_dtype=BF16, name="kv_proj")
        yc = _mem_attention(zq3, kv.reshape(b, -1, 2 * mem_width), col_block=four_width // mem_width,
                            n_heads=n_heads, tm=512)
        merged = _gated_merge(ya.reshape(n_tok, -1), yb.reshape(n_tok, -1), yc.reshape(n_tok, -1), h,
                              proj_pool[l], proj_fourier[l], proj_mem[l], w_in[l],
                              gate_col0=mix_width, tm=1024, tn=256)

        x1, h2, aff_t = _out_proj(merged, w_out[l], xf, norm_ffn_g[l], w_router[l], batch=b, tm=512)
        slot_rows, slot_cols = _select(aff_t.reshape(b * e, s), cap=cap)
        xin, aff_slot = _dispatch(slot_rows, aff_t, h2.reshape(b, s, d), n_experts=e, cap=cap)
        hidden = _expert_up(xin, w_expert_gate[l], w_expert_up[l], tf=512)
        y = _expert_down(hidden, w_expert_down[l], aff_slot, tn=512)
        last = l + 1 == depth
        xf = _combine(slot_cols, y, x1, norm_final_g, batch=b, n_experts=e, cap=cap, tt=512, eg=8, final_norm=last)
    return xf.reshape(b, s, d)
```

```python
import functools

import numpy as np
import jax
import jax.numpy as jnp
from jax import lax
from jax.experimental import pallas as pl
from jax.experimental.pallas import tpu as pltpu

F32 = jnp.float32
BF16 = jnp.bfloat16

EPS = 1e-6
POOL_WINDOWS = (2, 4, 8, 16)
N_EXPERTS = 16
CAPACITY_FACTOR = 2
LANES = 128
BF16_ROWS = 16
POOL_HALO = 16
ROW_CHUNK = 256
COMBINE_TILE = 512
COMBINE_WINDOW = 128
MIB = 1024 * 1024


def _cparams(n_axes, vmem_mib):
    return pltpu.CompilerParams(
        dimension_semantics=("arbitrary",) * n_axes,
        vmem_limit_bytes=vmem_mib * MIB,
    )


def _rmsnorm_kernel(x_ref, g_ref, o_ref):
    x = x_ref[...]
    inv = lax.rsqrt(jnp.mean(x * x, axis=-1, keepdims=True) + EPS)
    o_ref[...] = ((x * inv) * g_ref[...]).astype(o_ref.dtype)


def _rmsnorm(x, g, *, tm, out_dtype):
    m, d = x.shape
    return pl.pallas_call(
        _rmsnorm_kernel,
        grid=(m // tm,),
        in_specs=[pl.BlockSpec((tm, d), lambda i: (i, 0)), pl.BlockSpec((1, d), lambda i: (0, 0))],
        out_specs=pl.BlockSpec((tm, d), lambda i: (i, 0)),
        out_shape=jax.ShapeDtypeStruct((m, d), out_dtype),
        compiler_params=_cparams(1, 40),
        name="rmsnorm",
    )(x, g.reshape(1, d))


def _mm_kernel(a_ref, w_ref, o_ref, wc_ref):
    @pl.when(pl.program_id(1) == 0)
    def _():
        wc_ref[...] = w_ref[...].astype(BF16)

    o_ref[...] = jnp.dot(a_ref[...], wc_ref[...], preferred_element_type=F32).astype(o_ref.dtype)


def _matmul(a, w, *, n_cols, tm, tn, out_dtype, name):
    m, k = a.shape
    return pl.pallas_call(
        _mm_kernel,
        grid=(n_cols // tn, m // tm),
        in_specs=[pl.BlockSpec((tm, k), lambda n, i: (i, 0)), pl.BlockSpec((k, tn), lambda n, i: (0, n))],
        out_specs=pl.BlockSpec((tm, tn), lambda n, i: (i, n)),
        out_shape=jax.ShapeDtypeStruct((m, n_cols), out_dtype),
        scratch_shapes=[pltpu.VMEM((k, tn), BF16)],
        compiler_params=_cparams(2, 48),
        name=name,
    )(a, w)


def _in_proj_kernel(x_ref, g_ref, w_ref, h_ref, zp_ref, zq_ref, wc_ref, *, chunk):
    @pl.when(pl.program_id(0) == 0)
    def _():
        wc_ref[...] = w_ref[...].astype(BF16)

    wp = zp_ref.shape[1]
    for r0 in range(0, x_ref.shape[0], ROW_CHUNK):
        rs = pl.ds(r0, ROW_CHUNK)
        x = x_ref[rs, :]
        inv = lax.rsqrt(jnp.mean(x * x, axis=-1, keepdims=True) + EPS)
        hb = ((x * inv) * g_ref[...]).astype(BF16)
        h_ref[rs, :] = hb
        for c0 in range(0, wc_ref.shape[1], chunk):
            z = jnp.dot(hb, wc_ref[:, c0:c0 + chunk], preferred_element_type=F32)
            if c0 < wp:
                zp_ref[rs, c0:c0 + chunk] = z
            else:
                zq_ref[rs, c0 - wp:c0 - wp + chunk] = z.astype(BF16)


def _in_proj(x, g, w_in, *, pool_width, mix_width, tm):
    m, d = x.shape
    row = lambda width: pl.BlockSpec((tm, width), lambda i: (i, 0))
    return pl.pallas_call(
        functools.partial(_in_proj_kernel, chunk=512),
        grid=(m // tm,),
        in_specs=[row(d), pl.BlockSpec((1, d), lambda i: (0, 0)),
                  pl.BlockSpec((d, mix_width), lambda i: (0, 0), pipeline_mode=pl.Buffered(1))],
        out_specs=[row(d), row(pool_width), row(mix_width - pool_width)],
        out_shape=[jax.ShapeDtypeStruct((m, d), BF16), jax.ShapeDtypeStruct((m, pool_width), F32),
                   jax.ShapeDtypeStruct((m, mix_width - pool_width), BF16)],
        scratch_shapes=[pltpu.VMEM((d, mix_width), BF16)],
        compiler_params=_cparams(1, 56),
        name="in_proj",
    )(x, g.reshape(1, d), w_in)


def _out_proj_kernel(m_ref, w_ref, r_ref, g_ref, wr_ref, x1_ref, h2_ref, afft_ref, wc_ref, *, chunk, n_experts):
    @pl.when(pl.program_id(0) == 0)
    def _():
        wc_ref[...] = w_ref[...].astype(BF16)

    wr = wr_ref[...].astype(BF16)
    for r0 in range(0, m_ref.shape[0], ROW_CHUNK):
        rs = pl.ds(r0, ROW_CHUNK)
        a = m_ref[rs, :]
        for c0 in range(0, wc_ref.shape[1], chunk):
            cols = slice(c0, c0 + chunk)
            x1_ref[rs, cols] = r_ref[rs, cols] + jnp.dot(a, wc_ref[:, cols], preferred_element_type=F32)

        x = x1_ref[rs, :]
        inv = lax.rsqrt(jnp.mean(x * x, axis=-1, keepdims=True) + EPS)
        hb = ((x * inv) * g_ref[...]).astype(BF16)
        h2_ref[rs, :] = hb
        logits = jnp.dot(hb, wr, preferred_element_type=F32)
        lane = lax.broadcasted_iota(jnp.int32, logits.shape, 1)
        logits = jnp.where(lane < n_experts, logits, -1e30)
        mx = jnp.max(logits, axis=-1, keepdims=True)
        p = jnp.exp(logits - mx)
        aff = p / jnp.sum(p, axis=-1, keepdims=True)
        afft_ref[0, :, rs] = aff.T[:n_experts, :]


def _out_proj(merged, w_out, resid, g, w_router, *, batch, tm):
    m, d = resid.shape
    e = w_router.shape[1]
    s = m // batch
    tps = s // tm
    wr = jnp.pad(w_router, ((0, 0), (0, LANES - e)))
    row = pl.BlockSpec((tm, d), lambda i: (i, 0))
    return pl.pallas_call(
        functools.partial(_out_proj_kernel, chunk=512, n_experts=e),
        grid=(m // tm,),
        in_specs=[row, pl.BlockSpec((d, d), lambda i: (0, 0), pipeline_mode=pl.Buffered(1)), row,
                  pl.BlockSpec((1, d), lambda i: (0, 0)), pl.BlockSpec((d, LANES), lambda i: (0, 0))],
        out_specs=[row, row, pl.BlockSpec((1, e, tm), lambda i: (i // tps, 0, i % tps))],
        out_shape=[jax.ShapeDtypeStruct((m, d), F32), jax.ShapeDtypeStruct((m, d), BF16),
                   jax.ShapeDtypeStruct((batch, e, s), F32)],
        scratch_shapes=[pltpu.VMEM((d, d), BF16)],
        compiler_params=_cparams(1, 56),
        name="out_proj",
    )(merged, w_out, resid, g.reshape(1, d), wr)


def _pool_kernel(u_ref, pw_ref, ps_ref, o_ref, pad_ref, *, chunk):
    s = u_ref.shape[1]
    c = pw_ref.shape[1]
    h = POOL_HALO
    zeros = jnp.zeros((h, c), F32)
    pad_ref[pl.ds(0, h), :] = zeros
    pad_ref[pl.ds(h + s, h), :] = zeros
    for g, w in enumerate(POOL_WINDOWS):
        cols = slice(g * c, (g + 1) * c)
        pad_ref[pl.ds(h, s), :] = u_ref[0, :, cols]
        wg = pw_ref[g].astype(BF16)
        scale = ps_ref[:, cols]
        for r0 in range(0, s, chunk):
            pos = r0 + lax.broadcasted_iota(jnp.int32, (chunk, 1), 0)
            lo = jnp.maximum(pos - w // 2, 0)
            hi = jnp.minimum(pos + (w - w // 2), s)
            cnt = (hi - lo).astype(F32)
            tot = pad_ref[pl.ds(h + r0 - w // 2, chunk), :]
            for k in range(1, w):
                tot = tot + pad_ref[pl.ds(h + r0 - w // 2 + k, chunk), :]
            pooled = tot / cnt - pad_ref[pl.ds(h + r0, chunk), :]
            y = jnp.dot(pooled.astype(BF16), wg, preferred_element_type=F32)
            o_ref[0, pl.ds(r0, chunk), cols] = (y * scale).astype(o_ref.dtype)


def _pool_mixer(z3, pool_w, pool_scale):
    b, s, _ = z3.shape
    g, c, _ = pool_w.shape
    width = g * c
    return pl.pallas_call(
        functools.partial(_pool_kernel, chunk=256),
        grid=(b,),
        in_specs=[pl.BlockSpec((1, s, width), lambda i: (i, 0, 0)),
                  pl.BlockSpec((g, c, c), lambda i: (0, 0, 0)),
                  pl.BlockSpec((1, width), lambda i: (0, 0))],
        out_specs=pl.BlockSpec((1, s, width), lambda i: (i, 0, 0)),
        out_shape=jax.ShapeDtypeStruct((b, s, width), BF16),
        scratch_shapes=[pltpu.VMEM((s + 2 * POOL_HALO, c), F32)],
        compiler_params=_cparams(1, 48),
        name="pool_mixer",
    )(z3, pool_w, pool_scale.reshape(1, width))


def _dft_tables(s, c):
    def tab(n):
        r = np.outer(np.arange(n), np.arange(n)) % n
        ang = r * (2.0 * np.pi / n)
        return np.cos(ang), np.sin(ang)
    cs, ss = tab(s)
    cc, sc = tab(c)
    return (jnp.asarray(np.concatenate([cs, -ss], axis=1), dtype=F32).astype(BF16),
            jnp.asarray(np.concatenate([cc, sc], axis=1), dtype=F32).astype(BF16))


def _fourier_kernel(u_ref, dft_ref, ccsc_ref, fw_ref, o_ref, t_ref, *, norm):
    s = u_ref.shape[1]
    ng, c, _ = fw_ref.shape

    @pl.when(pl.program_id(1) == 0)
    def _():
        for g in range(ng):
            ug = u_ref[0, :, g * c:(g + 1) * c]
            ab = jnp.dot(ug, ccsc_ref[...], preferred_element_type=F32)
            t_ref[pl.ds(0, s), g * c:(g + 1) * c] = ab[:, :c].astype(BF16)
            t_ref[pl.ds(s, s), g * c:(g + 1) * c] = ab[:, c:].astype(BF16)

    f = jnp.dot(dft_ref[...], t_ref[...], preferred_element_type=F32) * norm
    for g in range(ng):
        y = jnp.dot(f[:, g * c:(g + 1) * c].astype(BF16), fw_ref[g].astype(BF16), preferred_element_type=F32)
        o_ref[0, :, g * c:(g + 1) * c] = y.astype(o_ref.dtype)


def _fourier_mixer(z3, fourier_w, *, col_block, tk):
    b, s, _ = z3.shape
    ng, c, _ = fourier_w.shape
    width = ng * c
    dft, ccsc = _dft_tables(s, c)
    return pl.pallas_call(
        functools.partial(_fourier_kernel, norm=float((s * c) ** -0.5)),
        grid=(b, s // tk),
        in_specs=[pl.BlockSpec((1, s, width), lambda i, k: (i, 0, col_block)),
                  pl.BlockSpec((tk, 2 * s), lambda i, k: (k, 0)),
                  pl.BlockSpec((c, 2 * c), lambda i, k: (0, 0)),
                  pl.BlockSpec((ng, c, c), lambda i, k: (0, 0, 0))],
        out_specs=pl.BlockSpec((1, tk, width), lambda i, k: (i, k, 0)),
        out_shape=jax.ShapeDtypeStruct((b, s, width), BF16),
        scratch_shapes=[pltpu.VMEM((2 * s, width), BF16)],
        compiler_params=_cparams(2, 48),
        name="fourier_mixer",
    )(z3, dft, ccsc, fourier_w)


def _attn_kernel(q_ref, kv_ref, o_ref, *, n_heads, scale):
    dh = q_ref.shape[2] // n_heads
    width = n_heads * dh
    for h in range(n_heads):
        q = q_ref[0, :, h * dh:(h + 1) * dh]
        k = kv_ref[0, :, h * dh:(h + 1) * dh]
        v = kv_ref[0, :, width + h * dh:width + (h + 1) * dh]
        sc = lax.dot_general(q, k, (((1,), (1,)), ((), ())), preferred_element_type=F32) * scale
        mx = jnp.max(sc, axis=-1, keepdims=True)
        p = jnp.exp(sc - mx)
        p = p / jnp.sum(p, axis=-1, keepdims=True)
        o = jnp.dot(p.astype(BF16), v, preferred_element_type=F32)
        o_ref[0, :, h * dh:(h + 1) * dh] = o.astype(o_ref.dtype)


def _mem_attention(z3, kv3, *, col_block, n_heads, tm):
    b, s, _ = z3.shape
    _, m, kvw = kv3.shape
    width = kvw // 2
    return pl.pallas_call(
        functools.partial(_attn_kernel, n_heads=n_heads, scale=float((width // n_heads) ** -0.5)),
        grid=(b, s // tm),
        in_specs=[pl.BlockSpec((1, tm, width), lambda i, j: (i, j, col_block)),
                  pl.BlockSpec((1, m, kvw), lambda i, j: (i, 0, 0))],
        out_specs=pl.BlockSpec((1, tm, width), lambda i, j: (i, j, 0)),
        out_shape=jax.ShapeDtypeStruct((b, s, width), BF16),
        compiler_params=_cparams(2, 40),
        name="mem_attention",
    )(z3, kv3)


def _merge_kernel(ya_ref, yb_ref, yc_ref, h_ref, pp_ref, pf_ref, pm_ref, g0_ref, g1_ref, g2_ref, o_ref,
                  cpp, cpf, cpm, cg0, cg1, cg2):
    @pl.when(pl.program_id(1) == 0)
    def _():
        for src, dst in ((pp_ref, cpp), (pf_ref, cpf), (pm_ref, cpm), (g0_ref, cg0), (g1_ref, cg1), (g2_ref, cg2)):
            dst[...] = src[...].astype(BF16)

    h = h_ref[...]

    def branch(y_ref, proj, gate_w):
        gate = jax.nn.sigmoid(jnp.dot(h, gate_w[...], preferred_element_type=F32))
        return gate * jnp.dot(y_ref[...], proj[...], preferred_element_type=F32)

    acc = branch(ya_ref, cpp, cg0)
    acc = acc + branch(yb_ref, cpf, cg1)
    acc = acc + branch(yc_ref, cpm, cg2)
    o_ref[...] = acc.astype(o_ref.dtype)


def _gated_merge(ya, yb, yc, h, proj_pool, proj_fourier, proj_mem, w_in, *, gate_col0, tm, tn):
    m, d = h.shape
    gb = gate_col0 // tn
    nb = d // tn
    act = lambda width: pl.BlockSpec((tm, width), lambda n, i: (i, 0))
    wsp = lambda rows, off: pl.BlockSpec((rows, tn), lambda n, i: (0, n + off))
    return pl.pallas_call(
        _merge_kernel,
        grid=(d // tn, m // tm),
        in_specs=[act(ya.shape[1]), act(yb.shape[1]), act(yc.shape[1]), act(d),
                  wsp(proj_pool.shape[0], 0), wsp(proj_fourier.shape[0], 0), wsp(proj_mem.shape[0], 0),
                  wsp(d, gb), wsp(d, gb + nb), wsp(d, gb + 2 * nb)],
        out_specs=pl.BlockSpec((tm, tn), lambda n, i: (i, n)),
        out_shape=jax.ShapeDtypeStruct((m, d), BF16),
        scratch_shapes=[pltpu.VMEM((proj_pool.shape[0], tn), BF16), pltpu.VMEM((proj_fourier.shape[0], tn), BF16),
                        pltpu.VMEM((proj_mem.shape[0], tn), BF16), pltpu.VMEM((d, tn), BF16),
                        pltpu.VMEM((d, tn), BF16), pltpu.VMEM((d, tn), BF16)],
        compiler_params=_cparams(2, 56),
        name="gated_merge",
    )(ya, yb, yc, h, proj_pool, proj_fourier, proj_mem, w_in, w_in, w_in)


def _select_kernel(aff_ref, tri_ref, slot_ref, slot_t_ref, starts_ref, *, cap, tile):
    a = aff_ref[...]
    rows = a.shape[0]
    capf = float(cap)

    def count(pred):
        return jnp.sum(pred.astype(F32), axis=-1, keepdims=True)

    def body(i, t_bits):
        cand = t_bits | jnp.left_shift(jnp.int32(1), 30 - i)
        return jnp.where(count(a >= pltpu.bitcast(cand, F32)) >= capf, cand, t_bits)

    t = pltpu.bitcast(lax.fori_loop(0, 31, body, jnp.zeros((rows, 1), jnp.int32)), F32)
    gt = a > t
    eq = a == t
    need = capf - count(gt)
    tri = tri_ref[...]
    eq_rank = jnp.dot(eq.astype(BF16), tri, preferred_element_type=F32)
    sel = gt | (eq & (eq_rank < need))
    pos = jnp.dot(sel.astype(BF16), tri, preferred_element_type=F32)
    slot = jnp.where(sel, pos, -1.0)
    slot_ref[...] = slot
    slot_t_ref[...] = slot.T
    s = a.shape[1]
    lane = lax.broadcasted_iota(jnp.int32, (rows, LANES), 1)
    starts = jnp.where(lane == s // tile, capf, 0.0)
    for t in range(s // tile):
        starts = jnp.where(lane == t, pos[:, t * tile:t * tile + 1], starts)
    starts_ref[...] = starts.astype(jnp.int32)


def _select(aff_rows, *, cap, tile):
    rows, s = aff_rows.shape
    idx = jnp.arange(s, dtype=jnp.int32)
    tri = (idx[:, None] < idx[None, :]).astype(BF16)
    full = lambda shape: pl.BlockSpec(shape, lambda i: (0, 0))
    return pl.pallas_call(
        functools.partial(_select_kernel, cap=cap, tile=tile),
        grid=(1,),
        in_specs=[full((rows, s)), full((s, s))],
        out_specs=[full((rows, s)), full((s, rows)), full((rows, LANES))],
        out_shape=[jax.ShapeDtypeStruct((rows, s), F32), jax.ShapeDtypeStruct((s, rows), F32),
                   jax.ShapeDtypeStruct((rows, LANES), jnp.int32)],
        compiler_params=_cparams(1, 48),
        name="expert_select",
    )(aff_rows, tri)


def _dispatch_kernel(slot_ref, aff_ref, h_ref, o_ref, oa_ref):
    cap = o_ref.shape[1]
    s = h_ref.shape[1]
    slot = slot_ref[0]
    j = lax.broadcasted_iota(jnp.int32, (cap, s), 0).astype(F32)
    hit = slot == j
    o_ref[0] = jnp.dot(hit.astype(BF16), h_ref[0], preferred_element_type=F32).astype(o_ref.dtype)
    oa_ref[0] = jnp.sum(jnp.where(hit, aff_ref[0], 0.0), axis=-1, keepdims=True)


def _dispatch(slot_rows, aff_rows, h3, *, n_experts, cap):
    b, s, d = h3.shape
    row = pl.BlockSpec((1, 1, s), lambda i, j: (i * n_experts + j, 0, 0))
    return pl.pallas_call(
        _dispatch_kernel,
        grid=(b, n_experts),
        in_specs=[row, row, pl.BlockSpec((1, s, d), lambda i, j: (i, 0, 0))],
        out_specs=[pl.BlockSpec((1, cap, d), lambda i, j: (j, i, 0)),
                   pl.BlockSpec((1, cap, 1), lambda i, j: (j, i, 0))],
        out_shape=[jax.ShapeDtypeStruct((n_experts, b * cap, d), BF16),
                   jax.ShapeDtypeStruct((n_experts, b * cap, 1), F32)],
        compiler_params=_cparams(2, 48),
        name="dispatch",
    )(slot_rows.reshape(b * n_experts, 1, s), aff_rows.reshape(b * n_experts, 1, s), h3)


def _combine_kernel(starts_ref, slot_ref, y_ref, x_ref, g_ref, o_ref, acc_ref, *, n_experts, win, final_norm):
    bi = pl.program_id(0)
    ti = pl.program_id(1)
    eh = pl.program_id(2)
    tt, d = x_ref.shape
    eg, cap, _ = y_ref.shape

    lane = lax.broadcasted_iota(jnp.int32, (tt, LANES), 1)
    slots = slot_ref[...]
    cols, wins = [], []
    fits = None
    for k in range(eg):
        row = bi * n_experts + eh * eg + k
        lo = starts_ref[row, ti]
        hi = starts_ref[row, ti + 1]
        w0 = jnp.minimum(jnp.bitwise_and(lo, -BF16_ROWS), cap - win)
        ok = hi - w0 <= win
        fits = ok if fits is None else jnp.logical_and(fits, ok)
        wins.append(w0)
        cols.append(jnp.sum(jnp.where(lane == row, slots, 0.0), axis=-1, keepdims=True))

    def accumulate(contrib):
        @pl.when(eh == 0)
        def _():
            acc_ref[...] = x_ref[...] + contrib

        @pl.when(eh > 0)
        def _():
            acc_ref[...] += contrib

    @pl.when(fits)
    def _():
        j = lax.broadcasted_iota(jnp.int32, (tt, win), 1).astype(F32)
        onehot = jnp.concatenate([(cols[k] - wins[k].astype(F32) == j).astype(BF16) for k in range(eg)], axis=1)
        ywin = jnp.concatenate([y_ref[k, pl.ds(pl.multiple_of(wins[k], BF16_ROWS), win), :] for k in range(eg)],
                               axis=0)
        accumulate(jnp.dot(onehot, ywin, preferred_element_type=F32))

    @pl.when(jnp.logical_not(fits))
    def _():
        j = lax.broadcasted_iota(jnp.int32, (tt, cap), 1).astype(F32)
        onehot = jnp.concatenate([(cols[k] == j).astype(BF16) for k in range(eg)], axis=1)
        accumulate(jnp.dot(onehot, y_ref[...].reshape(eg * cap, d), preferred_element_type=F32))

    @pl.when(eh == n_experts // eg - 1)
    def _():
        x = acc_ref[...]
        if final_norm:
            inv = lax.rsqrt(jnp.mean(x * x, axis=-1, keepdims=True) + EPS)
            x = (x * inv) * g_ref[...]
        o_ref[...] = x


def _combine(starts, slot_cols, y, x1, g, *, batch, n_experts, cap, tt, eg, win, final_norm):
    m, d = x1.shape
    s = m // batch
    tps = s // tt
    assert slot_cols.shape == (s, LANES) and batch * n_experts == LANES and win % BF16_ROWS == 0 and win <= cap
    grid_spec = pltpu.PrefetchScalarGridSpec(
        num_scalar_prefetch=1,
        grid=(batch, tps, n_experts // eg),
        in_specs=[pl.BlockSpec((tt, LANES), lambda b, t, e, st: (t, 0)),
                  pl.BlockSpec((eg, cap, d), lambda b, t, e, st: (e, b, 0)),
                  pl.BlockSpec((tt, d), lambda b, t, e, st: (b * tps + t, 0)),
                  pl.BlockSpec((1, d), lambda b, t, e, st: (0, 0))],
        out_specs=pl.BlockSpec((tt, d), lambda b, t, e, st: (b * tps + t, 0)),
        scratch_shapes=[pltpu.VMEM((tt, d), F32)],
    )
    return pl.pallas_call(
        functools.partial(_combine_kernel, n_experts=n_experts, win=win, final_norm=final_norm),
        grid_spec=grid_spec,
        out_shape=jax.ShapeDtypeStruct((m, d), F32),
        compiler_params=_cparams(3, 56),
        name="combine",
    )(starts, slot_cols, y, x1, g.reshape(1, d))


def _expert_up_kernel(x_ref, wg_ref, wu_ref, o_ref, *, chunk):
    wg = wg_ref[0].astype(BF16)
    wu = wu_ref[0].astype(BF16)
    for r0 in range(0, x_ref.shape[1], chunk):
        x = x_ref[0, pl.ds(r0, chunk), :]
        gate = jnp.dot(x, wg, preferred_element_type=F32)
        up = jnp.dot(x, wu, preferred_element_type=F32)
        o_ref[0, pl.ds(r0, chunk), :] = (jax.nn.silu(gate) * up).astype(o_ref.dtype)


def _expert_up(xin, w_gate, w_up, *, tf):
    e, m, d = xin.shape
    f = w_gate.shape[2]
    return pl.pallas_call(
        functools.partial(_expert_up_kernel, chunk=512),
        grid=(e, f // tf),
        in_specs=[pl.BlockSpec((1, m, d), lambda i, j: (i, 0, 0)),
                  pl.BlockSpec((1, d, tf), lambda i, j: (i, 0, j)),
                  pl.BlockSpec((1, d, tf), lambda i, j: (i, 0, j))],
        out_specs=pl.BlockSpec((1, m, tf), lambda i, j: (i, 0, j)),
        out_shape=jax.ShapeDtypeStruct((e, m, f), BF16),
        compiler_params=_cparams(2, 56),
        name="expert_up",
    )(xin, w_gate, w_up)


def _expert_down_kernel(h_ref, wd_ref, a_ref, o_ref, *, chunk):
    wd = wd_ref[0].astype(BF16)
    for r0 in range(0, h_ref.shape[1], chunk):
        y = jnp.dot(h_ref[0, pl.ds(r0, chunk), :], wd, preferred_element_type=F32)
        o_ref[0, pl.ds(r0, chunk), :] = (y * a_ref[0, pl.ds(r0, chunk), :]).astype(o_ref.dtype)


def _expert_down(hidden, w_down, aff, *, tn):
    e, m, f = hidden.shape
    d = w_down.shape[2]
    return pl.pallas_call(
        functools.partial(_expert_down_kernel, chunk=512),
        grid=(e, d // tn),
        in_specs=[pl.BlockSpec((1, m, f), lambda i, j: (i, 0, 0)),
                  pl.BlockSpec((1, f, tn), lambda i, j: (i, 0, j)),
                  pl.BlockSpec((1, m, 1), lambda i, j: (i, 0, 0))],
        out_specs=pl.BlockSpec((1, m, tn), lambda i, j: (i, 0, j)),
        out_shape=jax.ShapeDtypeStruct((e, m, d), BF16),
        compiler_params=_cparams(2, 56),
        name="expert_down",
    )(hidden, w_down, aff)


def kernel(x, mem, norm_mix_g, norm_mem_g, w_in, pool_w, pool_scale, fourier_w, w_kv_mem, proj_pool, proj_fourier,
           proj_mem, w_out, norm_ffn_g, w_router, w_expert_gate, w_expert_up, w_expert_down, norm_final_g):
    b, s, d = x.shape
    depth = w_in.shape[0]
    n_tok = b * s
    pool_width = pool_w.shape[1] * pool_w.shape[2]
    four_width = fourier_w.shape[1] * fourier_w.shape[2]
    mem_width = proj_mem.shape[1]
    mix_width = pool_width + four_width + mem_width
    n_heads = 4
    e = w_router.shape[2]
    cap = CAPACITY_FACTOR * s // e

    xf = x.reshape(n_tok, d)
    for l in range(depth):
        h, zp, zq = _in_proj(xf, norm_mix_g[l], w_in[l], pool_width=pool_width, mix_width=mix_width, tm=512)
        zq3 = zq.reshape(b, s, mix_width - pool_width)
        ya = _pool_mixer(zp.reshape(b, s, pool_width), pool_w[l], pool_scale[l])
        yb = _fourier_mixer(zq3, fourier_w[l], col_block=0, tk=512)
        memn = _rmsnorm(mem.reshape(-1, d), norm_mem_g[l], tm=512, out_dtype=BF16)
        kv = _matmul(memn, w_kv_mem[l], n_cols=2 * mem_width, tm=1024, tn=512, out_dtype=BF16, name="kv_proj")
        yc = _mem_attention(zq3, kv.reshape(b, -1, 2 * mem_width), col_block=four_width // mem_width,
                            n_heads=n_heads, tm=512)
        merged = _gated_merge(ya.reshape(n_tok, -1), yb.reshape(n_tok, -1), yc.reshape(n_tok, -1), h,
                              proj_pool[l], proj_fourier[l], proj_mem[l], w_in[l],
                              gate_col0=mix_width, tm=1024, tn=256)

        x1, h2, aff_t = _out_proj(merged, w_out[l], xf, norm_ffn_g[l], w_router[l], batch=b, tm=512)
        slot_rows, slot_cols, starts = _select(aff_t.reshape(b * e, s), cap=cap, tile=COMBINE_TILE)
        xin, aff_slot = _dispatch(slot_rows, aff_t, h2.reshape(b, s, d), n_experts=e, cap=cap)
        hidden = _expert_up(xin, w_expert_gate[l], w_expert_up[l], tf=512)
        y = _expert_down(hidden, w_expert_down[l], aff_slot, tn=512)
        last = l + 1 == depth
        xf = _combine(starts[:, :s // COMBINE_TILE + 1], slot_cols, y, x1, norm_final_g, batch=b, n_experts=e, cap=cap,
                      tt=COMBINE_TILE, eg=8, win=COMBINE_WINDOW, final_norm=last)
    return xf.reshape(b, s, d)
```

```python
import functools

import numpy as np
import jax
import jax.numpy as jnp
from jax import lax
from jax.experimental import pallas as pl
from jax.experimental.pallas import tpu as pltpu

F32 = jnp.float32
BF16 = jnp.bfloat16

EPS = 1e-6
POOL_WINDOWS = (2, 4, 8, 16)
N_EXPERTS = 16
CAPACITY_FACTOR = 2
LANES = 128
BF16_ROWS = 16
POOL_HALO = 16
ROW_CHUNK = 256
COMBINE_TILE = 256
COMBINE_WINDOW = 64
MIB = 1024 * 1024


def _cparams(n_axes, vmem_mib):
    return pltpu.CompilerParams(
        dimension_semantics=("arbitrary",) * n_axes,
        vmem_limit_bytes=vmem_mib * MIB,
    )


def _rmsnorm_kernel(x_ref, g_ref, o_ref):
    x = x_ref[...]
    inv = lax.rsqrt(jnp.mean(x * x, axis=-1, keepdims=True) + EPS)
    o_ref[...] = ((x * inv) * g_ref[...]).astype(o_ref.dtype)


def _rmsnorm(x, g, *, tm, out_dtype):
    m, d = x.shape
    return pl.pallas_call(
        _rmsnorm_kernel,
        grid=(m // tm,),
        in_specs=[pl.BlockSpec((tm, d), lambda i: (i, 0)), pl.BlockSpec((1, d), lambda i: (0, 0))],
        out_specs=pl.BlockSpec((tm, d), lambda i: (i, 0)),
        out_shape=jax.ShapeDtypeStruct((m, d), out_dtype),
        compiler_params=_cparams(1, 40),
        name="rmsnorm",
    )(x, g.reshape(1, d))


def _mm_kernel(a_ref, w_ref, o_ref, wc_ref):
    @pl.when(pl.program_id(1) == 0)
    def _():
        wc_ref[...] = w_ref[...].astype(BF16)

    o_ref[...] = jnp.dot(a_ref[...], wc_ref[...], preferred_element_type=F32).astype(o_ref.dtype)


def _matmul(a, w, *, n_cols, tm, tn, out_dtype, name):
    m, k = a.shape
    return pl.pallas_call(
        _mm_kernel,
        grid=(n_cols // tn, m // tm),
        in_specs=[pl.BlockSpec((tm, k), lambda n, i: (i, 0)), pl.BlockSpec((k, tn), lambda n, i: (0, n))],
        out_specs=pl.BlockSpec((tm, tn), lambda n, i: (i, n)),
        out_shape=jax.ShapeDtypeStruct((m, n_cols), out_dtype),
        scratch_shapes=[pltpu.VMEM((k, tn), BF16)],
        compiler_params=_cparams(2, 48),
        name=name,
    )(a, w)


def _in_proj_kernel(x_ref, g_ref, w_ref, h_ref, zp_ref, zq_ref, wc_ref, *, chunk):
    @pl.when(pl.program_id(0) == 0)
    def _():
        wc_ref[...] = w_ref[...].astype(BF16)

    wp = zp_ref.shape[1]
    for r0 in range(0, x_ref.shape[0], ROW_CHUNK):
        rs = pl.ds(r0, ROW_CHUNK)
        x = x_ref[rs, :]
        inv = lax.rsqrt(jnp.mean(x * x, axis=-1, keepdims=True) + EPS)
        hb = ((x * inv) * g_ref[...]).astype(BF16)
        h_ref[rs, :] = hb
        for c0 in range(0, wc_ref.shape[1], chunk):
            z = jnp.dot(hb, wc_ref[:, c0:c0 + chunk], preferred_element_type=F32)
            if c0 < wp:
                zp_ref[rs, c0:c0 + chunk] = z
            else:
                zq_ref[rs, c0 - wp:c0 - wp + chunk] = z.astype(BF16)


def _in_proj(x, g, w_in, *, pool_width, mix_width, tm):
    m, d = x.shape
    row = lambda width: pl.BlockSpec((tm, width), lambda i: (i, 0))
    return pl.pallas_call(
        functools.partial(_in_proj_kernel, chunk=512),
        grid=(m // tm,),
        in_specs=[row(d), pl.BlockSpec((1, d), lambda i: (0, 0)),
                  pl.BlockSpec((d, mix_width), lambda i: (0, 0), pipeline_mode=pl.Buffered(1))],
        out_specs=[row(d), row(pool_width), row(mix_width - pool_width)],
        out_shape=[jax.ShapeDtypeStruct((m, d), BF16), jax.ShapeDtypeStruct((m, pool_width), F32),
                   jax.ShapeDtypeStruct((m, mix_width - pool_width), BF16)],
        scratch_shapes=[pltpu.VMEM((d, mix_width), BF16)],
        compiler_params=_cparams(1, 56),
        name="in_proj",
    )(x, g.reshape(1, d), w_in)


def _out_proj_kernel(m_ref, w_ref, r_ref, g_ref, wr_ref, x1_ref, h2_ref, afft_ref, wc_ref, *, chunk, n_experts):
    @pl.when(pl.program_id(0) == 0)
    def _():
        wc_ref[...] = w_ref[...].astype(BF16)

    wr = wr_ref[...].astype(BF16)
    for r0 in range(0, m_ref.shape[0], ROW_CHUNK):
        rs = pl.ds(r0, ROW_CHUNK)
        a = m_ref[rs, :]
        for c0 in range(0, wc_ref.shape[1], chunk):
            cols = slice(c0, c0 + chunk)
            x1_ref[rs, cols] = r_ref[rs, cols] + jnp.dot(a, wc_ref[:, cols], preferred_element_type=F32)

        x = x1_ref[rs, :]
        inv = lax.rsqrt(jnp.mean(x * x, axis=-1, keepdims=True) + EPS)
        hb = ((x * inv) * g_ref[...]).astype(BF16)
        h2_ref[rs, :] = hb
        logits = jnp.dot(hb, wr, preferred_element_type=F32)
        lane = lax.broadcasted_iota(jnp.int32, logits.shape, 1)
        logits = jnp.where(lane < n_experts, logits, -1e30)
        mx = jnp.max(logits, axis=-1, keepdims=True)
        p = jnp.exp(logits - mx)
        aff = p / jnp.sum(p, axis=-1, keepdims=True)
        afft_ref[0, :, rs] = aff.T[:n_experts, :]


def _out_proj(merged, w_out, resid, g, w_router, *, batch, tm):
    m, d = resid.shape
    e = w_router.shape[1]
    s = m // batch
    tps = s // tm
    wr = jnp.pad(w_router, ((0, 0), (0, LANES - e)))
    row = pl.BlockSpec((tm, d), lambda i: (i, 0))
    return pl.pallas_call(
        functools.partial(_out_proj_kernel, chunk=512, n_experts=e),
        grid=(m // tm,),
        in_specs=[row, pl.BlockSpec((d, d), lambda i: (0, 0), pipeline_mode=pl.Buffered(1)), row,
                  pl.BlockSpec((1, d), lambda i: (0, 0)), pl.BlockSpec((d, LANES), lambda i: (0, 0))],
        out_specs=[row, row, pl.BlockSpec((1, e, tm), lambda i: (i // tps, 0, i % tps))],
        out_shape=[jax.ShapeDtypeStruct((m, d), F32), jax.ShapeDtypeStruct((m, d), BF16),
                   jax.ShapeDtypeStruct((batch, e, s), F32)],
        scratch_shapes=[pltpu.VMEM((d, d), BF16)],
        compiler_params=_cparams(1, 56),
        name="out_proj",
    )(merged, w_out, resid, g.reshape(1, d), wr)


def _pool_kernel(u_ref, pw_ref, ps_ref, o_ref, pad_ref, *, chunk):
    s = u_ref.shape[1]
    c = pw_ref.shape[1]
    h = POOL_HALO
    zeros = jnp.zeros((h, c), F32)
    pad_ref[pl.ds(0, h), :] = zeros
    pad_ref[pl.ds(h + s, h), :] = zeros
    for g, w in enumerate(POOL_WINDOWS):
        cols = slice(g * c, (g + 1) * c)
        pad_ref[pl.ds(h, s), :] = u_ref[0, :, cols]
        wg = pw_ref[g].astype(BF16)
        scale = ps_ref[:, cols]
        for r0 in range(0, s, chunk):
            pos = r0 + lax.broadcasted_iota(jnp.int32, (chunk, 1), 0)
            lo = jnp.maximum(pos - w // 2, 0)
            hi = jnp.minimum(pos + (w - w // 2), s)
            cnt = (hi - lo).astype(F32)
            tot = pad_ref[pl.ds(h + r0 - w // 2, chunk), :]
            for k in range(1, w):
                tot = tot + pad_ref[pl.ds(h + r0 - w // 2 + k, chunk), :]
            pooled = tot / cnt - pad_ref[pl.ds(h + r0, chunk), :]
            y = jnp.dot(pooled.astype(BF16), wg, preferred_element_type=F32)
            o_ref[0, pl.ds(r0, chunk), cols] = (y * scale).astype(o_ref.dtype)


def _pool_mixer(z3, pool_w, pool_scale):
    b, s, _ = z3.shape
    g, c, _ = pool_w.shape
    width = g * c
    return pl.pallas_call(
        functools.partial(_pool_kernel, chunk=256),
        grid=(b,),
        in_specs=[pl.BlockSpec((1, s, width), lambda i: (i, 0, 0)),
                  pl.BlockSpec((g, c, c), lambda i: (0, 0, 0)),
                  pl.BlockSpec((1, width), lambda i: (0, 0))],
        out_specs=pl.BlockSpec((1, s, width), lambda i: (i, 0, 0)),
        out_shape=jax.ShapeDtypeStruct((b, s, width), BF16),
        scratch_shapes=[pltpu.VMEM((s + 2 * POOL_HALO, c), F32)],
        compiler_params=_cparams(1, 48),
        name="pool_mixer",
    )(z3, pool_w, pool_scale.reshape(1, width))


def _dft_tables(s, c):
    def tab(n):
        r = np.outer(np.arange(n), np.arange(n)) % n
        ang = r * (2.0 * np.pi / n)
        return np.cos(ang), np.sin(ang)
    cs, ss = tab(s)
    cc, sc = tab(c)
    return (jnp.asarray(np.concatenate([cs, -ss], axis=1), dtype=F32).astype(BF16),
            jnp.asarray(np.concatenate([cc, sc], axis=1), dtype=F32).astype(BF16))


def _fourier_kernel(u_ref, dft_ref, ccsc_ref, fw_ref, o_ref, t_ref, *, norm):
    s = u_ref.shape[1]
    ng, c, _ = fw_ref.shape

    @pl.when(pl.program_id(1) == 0)
    def _():
        for g in range(ng):
            ug = u_ref[0, :, g * c:(g + 1) * c]
            ab = jnp.dot(ug, ccsc_ref[...], preferred_element_type=F32)
            t_ref[pl.ds(0, s), g * c:(g + 1) * c] = ab[:, :c].astype(BF16)
            t_ref[pl.ds(s, s), g * c:(g + 1) * c] = ab[:, c:].astype(BF16)

    f = jnp.dot(dft_ref[...], t_ref[...], preferred_element_type=F32) * norm
    for g in range(ng):
        y = jnp.dot(f[:, g * c:(g + 1) * c].astype(BF16), fw_ref[g].astype(BF16), preferred_element_type=F32)
        o_ref[0, :, g * c:(g + 1) * c] = y.astype(o_ref.dtype)


def _fourier_mixer(z3, fourier_w, *, col_block, tk):
    b, s, _ = z3.shape
    ng, c, _ = fourier_w.shape
    width = ng * c
    dft, ccsc = _dft_tables(s, c)
    return pl.pallas_call(
        functools.partial(_fourier_kernel, norm=float((s * c) ** -0.5)),
        grid=(b, s // tk),
        in_specs=[pl.BlockSpec((1, s, width), lambda i, k: (i, 0, col_block)),
                  pl.BlockSpec((tk, 2 * s), lambda i, k: (k, 0)),
                  pl.BlockSpec((c, 2 * c), lambda i, k: (0, 0)),
                  pl.BlockSpec((ng, c, c), lambda i, k: (0, 0, 0))],
        out_specs=pl.BlockSpec((1, tk, width), lambda i, k: (i, k, 0)),
        out_shape=jax.ShapeDtypeStruct((b, s, width), BF16),
        scratch_shapes=[pltpu.VMEM((2 * s, width), BF16)],
        compiler_params=_cparams(2, 48),
        name="fourier_mixer",
    )(z3, dft, ccsc, fourier_w)


def _attn_kernel(q_ref, kv_ref, o_ref, *, n_heads, scale):
    dh = q_ref.shape[2] // n_heads
    width = n_heads * dh
    for h in range(n_heads):
        q = q_ref[0, :, h * dh:(h + 1) * dh]
        k = kv_ref[0, :, h * dh:(h + 1) * dh]
        v = kv_ref[0, :, width + h * dh:width + (h + 1) * dh]
        sc = lax.dot_general(q, k, (((1,), (1,)), ((), ())), preferred_element_type=F32) * scale
        mx = jnp.max(sc, axis=-1, keepdims=True)
        p = jnp.exp(sc - mx)
        p = p / jnp.sum(p, axis=-1, keepdims=True)
        o = jnp.dot(p.astype(BF16), v, preferred_element_type=F32)
        o_ref[0, :, h * dh:(h + 1) * dh] = o.astype(o_ref.dtype)


def _mem_attention(z3, kv3, *, col_block, n_heads, tm):
    b, s, _ = z3.shape
    _, m, kvw = kv3.shape
    width = kvw // 2
    return pl.pallas_call(
        functools.partial(_attn_kernel, n_heads=n_heads, scale=float((width // n_heads) ** -0.5)),
        grid=(b, s // tm),
        in_specs=[pl.BlockSpec((1, tm, width), lambda i, j: (i, j, col_block)),
                  pl.BlockSpec((1, m, kvw), lambda i, j: (i, 0, 0))],
        out_specs=pl.BlockSpec((1, tm, width), lambda i, j: (i, j, 0)),
        out_shape=jax.ShapeDtypeStruct((b, s, width), BF16),
        compiler_params=_cparams(2, 40),
        name="mem_attention",
    )(z3, kv3)


def _merge_kernel(ya_ref, yb_ref, yc_ref, h_ref, pp_ref, pf_ref, pm_ref, g0_ref, g1_ref, g2_ref, o_ref,
                  cpp, cpf, cpm, cg0, cg1, cg2):
    @pl.when(pl.program_id(1) == 0)
    def _():
        for src, dst in ((pp_ref, cpp), (pf_ref, cpf), (pm_ref, cpm), (g0_ref, cg0), (g1_ref, cg1), (g2_ref, cg2)):
            dst[...] = src[...].astype(BF16)

    h = h_ref[...]

    def branch(y_ref, proj, gate_w):
        gate = jax.nn.sigmoid(jnp.dot(h, gate_w[...], preferred_element_type=F32))
        return gate * jnp.dot(y_ref[...], proj[...], preferred_element_type=F32)

    acc = branch(ya_ref, cpp, cg0)
    acc = acc + branch(yb_ref, cpf, cg1)
    acc = acc + branch(yc_ref, cpm, cg2)
    o_ref[...] = acc.astype(o_ref.dtype)


def _gated_merge(ya, yb, yc, h, proj_pool, proj_fourier, proj_mem, w_in, *, gate_col0, tm, tn):
    m, d = h.shape
    gb = gate_col0 // tn
    nb = d // tn
    act = lambda width: pl.BlockSpec((tm, width), lambda n, i: (i, 0))
    wsp = lambda rows, off: pl.BlockSpec((rows, tn), lambda n, i: (0, n + off))
    return pl.pallas_call(
        _merge_kernel,
        grid=(d // tn, m // tm),
        in_specs=[act(ya.shape[1]), act(yb.shape[1]), act(yc.shape[1]), act(d),
                  wsp(proj_pool.shape[0], 0), wsp(proj_fourier.shape[0], 0), wsp(proj_mem.shape[0], 0),
                  wsp(d, gb), wsp(d, gb + nb), wsp(d, gb + 2 * nb)],
        out_specs=pl.BlockSpec((tm, tn), lambda n, i: (i, n)),
        out_shape=jax.ShapeDtypeStruct((m, d), BF16),
        scratch_shapes=[pltpu.VMEM((proj_pool.shape[0], tn), BF16), pltpu.VMEM((proj_fourier.shape[0], tn), BF16),
                        pltpu.VMEM((proj_mem.shape[0], tn), BF16), pltpu.VMEM((d, tn), BF16),
                        pltpu.VMEM((d, tn), BF16), pltpu.VMEM((d, tn), BF16)],
        compiler_params=_cparams(2, 56),
        name="gated_merge",
    )(ya, yb, yc, h, proj_pool, proj_fourier, proj_mem, w_in, w_in, w_in)


def _select_kernel(aff_ref, tri_ref, slot_ref, slot_t_ref, starts_ref, *, cap, tile):
    a = aff_ref[...]
    rows = a.shape[0]
    capf = float(cap)

    def count(pred):
        return jnp.sum(pred.astype(F32), axis=-1, keepdims=True)

    def body(i, t_bits):
        cand = t_bits | jnp.left_shift(jnp.int32(1), 30 - i)
        return jnp.where(count(a >= pltpu.bitcast(cand, F32)) >= capf, cand, t_bits)

    t = pltpu.bitcast(lax.fori_loop(0, 31, body, jnp.zeros((rows, 1), jnp.int32)), F32)
    gt = a > t
    eq = a == t
    need = capf - count(gt)
    tri = tri_ref[...]
    eq_rank = jnp.dot(eq.astype(BF16), tri, preferred_element_type=F32)
    sel = gt | (eq & (eq_rank < need))
    pos = jnp.dot(sel.astype(BF16), tri, preferred_element_type=F32)
    slot = jnp.where(sel, pos, -1.0)
    slot_ref[...] = slot
    slot_t_ref[...] = slot.T
    s = a.shape[1]
    lane = lax.broadcasted_iota(jnp.int32, (rows, LANES), 1)
    starts = jnp.where(lane == s // tile, capf, 0.0)
    for t in range(s // tile):
        starts = jnp.where(lane == t, pos[:, t * tile:t * tile + 1], starts)
    starts_ref[...] = starts.astype(jnp.int32)


def _select(aff_rows, *, cap, tile):
    rows, s = aff_rows.shape
    idx = jnp.arange(s, dtype=jnp.int32)
    tri = (idx[:, None] < idx[None, :]).astype(BF16)
    full = lambda shape: pl.BlockSpec(shape, lambda i: (0, 0))
    return pl.pallas_call(
        functools.partial(_select_kernel, cap=cap, tile=tile),
        grid=(1,),
        in_specs=[full((rows, s)), full((s, s))],
        out_specs=[full((rows, s)), full((s, rows)), full((rows, LANES))],
        out_shape=[jax.ShapeDtypeStruct((rows, s), F32), jax.ShapeDtypeStruct((s, rows), F32),
                   jax.ShapeDtypeStruct((rows, LANES), jnp.int32)],
        compiler_params=_cparams(1, 48),
        name="expert_select",
    )(aff_rows, tri)


def _dispatch_kernel(slot_ref, aff_ref, h_ref, o_ref, oa_ref):
    cap = o_ref.shape[1]
    s = h_ref.shape[1]
    slot = slot_ref[0]
    j = lax.broadcasted_iota(jnp.int32, (cap, s), 0).astype(F32)
    hit = slot == j
    o_ref[0] = jnp.dot(hit.astype(BF16), h_ref[0], preferred_element_type=F32).astype(o_ref.dtype)
    oa_ref[0] = jnp.sum(jnp.where(hit, aff_ref[0], 0.0), axis=-1, keepdims=True)


def _dispatch(slot_rows, aff_rows, h3, *, n_experts, cap):
    b, s, d = h3.shape
    row = pl.BlockSpec((1, 1, s), lambda i, j: (i * n_experts + j, 0, 0))
    return pl.pallas_call(
        _dispatch_kernel,
        grid=(b, n_experts),
        in_specs=[row, row, pl.BlockSpec((1, s, d), lambda i, j: (i, 0, 0))],
        out_specs=[pl.BlockSpec((1, cap, d), lambda i, j: (j, i, 0)),
                   pl.BlockSpec((1, cap, 1), lambda i, j: (j, i, 0))],
        out_shape=[jax.ShapeDtypeStruct((n_experts, b * cap, d), BF16),
                   jax.ShapeDtypeStruct((n_experts, b * cap, 1), F32)],
        compiler_params=_cparams(2, 48),
        name="dispatch",
    )(slot_rows.reshape(b * n_experts, 1, s), aff_rows.reshape(b * n_experts, 1, s), h3)


def _combine_kernel(starts_ref, slot_ref, y_ref, x_ref, g_ref, o_ref, *, win, final_norm):
    bi = pl.program_id(0)
    ti = pl.program_id(1)
    tt, d = x_ref.shape
    ne, cap, _ = y_ref.shape

    wins = []
    fits = None
    for k in range(ne):
        lo = starts_ref[bi * ne + k, ti]
        hi = starts_ref[bi * ne + k, ti + 1]
        w0 = jnp.minimum(jnp.bitwise_and(lo, -BF16_ROWS), cap - win)
        ok = hi - w0 <= win
        fits = ok if fits is None else jnp.logical_and(fits, ok)
        wins.append(w0)

    slots = slot_ref[...].astype(BF16)

    def onehot(width, first_slot):
        shift = width.bit_length() - 1
        n = ne * width
        lane_of = bi * ne + lax.shift_right_logical(lax.broadcasted_iota(jnp.int32, (LANES, n), 1), shift)
        spread = (lax.broadcasted_iota(jnp.int32, (LANES, n), 0) == lane_of).astype(BF16)
        slot_b = jnp.dot(slots, spread, preferred_element_type=F32)
        c = lax.broadcasted_iota(jnp.int32, (1, n), 1)
        target = jnp.bitwise_and(c, width - 1)
        if first_slot is not None:
            kk = lax.shift_right_logical(c, shift)
            for k in range(ne):
                target = target + jnp.where(kk == k, first_slot[k], 0)
        return (slot_b == target.astype(F32)).astype(BF16)

    def finish(contrib):
        x = x_ref[...] + contrib
        if final_norm:
            inv = lax.rsqrt(jnp.mean(x * x, axis=-1, keepdims=True) + EPS)
            x = (x * inv) * g_ref[...]
        o_ref[...] = x

    @pl.when(fits)
    def _():
        ywin = jnp.concatenate([y_ref[k, pl.ds(pl.multiple_of(wins[k], BF16_ROWS), win), :] for k in range(ne)],
                               axis=0)
        finish(jnp.dot(onehot(win, wins), ywin, preferred_element_type=F32))

    @pl.when(jnp.logical_not(fits))
    def _():
        finish(jnp.dot(onehot(cap, None), y_ref[...].reshape(ne * cap, d), preferred_element_type=F32))


def _combine(starts, slot_cols, y, x1, g, *, batch, cap, tt, win, final_norm):
    m, d = x1.shape
    ne = y.shape[0]
    s = m // batch
    tps = s // tt
    assert slot_cols.shape == (s, LANES) and batch * ne == LANES
    assert win % BF16_ROWS == 0 and win <= cap and win & (win - 1) == 0 and cap & (cap - 1) == 0
    grid_spec = pltpu.PrefetchScalarGridSpec(
        num_scalar_prefetch=1,
        grid=(batch, tps),
        in_specs=[pl.BlockSpec((tt, LANES), lambda b, t, st: (t, 0)),
                  pl.BlockSpec((ne, cap, d), lambda b, t, st: (0, b, 0)),
                  pl.BlockSpec((tt, d), lambda b, t, st: (b * tps + t, 0)),
                  pl.BlockSpec((1, d), lambda b, t, st: (0, 0))],
        out_specs=pl.BlockSpec((tt, d), lambda b, t, st: (b * tps + t, 0)),
    )
    return pl.pallas_call(
        functools.partial(_combine_kernel, win=win, final_norm=final_norm),
        grid_spec=grid_spec,
        out_shape=jax.ShapeDtypeStruct((m, d), F32),
        compiler_params=_cparams(2, 56),
        name="combine",
    )(starts, slot_cols, y, x1, g.reshape(1, d))


def _expert_up_kernel(x_ref, wg_ref, wu_ref, o_ref, *, chunk):
    wg = wg_ref[0].astype(BF16)
    wu = wu_ref[0].astype(BF16)
    for r0 in range(0, x_ref.shape[1], chunk):
        x = x_ref[0, pl.ds(r0, chunk), :]
        gate = jnp.dot(x, wg, preferred_element_type=F32)
        up = jnp.dot(x, wu, preferred_element_type=F32)
        o_ref[0, pl.ds(r0, chunk), :] = (jax.nn.silu(gate) * up).astype(o_ref.dtype)


def _expert_up(xin, w_gate, w_up, *, tf):
    e, m, d = xin.shape
    f = w_gate.shape[2]
    return pl.pallas_call(
        functools.partial(_expert_up_kernel, chunk=512),
        grid=(e, f // tf),
        in_specs=[pl.BlockSpec((1, m, d), lambda i, j: (i, 0, 0)),
                  pl.BlockSpec((1, d, tf), lambda i, j: (i, 0, j)),
                  pl.BlockSpec((1, d, tf), lambda i, j: (i, 0, j))],
        out_specs=pl.BlockSpec((1, m, tf), lambda i, j: (i, 0, j)),
        out_shape=jax.ShapeDtypeStruct((e, m, f), BF16),
        compiler_params=_cparams(2, 56),
        name="expert_up",
    )(xin, w_gate, w_up)


def _expert_down_kernel(h_ref, wd_ref, a_ref, o_ref, *, chunk):
    wd = wd_ref[0].astype(BF16)
    for r0 in range(0, h_ref.shape[1], chunk):
        y = jnp.dot(h_ref[0, pl.ds(r0, chunk), :], wd, preferred_element_type=F32)
        o_ref[0, pl.ds(r0, chunk), :] = (y * a_ref[0, pl.ds(r0, chunk), :]).astype(o_ref.dtype)


def _expert_down(hidden, w_down, aff, *, tn):
    e, m, f = hidden.shape
    d = w_down.shape[2]
    return pl.pallas_call(
        functools.partial(_expert_down_kernel, chunk=512),
        grid=(e, d // tn),
        in_specs=[pl.BlockSpec((1, m, f), lambda i, j: (i, 0, 0)),
                  pl.BlockSpec((1, f, tn), lambda i, j: (i, 0, j)),
                  pl.BlockSpec((1, m, 1), lambda i, j: (i, 0, 0))],
        out_specs=pl.BlockSpec((1, m, tn), lambda i, j: (i, 0, j)),
        out_shape=jax.ShapeDtypeStruct((e, m, d), BF16),
        compiler_params=_cparams(2, 56),
        name="expert_down",
    )(hidden, w_down, aff)


def kernel(x, mem, norm_mix_g, norm_mem_g, w_in, pool_w, pool_scale, fourier_w, w_kv_mem, proj_pool, proj_fourier,
           proj_mem, w_out, norm_ffn_g, w_router, w_expert_gate, w_expert_up, w_expert_down, norm_final_g):
    b, s, d = x.shape
    depth = w_in.shape[0]
    n_tok = b * s
    pool_width = pool_w.shape[1] * pool_w.shape[2]
    four_width = fourier_w.shape[1] * fourier_w.shape[2]
    mem_width = proj_mem.shape[1]
    mix_width = pool_width + four_width + mem_width
    n_heads = 4
    e = w_router.shape[2]
    cap = CAPACITY_FACTOR * s // e

    xf = x.reshape(n_tok, d)
    for l in range(depth):
        h, zp, zq = _in_proj(xf, norm_mix_g[l], w_in[l], pool_width=pool_width, mix_width=mix_width, tm=512)
        zq3 = zq.reshape(b, s, mix_width - pool_width)
        ya = _pool_mixer(zp.reshape(b, s, pool_width), pool_w[l], pool_scale[l])
        yb = _fourier_mixer(zq3, fourier_w[l], col_block=0, tk=512)
        memn = _rmsnorm(mem.reshape(-1, d), norm_mem_g[l], tm=512, out_dtype=BF16)
        kv = _matmul(memn, w_kv_mem[l], n_cols=2 * mem_width, tm=1024, tn=512, out_dtype=BF16, name="kv_proj")
        yc = _mem_attention(zq3, kv.reshape(b, -1, 2 * mem_width), col_block=four_width // mem_width,
                            n_heads=n_heads, tm=512)
        merged = _gated_merge(ya.reshape(n_tok, -1), yb.reshape(n_tok, -1), yc.reshape(n_tok, -1), h,
                              proj_pool[l], proj_fourier[l], proj_mem[l], w_in[l],
                              gate_col0=mix_width, tm=1024, tn=256)

        x1, h2, aff_t = _out_proj(merged, w_out[l], xf, norm_ffn_g[l], w_router[l], batch=b, tm=512)
        slot_rows, slot_cols, starts = _select(aff_t.reshape(b * e, s), cap=cap, tile=COMBINE_TILE)
        xin, aff_slot = _dispatch(slot_rows, aff_t, h2.reshape(b, s, d), n_experts=e, cap=cap)
        hidden = _expert_up(xin, w_expert_gate[l], w_expert_up[l], tf=512)
        y = _expert_down(hidden, w_expert_down[l], aff_slot, tn=512)
        last = l + 1 == depth
        xf = _combine(starts[:, :s // COMBINE_TILE + 1], slot_cols, y, x1, norm_final_g, batch=b, cap=cap,
                      tt=COMBINE_TILE, win=COMBINE_WINDOW, final_norm=last)
    return xf.reshape(b, s, d)
```

```python
import functools

import numpy as np
import jax
import jax.numpy as jnp
from jax import lax
from jax.experimental import pallas as pl
from jax.experimental.pallas import tpu as pltpu

F32 = jnp.float32
BF16 = jnp.bfloat16

EPS = 1e-6
POOL_WINDOWS = (2, 4, 8, 16)
N_EXPERTS = 16
CAPACITY_FACTOR = 2
LANES = 128
BF16_ROWS = 16
POOL_HALO = 16
ROW_CHUNK = 256
DISPATCH_BLOCK = 64
DISPATCH_WINDOW = 768
COMBINE_TILE = 256
COMBINE_WINDOW = 64
MIB = 1024 * 1024


def _cparams(n_axes, vmem_mib):
    return pltpu.CompilerParams(
        dimension_semantics=("arbitrary",) * n_axes,
        vmem_limit_bytes=vmem_mib * MIB,
    )


def _rmsnorm_kernel(x_ref, g_ref, o_ref):
    x = x_ref[...]
    inv = lax.rsqrt(jnp.mean(x * x, axis=-1, keepdims=True) + EPS)
    o_ref[...] = ((x * inv) * g_ref[...]).astype(o_ref.dtype)


def _rmsnorm(x, g, *, tm, out_dtype):
    m, d = x.shape
    return pl.pallas_call(
        _rmsnorm_kernel,
        grid=(m // tm,),
        in_specs=[pl.BlockSpec((tm, d), lambda i: (i, 0)), pl.BlockSpec((1, d), lambda i: (0, 0))],
        out_specs=pl.BlockSpec((tm, d), lambda i: (i, 0)),
        out_shape=jax.ShapeDtypeStruct((m, d), out_dtype),
        compiler_params=_cparams(1, 40),
        name="rmsnorm",
    )(x, g.reshape(1, d))


def _mm_kernel(a_ref, w_ref, o_ref, wc_ref):
    @pl.when(pl.program_id(1) == 0)
    def _():
        wc_ref[...] = w_ref[...].astype(BF16)

    o_ref[...] = jnp.dot(a_ref[...], wc_ref[...], preferred_element_type=F32).astype(o_ref.dtype)


def _matmul(a, w, *, n_cols, tm, tn, out_dtype, name):
    m, k = a.shape
    return pl.pallas_call(
        _mm_kernel,
        grid=(n_cols // tn, m // tm),
        in_specs=[pl.BlockSpec((tm, k), lambda n, i: (i, 0)), pl.BlockSpec((k, tn), lambda n, i: (0, n))],
        out_specs=pl.BlockSpec((tm, tn), lambda n, i: (i, n)),
        out_shape=jax.ShapeDtypeStruct((m, n_cols), out_dtype),
        scratch_shapes=[pltpu.VMEM((k, tn), BF16)],
        compiler_params=_cparams(2, 48),
        name=name,
    )(a, w)


def _in_proj_kernel(x_ref, g_ref, w_ref, h_ref, zp_ref, zq_ref, wc_ref, *, chunk):
    @pl.when(pl.program_id(0) == 0)
    def _():
        wc_ref[...] = w_ref[...].astype(BF16)

    wp = zp_ref.shape[1]
    for r0 in range(0, x_ref.shape[0], ROW_CHUNK):
        rs = pl.ds(r0, ROW_CHUNK)
        x = x_ref[rs, :]
        inv = lax.rsqrt(jnp.mean(x * x, axis=-1, keepdims=True) + EPS)
        hb = ((x * inv) * g_ref[...]).astype(BF16)
        h_ref[rs, :] = hb
        for c0 in range(0, wc_ref.shape[1], chunk):
            z = jnp.dot(hb, wc_ref[:, c0:c0 + chunk], preferred_element_type=F32)
            if c0 < wp:
                zp_ref[rs, c0:c0 + chunk] = z
            else:
                zq_ref[rs, c0 - wp:c0 - wp + chunk] = z.astype(BF16)


def _in_proj(x, g, w_in, *, pool_width, mix_width, tm):
    m, d = x.shape
    row = lambda width: pl.BlockSpec((tm, width), lambda i: (i, 0))
    return pl.pallas_call(
        functools.partial(_in_proj_kernel, chunk=512),
        grid=(m // tm,),
        in_specs=[row(d), pl.BlockSpec((1, d), lambda i: (0, 0)),
                  pl.BlockSpec((d, mix_width), lambda i: (0, 0), pipeline_mode=pl.Buffered(1))],
        out_specs=[row(d), row(pool_width), row(mix_width - pool_width)],
        out_shape=[jax.ShapeDtypeStruct((m, d), BF16), jax.ShapeDtypeStruct((m, pool_width), F32),
                   jax.ShapeDtypeStruct((m, mix_width - pool_width), BF16)],
        scratch_shapes=[pltpu.VMEM((d, mix_width), BF16)],
        compiler_params=_cparams(1, 56),
        name="in_proj",
    )(x, g.reshape(1, d), w_in)


def _out_proj_kernel(m_ref, w_ref, r_ref, g_ref, wr_ref, x1_ref, h2_ref, afft_ref, wc_ref, *, chunk, n_experts):
    @pl.when(pl.program_id(0) == 0)
    def _():
        wc_ref[...] = w_ref[...].astype(BF16)

    wr = wr_ref[...].astype(BF16)
    for r0 in range(0, m_ref.shape[0], ROW_CHUNK):
        rs = pl.ds(r0, ROW_CHUNK)
        a = m_ref[rs, :]
        for c0 in range(0, wc_ref.shape[1], chunk):
            cols = slice(c0, c0 + chunk)
            x1_ref[rs, cols] = r_ref[rs, cols] + jnp.dot(a, wc_ref[:, cols], preferred_element_type=F32)

        x = x1_ref[rs, :]
        inv = lax.rsqrt(jnp.mean(x * x, axis=-1, keepdims=True) + EPS)
        hb = ((x * inv) * g_ref[...]).astype(BF16)
        h2_ref[rs, :] = hb
        logits = jnp.dot(hb, wr, preferred_element_type=F32)
        lane = lax.broadcasted_iota(jnp.int32, logits.shape, 1)
        logits = jnp.where(lane < n_experts, logits, -1e30)
        mx = jnp.max(logits, axis=-1, keepdims=True)
        p = jnp.exp(logits - mx)
        aff = p / jnp.sum(p, axis=-1, keepdims=True)
        afft_ref[0, :, rs] = aff.T[:n_experts, :]


def _out_proj(merged, w_out, resid, g, w_router, *, batch, tm):
    m, d = resid.shape
    e = w_router.shape[1]
    s = m // batch
    tps = s // tm
    wr = jnp.pad(w_router, ((0, 0), (0, LANES - e)))
    row = pl.BlockSpec((tm, d), lambda i: (i, 0))
    return pl.pallas_call(
        functools.partial(_out_proj_kernel, chunk=512, n_experts=e),
        grid=(m // tm,),
        in_specs=[row, pl.BlockSpec((d, d), lambda i: (0, 0), pipeline_mode=pl.Buffered(1)), row,
                  pl.BlockSpec((1, d), lambda i: (0, 0)), pl.BlockSpec((d, LANES), lambda i: (0, 0))],
        out_specs=[row, row, pl.BlockSpec((1, e, tm), lambda i: (i // tps, 0, i % tps))],
        out_shape=[jax.ShapeDtypeStruct((m, d), F32), jax.ShapeDtypeStruct((m, d), BF16),
                   jax.ShapeDtypeStruct((batch, e, s), F32)],
        scratch_shapes=[pltpu.VMEM((d, d), BF16)],
        compiler_params=_cparams(1, 56),
        name="out_proj",
    )(merged, w_out, resid, g.reshape(1, d), wr)


def _pool_kernel(u_ref, pw_ref, ps_ref, o_ref, pad_ref, *, chunk):
    s = u_ref.shape[1]
    c = pw_ref.shape[1]
    h = POOL_HALO
    zeros = jnp.zeros((h, c), F32)
    pad_ref[pl.ds(0, h), :] = zeros
    pad_ref[pl.ds(h + s, h), :] = zeros
    for g, w in enumerate(POOL_WINDOWS):
        cols = slice(g * c, (g + 1) * c)
        pad_ref[pl.ds(h, s), :] = u_ref[0, :, cols]
        wg = pw_ref[g].astype(BF16)
        scale = ps_ref[:, cols]
        for r0 in range(0, s, chunk):
            pos = r0 + lax.broadcasted_iota(jnp.int32, (chunk, 1), 0)
            lo = jnp.maximum(pos - w // 2, 0)
            hi = jnp.minimum(pos + (w - w // 2), s)
            cnt = (hi - lo).astype(F32)
            tot = pad_ref[pl.ds(h + r0 - w // 2, chunk), :]
            for k in range(1, w):
                tot = tot + pad_ref[pl.ds(h + r0 - w // 2 + k, chunk), :]
            pooled = tot / cnt - pad_ref[pl.ds(h + r0, chunk), :]
            y = jnp.dot(pooled.astype(BF16), wg, preferred_element_type=F32)
            o_ref[0, pl.ds(r0, chunk), cols] = (y * scale).astype(o_ref.dtype)


def _pool_mixer(z3, pool_w, pool_scale):
    b, s, _ = z3.shape
    g, c, _ = pool_w.shape
    width = g * c
    return pl.pallas_call(
        functools.partial(_pool_kernel, chunk=256),
        grid=(b,),
        in_specs=[pl.BlockSpec((1, s, width), lambda i: (i, 0, 0)),
                  pl.BlockSpec((g, c, c), lambda i: (0, 0, 0)),
                  pl.BlockSpec((1, width), lambda i: (0, 0))],
        out_specs=pl.BlockSpec((1, s, width), lambda i: (i, 0, 0)),
        out_shape=jax.ShapeDtypeStruct((b, s, width), BF16),
        scratch_shapes=[pltpu.VMEM((s + 2 * POOL_HALO, c), F32)],
        compiler_params=_cparams(1, 48),
        name="pool_mixer",
    )(z3, pool_w, pool_scale.reshape(1, width))


def _dft_tables(s, c):
    def tab(n):
        r = np.outer(np.arange(n), np.arange(n)) % n
        ang = r * (2.0 * np.pi / n)
        return np.cos(ang), np.sin(ang)
    cs, ss = tab(s)
    cc, sc = tab(c)
    return (jnp.asarray(np.concatenate([cs, -ss], axis=1), dtype=F32).astype(BF16),
            jnp.asarray(np.concatenate([cc, sc], axis=1), dtype=F32).astype(BF16))


def _fourier_kernel(u_ref, dft_ref, ccsc_ref, fw_ref, o_ref, t_ref, *, norm):
    s = u_ref.shape[1]
    ng, c, _ = fw_ref.shape

    @pl.when(pl.program_id(1) == 0)
    def _():
        for g in range(ng):
            ug = u_ref[0, :, g * c:(g + 1) * c]
            ab = jnp.dot(ug, ccsc_ref[...], preferred_element_type=F32)
            t_ref[pl.ds(0, s), g * c:(g + 1) * c] = ab[:, :c].astype(BF16)
            t_ref[pl.ds(s, s), g * c:(g + 1) * c] = ab[:, c:].astype(BF16)

    f = jnp.dot(dft_ref[...], t_ref[...], preferred_element_type=F32) * norm
    for g in range(ng):
        y = jnp.dot(f[:, g * c:(g + 1) * c].astype(BF16), fw_ref[g].astype(BF16), preferred_element_type=F32)
        o_ref[0, :, g * c:(g + 1) * c] = y.astype(o_ref.dtype)


def _fourier_mixer(z3, fourier_w, *, col_block, tk):
    b, s, _ = z3.shape
    ng, c, _ = fourier_w.shape
    width = ng * c
    dft, ccsc = _dft_tables(s, c)
    return pl.pallas_call(
        functools.partial(_fourier_kernel, norm=float((s * c) ** -0.5)),
        grid=(b, s // tk),
        in_specs=[pl.BlockSpec((1, s, width), lambda i, k: (i, 0, col_block)),
                  pl.BlockSpec((tk, 2 * s), lambda i, k: (k, 0)),
                  pl.BlockSpec((c, 2 * c), lambda i, k: (0, 0)),
                  pl.BlockSpec((ng, c, c), lambda i, k: (0, 0, 0))],
        out_specs=pl.BlockSpec((1, tk, width), lambda i, k: (i, k, 0)),
        out_shape=jax.ShapeDtypeStruct((b, s, width), BF16),
        scratch_shapes=[pltpu.VMEM((2 * s, width), BF16)],
        compiler_params=_cparams(2, 48),
        name="fourier_mixer",
    )(z3, dft, ccsc, fourier_w)


def _attn_kernel(q_ref, kv_ref, o_ref, *, n_heads, scale):
    dh = q_ref.shape[2] // n_heads
    width = n_heads * dh
    for h in range(n_heads):
        q = q_ref[0, :, h * dh:(h + 1) * dh]
        k = kv_ref[0, :, h * dh:(h + 1) * dh]
        v = kv_ref[0, :, width + h * dh:width + (h + 1) * dh]
        sc = lax.dot_general(q, k, (((1,), (1,)), ((), ())), preferred_element_type=F32) * scale
        mx = jnp.max(sc, axis=-1, keepdims=True)
        p = jnp.exp(sc - mx)
        p = p / jnp.sum(p, axis=-1, keepdims=True)
        o = jnp.dot(p.astype(BF16), v, preferred_element_type=F32)
        o_ref[0, :, h * dh:(h + 1) * dh] = o.astype(o_ref.dtype)


def _mem_attention(z3, kv3, *, col_block, n_heads, tm):
    b, s, _ = z3.shape
    _, m, kvw = kv3.shape
    width = kvw // 2
    return pl.pallas_call(
        functools.partial(_attn_kernel, n_heads=n_heads, scale=float((width // n_heads) ** -0.5)),
        grid=(b, s // tm),
        in_specs=[pl.BlockSpec((1, tm, width), lambda i, j: (i, j, col_block)),
                  pl.BlockSpec((1, m, kvw), lambda i, j: (i, 0, 0))],
        out_specs=pl.BlockSpec((1, tm, width), lambda i, j: (i, j, 0)),
        out_shape=jax.ShapeDtypeStruct((b, s, width), BF16),
        compiler_params=_cparams(2, 40),
        name="mem_attention",
    )(z3, kv3)


def _merge_kernel(ya_ref, yb_ref, yc_ref, h_ref, pp_ref, pf_ref, pm_ref, g0_ref, g1_ref, g2_ref, o_ref,
                  cpp, cpf, cpm, cg0, cg1, cg2):
    @pl.when(pl.program_id(1) == 0)
    def _():
        for src, dst in ((pp_ref, cpp), (pf_ref, cpf), (pm_ref, cpm), (g0_ref, cg0), (g1_ref, cg1), (g2_ref, cg2)):
            dst[...] = src[...].astype(BF16)

    h = h_ref[...]

    def branch(y_ref, proj, gate_w):
        gate = jax.nn.sigmoid(jnp.dot(h, gate_w[...], preferred_element_type=F32))
        return gate * jnp.dot(y_ref[...], proj[...], preferred_element_type=F32)

    acc = branch(ya_ref, cpp, cg0)
    acc = acc + branch(yb_ref, cpf, cg1)
    acc = acc + branch(yc_ref, cpm, cg2)
    o_ref[...] = acc.astype(o_ref.dtype)


def _gated_merge(ya, yb, yc, h, proj_pool, proj_fourier, proj_mem, w_in, *, gate_col0, tm, tn):
    m, d = h.shape
    gb = gate_col0 // tn
    nb = d // tn
    act = lambda width: pl.BlockSpec((tm, width), lambda n, i: (i, 0))
    wsp = lambda rows, off: pl.BlockSpec((rows, tn), lambda n, i: (0, n + off))
    return pl.pallas_call(
        _merge_kernel,
        grid=(d // tn, m // tm),
        in_specs=[act(ya.shape[1]), act(yb.shape[1]), act(yc.shape[1]), act(d),
                  wsp(proj_pool.shape[0], 0), wsp(proj_fourier.shape[0], 0), wsp(proj_mem.shape[0], 0),
                  wsp(d, gb), wsp(d, gb + nb), wsp(d, gb + 2 * nb)],
        out_specs=pl.BlockSpec((tm, tn), lambda n, i: (i, n)),
        out_shape=jax.ShapeDtypeStruct((m, d), BF16),
        scratch_shapes=[pltpu.VMEM((proj_pool.shape[0], tn), BF16), pltpu.VMEM((proj_fourier.shape[0], tn), BF16),
                        pltpu.VMEM((proj_mem.shape[0], tn), BF16), pltpu.VMEM((d, tn), BF16),
                        pltpu.VMEM((d, tn), BF16), pltpu.VMEM((d, tn), BF16)],
        compiler_params=_cparams(2, 56),
        name="gated_merge",
    )(ya, yb, yc, h, proj_pool, proj_fourier, proj_mem, w_in, w_in, w_in)


def _select_kernel(aff_ref, tri_ref, slot_t_ref, starts_ref, idx_ref, affsel_ref, bounds_ref, slot_ref,
                   *, cap, tile, block):
    a = aff_ref[...]
    rows = a.shape[0]
    capf = float(cap)

    def count(pred):
        return jnp.sum(pred.astype(F32), axis=-1, keepdims=True)

    def body(i, t_bits):
        cand = t_bits | jnp.left_shift(jnp.int32(1), 30 - i)
        return jnp.where(count(a >= pltpu.bitcast(cand, F32)) >= capf, cand, t_bits)

    t = pltpu.bitcast(lax.fori_loop(0, 31, body, jnp.zeros((rows, 1), jnp.int32)), F32)
    gt = a > t
    eq = a == t
    need = capf - count(gt)
    tri = tri_ref[...]
    eq_rank = jnp.dot(eq.astype(BF16), tri, preferred_element_type=F32)
    sel = gt | (eq & (eq_rank < need))
    pos = jnp.dot(sel.astype(BF16), tri, preferred_element_type=F32)
    slot = jnp.where(sel, pos, -1.0)
    slot_ref[...] = slot
    slot_t_ref[...] = slot.T
    s = a.shape[1]
    lane = lax.broadcasted_iota(jnp.int32, (rows, LANES), 1)
    starts = jnp.where(lane == s // tile, capf, 0.0)
    for t in range(s // tile):
        starts = jnp.where(lane == t, pos[:, t * tile:t * tile + 1], starts)
    starts_ref[...] = starts.astype(jnp.int32)

    jcol = lax.broadcasted_iota(jnp.int32, (cap, s), 0).astype(F32)
    tok = lax.broadcasted_iota(jnp.int32, (cap, s), 1).astype(F32)
    lane_r = lax.broadcasted_iota(jnp.int32, (cap, LANES), 1)

    def invert(r, carry):
        idx, aff_sel = carry
        hit = slot_ref[pl.ds(r, 1), :] == jcol
        tok_col = jnp.sum(jnp.where(hit, tok, 0.0), axis=-1, keepdims=True)
        aff_col = jnp.sum(jnp.where(hit, aff_ref[pl.ds(r, 1), :], 0.0), axis=-1, keepdims=True)
        return jnp.where(lane_r == r, tok_col, idx), jnp.where(lane_r == r, aff_col, aff_sel)

    zeros = jnp.zeros((cap, LANES), F32)
    idx, aff_sel = lax.fori_loop(0, rows, invert, (zeros, zeros))
    idx_ref[...] = idx
    affsel_ref[...] = aff_sel
    nblk = cap // block
    sub = lax.broadcasted_iota(jnp.int32, (2 * nblk, LANES), 0)
    bounds = jnp.zeros((2 * nblk, LANES), F32)
    for i in range(nblk):
        bounds = jnp.where(sub == i, idx[i * block:i * block + 1, :], bounds)
        bounds = jnp.where(sub == nblk + i, idx[(i + 1) * block - 1:(i + 1) * block, :] + 1.0, bounds)
    bounds_ref[...] = bounds.astype(jnp.int32)


def _select(aff_rows, *, cap, tile, block):
    rows, s = aff_rows.shape
    assert rows == LANES
    idx = jnp.arange(s, dtype=jnp.int32)
    tri = (idx[:, None] < idx[None, :]).astype(BF16)
    full = lambda shape: pl.BlockSpec(shape, lambda i: (0, 0))
    nb2 = 2 * cap // block
    return pl.pallas_call(
        functools.partial(_select_kernel, cap=cap, tile=tile, block=block),
        grid=(1,),
        in_specs=[full((rows, s)), full((s, s))],
        out_specs=[full((s, rows)), full((rows, LANES)), full((cap, rows)), full((cap, rows)), full((nb2, rows))],
        out_shape=[jax.ShapeDtypeStruct((s, rows), F32), jax.ShapeDtypeStruct((rows, LANES), jnp.int32),
                   jax.ShapeDtypeStruct((cap, rows), F32), jax.ShapeDtypeStruct((cap, rows), F32),
                   jax.ShapeDtypeStruct((nb2, rows), jnp.int32)],
        scratch_shapes=[pltpu.VMEM((rows, s), F32)],
        compiler_params=_cparams(1, 48),
        name="expert_select",
    )(aff_rows, tri)


def _dispatch_kernel(bounds_ref, idx_ref, h_ref, o_ref, *, window, group):
    bi = pl.program_id(0)
    ji = pl.program_id(1)
    nblk = pl.num_programs(1)
    ne, block, d = o_ref.shape
    s = h_ref.shape[1]

    lo = hi = None
    for k in range(ne):
        l = bounds_ref[ji, bi * ne + k]
        h = bounds_ref[nblk + ji, bi * ne + k]
        lo = l if lo is None else jnp.minimum(lo, l)
        hi = h if hi is None else jnp.maximum(hi, h)
    w0 = jnp.minimum(jnp.bitwise_and(lo, -BF16_ROWS), s - window)
    fits = hi - w0 <= window

    rows = idx_ref[pl.ds(pl.multiple_of(ji * block, block), block), :]
    lane = lax.broadcasted_iota(jnp.int32, (block, LANES), 1)

    def gather(width, first_tok, src):
        t = lax.broadcasted_iota(jnp.int32, (group * block, width), 1).astype(F32) + first_tok
        for g0 in range(0, ne, group):
            toks = jnp.concatenate(
                [jnp.sum(jnp.where(lane == bi * ne + k, rows, 0.0), axis=-1, keepdims=True)
                 for k in range(g0, g0 + group)], axis=0)
            out = jnp.dot((toks == t).astype(BF16), src, preferred_element_type=F32)
            o_ref[g0:g0 + group] = out.reshape(group, block, d).astype(o_ref.dtype)

    @pl.when(fits)
    def _():
        gather(window, w0.astype(F32), h_ref[0, pl.ds(pl.multiple_of(w0, BF16_ROWS), window), :])

    @pl.when(jnp.logical_not(fits))
    def _():
        gather(s, 0.0, h_ref[0])


def _dispatch(bounds, idx, h3, *, n_experts, cap, block, window):
    b, s, d = h3.shape
    assert window % BF16_ROWS == 0 and window <= s and cap % block == 0 and block % BF16_ROWS == 0
    nblk = cap // block
    grid_spec = pltpu.PrefetchScalarGridSpec(
        num_scalar_prefetch=1,
        grid=(b, nblk),
        in_specs=[pl.BlockSpec((cap, LANES), lambda i, j, bd: (0, 0)),
                  pl.BlockSpec((1, s, d), lambda i, j, bd: (i, 0, 0))],
        out_specs=pl.BlockSpec((n_experts, block, d), lambda i, j, bd: (0, i * nblk + j, 0)),
    )
    return pl.pallas_call(
        functools.partial(_dispatch_kernel, window=window, group=4),
        grid_spec=grid_spec,
        out_shape=jax.ShapeDtypeStruct((n_experts, b * cap, d), BF16),
        compiler_params=_cparams(2, 48),
        name="dispatch",
    )(bounds, idx, h3)


def _combine_kernel(starts_ref, slot_ref, y_ref, x_ref, g_ref, o_ref, *, win, final_norm):
    bi = pl.program_id(0)
    ti = pl.program_id(1)
    tt, d = x_ref.shape
    ne, cap, _ = y_ref.shape

    wins = []
    fits = None
    for k in range(ne):
        lo = starts_ref[bi * ne + k, ti]
        hi = starts_ref[bi * ne + k, ti + 1]
        w0 = jnp.minimum(jnp.bitwise_and(lo, -BF16_ROWS), cap - win)
        ok = hi - w0 <= win
        fits = ok if fits is None else jnp.logical_and(fits, ok)
        wins.append(w0)

    slots = slot_ref[...].astype(BF16)

    def onehot(width, first_slot):
        shift = width.bit_length() - 1
        n = ne * width
        lane_of = bi * ne + lax.shift_right_logical(lax.broadcasted_iota(jnp.int32, (LANES, n), 1), shift)
        spread = (lax.broadcasted_iota(jnp.int32, (LANES, n), 0) == lane_of).astype(BF16)
        slot_b = jnp.dot(slots, spread, preferred_element_type=F32)
        c = lax.broadcasted_iota(jnp.int32, (1, n), 1)
        target = jnp.bitwise_and(c, width - 1)
        if first_slot is not None:
            kk = lax.shift_right_logical(c, shift)
            for k in range(ne):
                target = target + jnp.where(kk == k, first_slot[k], 0)
        return (slot_b == target.astype(F32)).astype(BF16)

    def finish(contrib):
        x = x_ref[...] + contrib
        if final_norm:
            inv = lax.rsqrt(jnp.mean(x * x, axis=-1, keepdims=True) + EPS)
            x = (x * inv) * g_ref[...]
        o_ref[...] = x

    @pl.when(fits)
    def _():
        ywin = jnp.concatenate([y_ref[k, pl.ds(pl.multiple_of(wins[k], BF16_ROWS), win), :] for k in range(ne)],
                               axis=0)
        finish(jnp.dot(onehot(win, wins), ywin, preferred_element_type=F32))

    @pl.when(jnp.logical_not(fits))
    def _():
        finish(jnp.dot(onehot(cap, None), y_ref[...].reshape(ne * cap, d), preferred_element_type=F32))


def _combine(starts, slot_cols, y, x1, g, *, batch, cap, tt, win, final_norm):
    m, d = x1.shape
    ne = y.shape[0]
    s = m // batch
    tps = s // tt
    assert slot_cols.shape == (s, LANES) and batch * ne == LANES
    assert win % BF16_ROWS == 0 and win <= cap and win & (win - 1) == 0 and cap & (cap - 1) == 0
    grid_spec = pltpu.PrefetchScalarGridSpec(
        num_scalar_prefetch=1,
        grid=(batch, tps),
        in_specs=[pl.BlockSpec((tt, LANES), lambda b, t, st: (t, 0)),
                  pl.BlockSpec((ne, cap, d), lambda b, t, st: (0, b, 0)),
                  pl.BlockSpec((tt, d), lambda b, t, st: (b * tps + t, 0)),
                  pl.BlockSpec((1, d), lambda b, t, st: (0, 0))],
        out_specs=pl.BlockSpec((tt, d), lambda b, t, st: (b * tps + t, 0)),
    )
    return pl.pallas_call(
        functools.partial(_combine_kernel, win=win, final_norm=final_norm),
        grid_spec=grid_spec,
        out_shape=jax.ShapeDtypeStruct((m, d), F32),
        compiler_params=_cparams(2, 56),
        name="combine",
    )(starts, slot_cols, y, x1, g.reshape(1, d))


def _expert_up_kernel(x_ref, wg_ref, wu_ref, o_ref, *, chunk):
    wg = wg_ref[0].astype(BF16)
    wu = wu_ref[0].astype(BF16)
    for r0 in range(0, x_ref.shape[1], chunk):
        x = x_ref[0, pl.ds(r0, chunk), :]
        gate = jnp.dot(x, wg, preferred_element_type=F32)
        up = jnp.dot(x, wu, preferred_element_type=F32)
        o_ref[0, pl.ds(r0, chunk), :] = (jax.nn.silu(gate) * up).astype(o_ref.dtype)


def _expert_up(xin, w_gate, w_up, *, tf):
    e, m, d = xin.shape
    f = w_gate.shape[2]
    return pl.pallas_call(
        functools.partial(_expert_up_kernel, chunk=512),
        grid=(e, f // tf),
        in_specs=[pl.BlockSpec((1, m, d), lambda i, j: (i, 0, 0)),
                  pl.BlockSpec((1, d, tf), lambda i, j: (i, 0, j)),
                  pl.BlockSpec((1, d, tf), lambda i, j: (i, 0, j))],
        out_specs=pl.BlockSpec((1, m, tf), lambda i, j: (i, 0, j)),
        out_shape=jax.ShapeDtypeStruct((e, m, f), BF16),
        compiler_params=_cparams(2, 56),
        name="expert_up",
    )(xin, w_gate, w_up)


def _expert_down_kernel(h_ref, wd_ref, a_ref, o_ref, *, chunk):
    wd = wd_ref[0].astype(BF16)
    for r0 in range(0, h_ref.shape[1], chunk):
        y = jnp.dot(h_ref[0, pl.ds(r0, chunk), :], wd, preferred_element_type=F32)
        o_ref[0, pl.ds(r0, chunk), :] = (y * a_ref[0, pl.ds(r0, chunk), :]).astype(o_ref.dtype)


def _expert_down(hidden, w_down, aff, *, tn):
    e, m, f = hidden.shape
    d = w_down.shape[2]
    return pl.pallas_call(
        functools.partial(_expert_down_kernel, chunk=512),
        grid=(e, d // tn),
        in_specs=[pl.BlockSpec((1, m, f), lambda i, j: (i, 0, 0)),
                  pl.BlockSpec((1, f, tn), lambda i, j: (i, 0, j)),
                  pl.BlockSpec((1, m, 1), lambda i, j: (i, 0, 0))],
        out_specs=pl.BlockSpec((1, m, tn), lambda i, j: (i, 0, j)),
        out_shape=jax.ShapeDtypeStruct((e, m, d), BF16),
        compiler_params=_cparams(2, 56),
        name="expert_down",
    )(hidden, w_down, aff)


def kernel(x, mem, norm_mix_g, norm_mem_g, w_in, pool_w, pool_scale, fourier_w, w_kv_mem, proj_pool, proj_fourier,
           proj_mem, w_out, norm_ffn_g, w_router, w_expert_gate, w_expert_up, w_expert_down, norm_final_g):
    b, s, d = x.shape
    depth = w_in.shape[0]
    n_tok = b * s
    pool_width = pool_w.shape[1] * pool_w.shape[2]
    four_width = fourier_w.shape[1] * fourier_w.shape[2]
    mem_width = proj_mem.shape[1]
    mix_width = pool_width + four_width + mem_width
    n_heads = 4
    e = w_router.shape[2]
    cap = CAPACITY_FACTOR * s // e

    xf = x.reshape(n_tok, d)
    for l in range(depth):
        h, zp, zq = _in_proj(xf, norm_mix_g[l], w_in[l], pool_width=pool_width, mix_width=mix_width, tm=512)
        zq3 = zq.reshape(b, s, mix_width - pool_width)
        ya = _pool_mixer(zp.reshape(b, s, pool_width), pool_w[l], pool_scale[l])
        yb = _fourier_mixer(zq3, fourier_w[l], col_block=0, tk=512)
        memn = _rmsnorm(mem.reshape(-1, d), norm_mem_g[l], tm=512, out_dtype=BF16)
        kv = _matmul(memn, w_kv_mem[l], n_cols=2 * mem_width, tm=1024, tn=512, out_dtype=BF16, name="kv_proj")
        yc = _mem_attention(zq3, kv.reshape(b, -1, 2 * mem_width), col_block=four_width // mem_width,
                            n_heads=n_heads, tm=512)
        merged = _gated_merge(ya.reshape(n_tok, -1), yb.reshape(n_tok, -1), yc.reshape(n_tok, -1), h,
                              proj_pool[l], proj_fourier[l], proj_mem[l], w_in[l],
                              gate_col0=mix_width, tm=1024, tn=256)

        x1, h2, aff_t = _out_proj(merged, w_out[l], xf, norm_ffn_g[l], w_router[l], batch=b, tm=512)
        slot_cols, starts, idx, aff_sel, bounds = _select(aff_t.reshape(b * e, s), cap=cap, tile=COMBINE_TILE,
                                                          block=DISPATCH_BLOCK)
        xin = _dispatch(bounds, idx, h2.reshape(b, s, d), n_experts=e, cap=cap, block=DISPATCH_BLOCK,
                        window=DISPATCH_WINDOW)
        aff_slot = aff_sel.T.reshape(b, e, cap).swapaxes(0, 1).reshape(e, b * cap, 1)
        hidden = _expert_up(xin, w_expert_gate[l], w_expert_up[l], tf=512)
        y = _expert_down(hidden, w_expert_down[l], aff_slot, tn=512)
        last = l + 1 == depth
        xf = _combine(starts[:, :s // COMBINE_TILE + 1], slot_cols, y, x1, norm_final_g, batch=b, cap=cap,
                      tt=COMBINE_TILE, win=COMBINE_WINDOW, final_norm=last)
    return xf.reshape(b, s, d)
```

```python
import functools

import numpy as np
import jax
import jax.numpy as jnp
from jax import lax
from jax.experimental import pallas as pl
from jax.experimental.pallas import tpu as pltpu

F32 = jnp.float32
BF16 = jnp.bfloat16

EPS = 1e-6
POOL_WINDOWS = (2, 4, 8, 16)
N_EXPERTS = 16
CAPACITY_FACTOR = 2
LANES = 128
BF16_ROWS = 16
POOL_HALO = 16
ROW_CHUNK = 256
DISPATCH_BLOCK = 64
DISPATCH_WINDOW = 768
COMBINE_TILE = 256
COMBINE_WINDOW = 64
MIB = 1024 * 1024


def _cparams(n_axes, vmem_mib):
    return pltpu.CompilerParams(
        dimension_semantics=("arbitrary",) * n_axes,
        vmem_limit_bytes=vmem_mib * MIB,
    )


def _rmsnorm_kernel(x_ref, g_ref, o_ref):
    x = x_ref[...]
    inv = lax.rsqrt(jnp.mean(x * x, axis=-1, keepdims=True) + EPS)
    o_ref[...] = ((x * inv) * g_ref[...]).astype(o_ref.dtype)


def _rmsnorm(x, g, *, tm, out_dtype):
    m, d = x.shape
    return pl.pallas_call(
        _rmsnorm_kernel,
        grid=(m // tm,),
        in_specs=[pl.BlockSpec((tm, d), lambda i: (i, 0)), pl.BlockSpec((1, d), lambda i: (0, 0))],
        out_specs=pl.BlockSpec((tm, d), lambda i: (i, 0)),
        out_shape=jax.ShapeDtypeStruct((m, d), out_dtype),
        compiler_params=_cparams(1, 40),
        name="rmsnorm",
    )(x, g.reshape(1, d))


def _mm_kernel(a_ref, w_ref, o_ref, wc_ref):
    @pl.when(pl.program_id(1) == 0)
    def _():
        wc_ref[...] = w_ref[...].astype(BF16)

    o_ref[...] = jnp.dot(a_ref[...], wc_ref[...], preferred_element_type=F32).astype(o_ref.dtype)


def _matmul(a, w, *, n_cols, tm, tn, out_dtype, name):
    m, k = a.shape
    return pl.pallas_call(
        _mm_kernel,
        grid=(n_cols // tn, m // tm),
        in_specs=[pl.BlockSpec((tm, k), lambda n, i: (i, 0)), pl.BlockSpec((k, tn), lambda n, i: (0, n))],
        out_specs=pl.BlockSpec((tm, tn), lambda n, i: (i, n)),
        out_shape=jax.ShapeDtypeStruct((m, n_cols), out_dtype),
        scratch_shapes=[pltpu.VMEM((k, tn), BF16)],
        compiler_params=_cparams(2, 48),
        name=name,
    )(a, w)


def _in_proj_kernel(x_ref, g_ref, w_ref, h_ref, zp_ref, zq_ref, wc_ref, *, chunk):
    @pl.when(pl.program_id(0) == 0)
    def _():
        wc_ref[...] = w_ref[...].astype(BF16)

    wp = zp_ref.shape[1]
    for r0 in range(0, x_ref.shape[0], ROW_CHUNK):
        rs = pl.ds(r0, ROW_CHUNK)
        x = x_ref[rs, :]
        inv = lax.rsqrt(jnp.mean(x * x, axis=-1, keepdims=True) + EPS)
        hb = ((x * inv) * g_ref[...]).astype(BF16)
        h_ref[rs, :] = hb
        for c0 in range(0, wc_ref.shape[1], chunk):
            z = jnp.dot(hb, wc_ref[:, c0:c0 + chunk], preferred_element_type=F32)
            if c0 < wp:
                zp_ref[rs, c0:c0 + chunk] = z
            else:
                zq_ref[rs, c0 - wp:c0 - wp + chunk] = z.astype(BF16)


def _in_proj(x, g, w_in, *, pool_width, mix_width, tm):
    m, d = x.shape
    row = lambda width: pl.BlockSpec((tm, width), lambda i: (i, 0))
    return pl.pallas_call(
        functools.partial(_in_proj_kernel, chunk=512),
        grid=(m // tm,),
        in_specs=[row(d), pl.BlockSpec((1, d), lambda i: (0, 0)),
                  pl.BlockSpec((d, mix_width), lambda i: (0, 0), pipeline_mode=pl.Buffered(1))],
        out_specs=[row(d), row(pool_width), row(mix_width - pool_width)],
        out_shape=[jax.ShapeDtypeStruct((m, d), BF16), jax.ShapeDtypeStruct((m, pool_width), F32),
                   jax.ShapeDtypeStruct((m, mix_width - pool_width), BF16)],
        scratch_shapes=[pltpu.VMEM((d, mix_width), BF16)],
        compiler_params=_cparams(1, 56),
        name="in_proj",
    )(x, g.reshape(1, d), w_in)


def _out_proj_kernel(m_ref, w_ref, r_ref, g_ref, wr_ref, x1_ref, h2_ref, afft_ref, wc_ref, *, chunk, n_experts):
    @pl.when(pl.program_id(0) == 0)
    def _():
        wc_ref[...] = w_ref[...].astype(BF16)

    wr = wr_ref[...].astype(BF16)
    for r0 in range(0, m_ref.shape[0], ROW_CHUNK):
        rs = pl.ds(r0, ROW_CHUNK)
        a = m_ref[rs, :]
        for c0 in range(0, wc_ref.shape[1], chunk):
            cols = slice(c0, c0 + chunk)
            x1_ref[rs, cols] = r_ref[rs, cols] + jnp.dot(a, wc_ref[:, cols], preferred_element_type=F32)

        x = x1_ref[rs, :]
        inv = lax.rsqrt(jnp.mean(x * x, axis=-1, keepdims=True) + EPS)
        hb = ((x * inv) * g_ref[...]).astype(BF16)
        h2_ref[rs, :] = hb
        logits = jnp.dot(hb, wr, preferred_element_type=F32)
        lane = lax.broadcasted_iota(jnp.int32, logits.shape, 1)
        logits = jnp.where(lane < n_experts, logits, -1e30)
        mx = jnp.max(logits, axis=-1, keepdims=True)
        p = jnp.exp(logits - mx)
        aff = p / jnp.sum(p, axis=-1, keepdims=True)
        afft_ref[0, :, rs] = aff.T[:n_experts, :]


def _out_proj(merged, w_out, resid, g, w_router, *, batch, tm):
    m, d = resid.shape
    e = w_router.shape[1]
    s = m // batch
    tps = s // tm
    wr = jnp.pad(w_router, ((0, 0), (0, LANES - e)))
    row = pl.BlockSpec((tm, d), lambda i: (i, 0))
    return pl.pallas_call(
        functools.partial(_out_proj_kernel, chunk=512, n_experts=e),
        grid=(m // tm,),
        in_specs=[row, pl.BlockSpec((d, d), lambda i: (0, 0), pipeline_mode=pl.Buffered(1)), row,
                  pl.BlockSpec((1, d), lambda i: (0, 0)), pl.BlockSpec((d, LANES), lambda i: (0, 0))],
        out_specs=[row, row, pl.BlockSpec((1, e, tm), lambda i: (i // tps, 0, i % tps))],
        out_shape=[jax.ShapeDtypeStruct((m, d), F32), jax.ShapeDtypeStruct((m, d), BF16),
                   jax.ShapeDtypeStruct((batch, e, s), F32)],
        scratch_shapes=[pltpu.VMEM((d, d), BF16)],
        compiler_params=_cparams(1, 56),
        name="out_proj",
    )(merged, w_out, resid, g.reshape(1, d), wr)


def _pool_kernel(u_ref, pw_ref, ps_ref, o_ref, pad_ref, *, chunk):
    s = u_ref.shape[1]
    c = pw_ref.shape[1]
    h = POOL_HALO
    zeros = jnp.zeros((h, c), F32)
    pad_ref[pl.ds(0, h), :] = zeros
    pad_ref[pl.ds(h + s, h), :] = zeros
    for g, w in enumerate(POOL_WINDOWS):
        cols = slice(g * c, (g + 1) * c)
        pad_ref[pl.ds(h, s), :] = u_ref[0, :, cols]
        wg = pw_ref[g].astype(BF16)
        scale = ps_ref[:, cols]
        for r0 in range(0, s, chunk):
            pos = r0 + lax.broadcasted_iota(jnp.int32, (chunk, 1), 0)
            lo = jnp.maximum(pos - w // 2, 0)
            hi = jnp.minimum(pos + (w - w // 2), s)
            cnt = (hi - lo).astype(F32)
            tot = pad_ref[pl.ds(h + r0 - w // 2, chunk), :]
            for k in range(1, w):
                tot = tot + pad_ref[pl.ds(h + r0 - w // 2 + k, chunk), :]
            pooled = tot / cnt - pad_ref[pl.ds(h + r0, chunk), :]
            y = jnp.dot(pooled.astype(BF16), wg, preferred_element_type=F32)
            o_ref[0, pl.ds(r0, chunk), cols] = (y * scale).astype(o_ref.dtype)


def _pool_mixer(z3, pool_w, pool_scale):
    b, s, _ = z3.shape
    g, c, _ = pool_w.shape
    width = g * c
    return pl.pallas_call(
        functools.partial(_pool_kernel, chunk=256),
        grid=(b,),
        in_specs=[pl.BlockSpec((1, s, width), lambda i: (i, 0, 0)),
                  pl.BlockSpec((g, c, c), lambda i: (0, 0, 0)),
                  pl.BlockSpec((1, width), lambda i: (0, 0))],
        out_specs=pl.BlockSpec((1, s, width), lambda i: (i, 0, 0)),
        out_shape=jax.ShapeDtypeStruct((b, s, width), BF16),
        scratch_shapes=[pltpu.VMEM((s + 2 * POOL_HALO, c), F32)],
        compiler_params=_cparams(1, 48),
        name="pool_mixer",
    )(z3, pool_w, pool_scale.reshape(1, width))


def _dft_tables(s, c):
    def tab(n):
        r = np.outer(np.arange(n), np.arange(n)) % n
        ang = r * (2.0 * np.pi / n)
        return np.cos(ang), np.sin(ang)
    cs, ss = tab(s)
    cc, sc = tab(c)
    return (jnp.asarray(np.concatenate([cs, -ss], axis=1), dtype=F32).astype(BF16),
            jnp.asarray(np.concatenate([cc, sc], axis=1), dtype=F32).astype(BF16))


def _fourier_kernel(u_ref, dft_ref, ccsc_ref, fw_ref, o_ref, t_ref, *, norm):
    s = u_ref.shape[1]
    ng, c, _ = fw_ref.shape

    @pl.when(pl.program_id(1) == 0)
    def _():
        for g in range(ng):
            ug = u_ref[0, :, g * c:(g + 1) * c]
            ab = jnp.dot(ug, ccsc_ref[...], preferred_element_type=F32)
            t_ref[pl.ds(0, s), g * c:(g + 1) * c] = ab[:, :c].astype(BF16)
            t_ref[pl.ds(s, s), g * c:(g + 1) * c] = ab[:, c:].astype(BF16)

    f = jnp.dot(dft_ref[...], t_ref[...], preferred_element_type=F32) * norm
    for g in range(ng):
        y = jnp.dot(f[:, g * c:(g + 1) * c].astype(BF16), fw_ref[g].astype(BF16), preferred_element_type=F32)
        o_ref[0, :, g * c:(g + 1) * c] = y.astype(o_ref.dtype)


def _fourier_mixer(z3, fourier_w, *, col_block, tk):
    b, s, _ = z3.shape
    ng, c, _ = fourier_w.shape
    width = ng * c
    dft, ccsc = _dft_tables(s, c)
    return pl.pallas_call(
        functools.partial(_fourier_kernel, norm=float((s * c) ** -0.5)),
        grid=(b, s // tk),
        in_specs=[pl.BlockSpec((1, s, width), lambda i, k: (i, 0, col_block)),
                  pl.BlockSpec((tk, 2 * s), lambda i, k: (k, 0)),
                  pl.BlockSpec((c, 2 * c), lambda i, k: (0, 0)),
                  pl.BlockSpec((ng, c, c), lambda i, k: (0, 0, 0))],
        out_specs=pl.BlockSpec((1, tk, width), lambda i, k: (i, k, 0)),
        out_shape=jax.ShapeDtypeStruct((b, s, width), BF16),
        scratch_shapes=[pltpu.VMEM((2 * s, width), BF16)],
        compiler_params=_cparams(2, 48),
        name="fourier_mixer",
    )(z3, dft, ccsc, fourier_w)


def _attn_kernel(q_ref, kv_ref, o_ref, *, n_heads, scale):
    dh = q_ref.shape[2] // n_heads
    width = n_heads * dh
    for h in range(n_heads):
        q = q_ref[0, :, h * dh:(h + 1) * dh]
        k = kv_ref[0, :, h * dh:(h + 1) * dh]
        v = kv_ref[0, :, width + h * dh:width + (h + 1) * dh]
        sc = lax.dot_general(q, k, (((1,), (1,)), ((), ())), preferred_element_type=F32) * scale
        mx = jnp.max(sc, axis=-1, keepdims=True)
        p = jnp.exp(sc - mx)
        p = p / jnp.sum(p, axis=-1, keepdims=True)
        o = jnp.dot(p.astype(BF16), v, preferred_element_type=F32)
        o_ref[0, :, h * dh:(h + 1) * dh] = o.astype(o_ref.dtype)


def _mem_attention(z3, kv3, *, col_block, n_heads, tm):
    b, s, _ = z3.shape
    _, m, kvw = kv3.shape
    width = kvw // 2
    return pl.pallas_call(
        functools.partial(_attn_kernel, n_heads=n_heads, scale=float((width // n_heads) ** -0.5)),
        grid=(b, s // tm),
        in_specs=[pl.BlockSpec((1, tm, width), lambda i, j: (i, j, col_block)),
                  pl.BlockSpec((1, m, kvw), lambda i, j: (i, 0, 0))],
        out_specs=pl.BlockSpec((1, tm, width), lambda i, j: (i, j, 0)),
        out_shape=jax.ShapeDtypeStruct((b, s, width), BF16),
        compiler_params=_cparams(2, 40),
        name="mem_attention",
    )(z3, kv3)


def _merge_kernel(ya_ref, yb_ref, yc_ref, h_ref, pp_ref, pf_ref, pm_ref, g0_ref, g1_ref, g2_ref, o_ref,
                  cpp, cpf, cpm, cg0, cg1, cg2):
    @pl.when(pl.program_id(1) == 0)
    def _():
        for src, dst in ((pp_ref, cpp), (pf_ref, cpf), (pm_ref, cpm), (g0_ref, cg0), (g1_ref, cg1), (g2_ref, cg2)):
            dst[...] = src[...].astype(BF16)

    h = h_ref[...]

    def branch(y_ref, proj, gate_w):
        gate = jax.nn.sigmoid(jnp.dot(h, gate_w[...], preferred_element_type=F32))
        return gate * jnp.dot(y_ref[...], proj[...], preferred_element_type=F32)

    acc = branch(ya_ref, cpp, cg0)
    acc = acc + branch(yb_ref, cpf, cg1)
    acc = acc + branch(yc_ref, cpm, cg2)
    o_ref[...] = acc.astype(o_ref.dtype)


def _gated_merge(ya, yb, yc, h, proj_pool, proj_fourier, proj_mem, w_in, *, gate_col0, tm, tn):
    m, d = h.shape
    gb = gate_col0 // tn
    nb = d // tn
    act = lambda width: pl.BlockSpec((tm, width), lambda n, i: (i, 0))
    wsp = lambda rows, off: pl.BlockSpec((rows, tn), lambda n, i: (0, n + off))
    return pl.pallas_call(
        _merge_kernel,
        grid=(d // tn, m // tm),
        in_specs=[act(ya.shape[1]), act(yb.shape[1]), act(yc.shape[1]), act(d),
                  wsp(proj_pool.shape[0], 0), wsp(proj_fourier.shape[0], 0), wsp(proj_mem.shape[0], 0),
                  wsp(d, gb), wsp(d, gb + nb), wsp(d, gb + 2 * nb)],
        out_specs=pl.BlockSpec((tm, tn), lambda n, i: (i, n)),
        out_shape=jax.ShapeDtypeStruct((m, d), BF16),
        scratch_shapes=[pltpu.VMEM((proj_pool.shape[0], tn), BF16), pltpu.VMEM((proj_fourier.shape[0], tn), BF16),
                        pltpu.VMEM((proj_mem.shape[0], tn), BF16), pltpu.VMEM((d, tn), BF16),
                        pltpu.VMEM((d, tn), BF16), pltpu.VMEM((d, tn), BF16)],
        compiler_params=_cparams(2, 56),
        name="gated_merge",
    )(ya, yb, yc, h, proj_pool, proj_fourier, proj_mem, w_in, w_in, w_in)


def _select_kernel(aff_ref, tri_ref, slot_t_ref, starts_ref, idx_ref, bounds_ref, slot_ref,
                   *, cap, tile, block):
    a = aff_ref[...]
    rows = a.shape[0]
    capf = float(cap)

    def count(pred):
        return jnp.sum(pred.astype(F32), axis=-1, keepdims=True)

    def body(i, t_bits):
        cand = t_bits | jnp.left_shift(jnp.int32(1), 30 - i)
        return jnp.where(count(a >= pltpu.bitcast(cand, F32)) >= capf, cand, t_bits)

    t = pltpu.bitcast(lax.fori_loop(0, 31, body, jnp.zeros((rows, 1), jnp.int32)), F32)
    gt = a > t
    eq = a == t
    need = capf - count(gt)
    tri = tri_ref[...]
    eq_rank = jnp.dot(eq.astype(BF16), tri, preferred_element_type=F32)
    sel = gt | (eq & (eq_rank < need))
    pos = jnp.dot(sel.astype(BF16), tri, preferred_element_type=F32)
    slot = jnp.where(sel, pos, -1.0)
    slot_ref[...] = slot
    slot_t_ref[...] = slot.T
    s = a.shape[1]
    lane = lax.broadcasted_iota(jnp.int32, (rows, LANES), 1)
    starts = jnp.where(lane == s // tile, capf, 0.0)
    for t in range(s // tile):
        starts = jnp.where(lane == t, pos[:, t * tile:t * tile + 1], starts)
    starts_ref[...] = starts.astype(jnp.int32)

    jcol = lax.broadcasted_iota(jnp.int32, (cap, 1), 0).astype(F32)
    tok = lax.broadcasted_iota(jnp.int32, (1, s), 1).astype(F32)
    lane_r = lax.broadcasted_iota(jnp.int32, (cap, LANES), 1)

    def invert(r, idx):
        hit = slot_ref[pl.ds(r, 1), :] == jcol
        tok_col = jnp.sum(jnp.where(hit, tok, 0.0), axis=-1, keepdims=True)
        return jnp.where(lane_r == r, tok_col, idx)

    idx = lax.fori_loop(0, rows, invert, jnp.zeros((cap, LANES), F32))
    idx_ref[...] = idx
    nblk = cap // block
    sub = lax.broadcasted_iota(jnp.int32, (2 * nblk, LANES), 0)
    bounds = jnp.zeros((2 * nblk, LANES), F32)
    for i in range(nblk):
        bounds = jnp.where(sub == i, idx[i * block:i * block + 1, :], bounds)
        bounds = jnp.where(sub == nblk + i, idx[(i + 1) * block - 1:(i + 1) * block, :] + 1.0, bounds)
    bounds_ref[...] = bounds.astype(jnp.int32)


def _select(aff_rows, *, cap, tile, block):
    rows, s = aff_rows.shape
    assert rows == LANES
    idx = jnp.arange(s, dtype=jnp.int32)
    tri = (idx[:, None] < idx[None, :]).astype(BF16)
    full = lambda shape: pl.BlockSpec(shape, lambda i: (0, 0))
    nb2 = 2 * cap // block
    return pl.pallas_call(
        functools.partial(_select_kernel, cap=cap, tile=tile, block=block),
        grid=(1,),
        in_specs=[full((rows, s)), full((s, s))],
        out_specs=[full((s, rows)), full((rows, LANES)), full((cap, rows)), full((nb2, rows))],
        out_shape=[jax.ShapeDtypeStruct((s, rows), F32), jax.ShapeDtypeStruct((rows, LANES), jnp.int32),
                   jax.ShapeDtypeStruct((cap, rows), F32), jax.ShapeDtypeStruct((nb2, rows), jnp.int32)],
        scratch_shapes=[pltpu.VMEM((rows, s), F32)],
        compiler_params=_cparams(1, 48),
        name="expert_select",
    )(aff_rows, tri)


def _dispatch_kernel(bounds_ref, idx_ref, aff_ref, h_ref, o_ref, oa_ref, *, window, group):
    bi = pl.program_id(0)
    ji = pl.program_id(1)
    nblk = pl.num_programs(1)
    ne, block, d = o_ref.shape
    s = h_ref.shape[1]

    lo = hi = None
    for k in range(ne):
        l = bounds_ref[ji, bi * ne + k]
        h = bounds_ref[nblk + ji, bi * ne + k]
        lo = l if lo is None else jnp.minimum(lo, l)
        hi = h if hi is None else jnp.maximum(hi, h)
    w0 = jnp.minimum(jnp.bitwise_and(lo, -BF16_ROWS), s - window)
    fits = hi - w0 <= window

    rows = idx_ref[pl.ds(pl.multiple_of(ji * block, block), block), :]
    lane = lax.broadcasted_iota(jnp.int32, (block, LANES), 1)
    tok_all = lax.broadcasted_iota(jnp.int32, (1, s), 1).astype(F32)

    def gather(width, first_tok, src):
        t = lax.broadcasted_iota(jnp.int32, (group * block, width), 1).astype(F32) + first_tok
        for g0 in range(0, ne, group):
            cols = [jnp.sum(jnp.where(lane == bi * ne + k, rows, 0.0), axis=-1, keepdims=True)
                    for k in range(g0, g0 + group)]
            toks = jnp.concatenate(cols, axis=0)
            out = jnp.dot((toks == t).astype(BF16), src, preferred_element_type=F32)
            o_ref[g0:g0 + group] = out.reshape(group, block, d).astype(o_ref.dtype)
            for k, col in zip(range(g0, g0 + group), cols):
                oa_ref[k] = jnp.sum(jnp.where(col == tok_all, aff_ref[0, k:k + 1, :], 0.0), axis=-1, keepdims=True)

    @pl.when(fits)
    def _():
        gather(window, w0.astype(F32), h_ref[0, pl.ds(pl.multiple_of(w0, BF16_ROWS), window), :])

    @pl.when(jnp.logical_not(fits))
    def _():
        gather(s, 0.0, h_ref[0])


def _dispatch(bounds, idx, aff_t, h3, *, n_experts, cap, block, window):
    b, s, d = h3.shape
    assert window % BF16_ROWS == 0 and window <= s and cap % block == 0 and block % BF16_ROWS == 0
    nblk = cap // block
    grid_spec = pltpu.PrefetchScalarGridSpec(
        num_scalar_prefetch=1,
        grid=(b, nblk),
        in_specs=[pl.BlockSpec((cap, LANES), lambda i, j, bd: (0, 0)),
                  pl.BlockSpec((1, n_experts, s), lambda i, j, bd: (i, 0, 0)),
                  pl.BlockSpec((1, s, d), lambda i, j, bd: (i, 0, 0))],
        out_specs=[pl.BlockSpec((n_experts, block, d), lambda i, j, bd: (0, i * nblk + j, 0)),
                   pl.BlockSpec((n_experts, block, 1), lambda i, j, bd: (0, i * nblk + j, 0))],
    )
    return pl.pallas_call(
        functools.partial(_dispatch_kernel, window=window, group=4),
        grid_spec=grid_spec,
        out_shape=[jax.ShapeDtypeStruct((n_experts, b * cap, d), BF16),
                   jax.ShapeDtypeStruct((n_experts, b * cap, 1), F32)],
        compiler_params=_cparams(2, 48),
        name="dispatch",
    )(bounds, idx, aff_t, h3)


def _combine_kernel(starts_ref, slot_ref, y_ref, x_ref, g_ref, o_ref, *, win, final_norm):
    bi = pl.program_id(0)
    ti = pl.program_id(1)
    tt, d = x_ref.shape
    ne, cap, _ = y_ref.shape

    wins = []
    fits = None
    for k in range(ne):
        lo = starts_ref[bi * ne + k, ti]
        hi = starts_ref[bi * ne + k, ti + 1]
        w0 = jnp.minimum(jnp.bitwise_and(lo, -BF16_ROWS), cap - win)
        ok = hi - w0 <= win
        fits = ok if fits is None else jnp.logical_and(fits, ok)
        wins.append(w0)

    slots = slot_ref[...].astype(BF16)

    def onehot(width, first_slot):
        shift = width.bit_length() - 1
        n = ne * width
        lane_of = bi * ne + lax.shift_right_logical(lax.broadcasted_iota(jnp.int32, (LANES, n), 1), shift)
        spread = (lax.broadcasted_iota(jnp.int32, (LANES, n), 0) == lane_of).astype(BF16)
        slot_b = jnp.dot(slots, spread, preferred_element_type=F32)
        c = lax.broadcasted_iota(jnp.int32, (1, n), 1)
        target = jnp.bitwise_and(c, width - 1)
        if first_slot is not None:
            kk = lax.shift_right_logical(c, shift)
            for k in range(ne):
                target = target + jnp.where(kk == k, first_slot[k], 0)
        return (slot_b == target.astype(F32)).astype(BF16)

    def finish(contrib):
        x = x_ref[...] + contrib
        if final_norm:
            inv = lax.rsqrt(jnp.mean(x * x, axis=-1, keepdims=True) + EPS)
            x = (x * inv) * g_ref[...]
        o_ref[...] = x

    @pl.when(fits)
    def _():
        ywin = jnp.concatenate([y_ref[k, pl.ds(pl.multiple_of(wins[k], BF16_ROWS), win), :] for k in range(ne)],
                               axis=0)
        finish(jnp.dot(onehot(win, wins), ywin, preferred_element_type=F32))

    @pl.when(jnp.logical_not(fits))
    def _():
        finish(jnp.dot(onehot(cap, None), y_ref[...].reshape(ne * cap, d), preferred_element_type=F32))


def _combine(starts, slot_cols, y, x1, g, *, batch, cap, tt, win, final_norm):
    m, d = x1.shape
    ne = y.shape[0]
    s = m // batch
    tps = s // tt
    assert slot_cols.shape == (s, LANES) and batch * ne == LANES
    assert win % BF16_ROWS == 0 and win <= cap and win & (win - 1) == 0 and cap & (cap - 1) == 0
    grid_spec = pltpu.PrefetchScalarGridSpec(
        num_scalar_prefetch=1,
        grid=(batch, tps),
        in_specs=[pl.BlockSpec((tt, LANES), lambda b, t, st: (t, 0)),
                  pl.BlockSpec((ne, cap, d), lambda b, t, st: (0, b, 0)),
                  pl.BlockSpec((tt, d), lambda b, t, st: (b * tps + t, 0)),
                  pl.BlockSpec((1, d), lambda b, t, st: (0, 0))],
        out_specs=pl.BlockSpec((tt, d), lambda b, t, st: (b * tps + t, 0)),
    )
    return pl.pallas_call(
        functools.partial(_combine_kernel, win=win, final_norm=final_norm),
        grid_spec=grid_spec,
        out_shape=jax.ShapeDtypeStruct((m, d), F32),
        compiler_params=_cparams(2, 56),
        name="combine",
    )(starts, slot_cols, y, x1, g.reshape(1, d))


def _expert_up_kernel(x_ref, wg_ref, wu_ref, o_ref, *, chunk):
    wg = wg_ref[0].astype(BF16)
    wu = wu_ref[0].astype(BF16)
    for r0 in range(0, x_ref.shape[1], chunk):
        x = x_ref[0, pl.ds(r0, chunk), :]
        gate = jnp.dot(x, wg, preferred_element_type=F32)
        up = jnp.dot(x, wu, preferred_element_type=F32)
        o_ref[0, pl.ds(r0, chunk), :] = (jax.nn.silu(gate) * up).astype(o_ref.dtype)


def _expert_up(xin, w_gate, w_up, *, tf):
    e, m, d = xin.shape
    f = w_gate.shape[2]
    return pl.pallas_call(
        functools.partial(_expert_up_kernel, chunk=512),
        grid=(e, f // tf),
        in_specs=[pl.BlockSpec((1, m, d), lambda i, j: (i, 0, 0)),
                  pl.BlockSpec((1, d, tf), lambda i, j: (i, 0, j)),
                  pl.BlockSpec((1, d, tf), lambda i, j: (i, 0, j))],
        out_specs=pl.BlockSpec((1, m, tf), lambda i, j: (i, 0, j)),
        out_shape=jax.ShapeDtypeStruct((e, m, f), BF16),
        compiler_params=_cparams(2, 56),
        name="expert_up",
    )(xin, w_gate, w_up)


def _expert_down_kernel(h_ref, wd_ref, a_ref, o_ref, *, chunk):
    wd = wd_ref[0].astype(BF16)
    for r0 in range(0, h_ref.shape[1], chunk):
        y = jnp.dot(h_ref[0, pl.ds(r0, chunk), :], wd, preferred_element_type=F32)
        o_ref[0, pl.ds(r0, chunk), :] = (y * a_ref[0, pl.ds(r0, chunk), :]).astype(o_ref.dtype)


def _expert_down(hidden, w_down, aff, *, tn):
    e, m, f = hidden.shape
    d = w_down.shape[2]
    return pl.pallas_call(
        functools.partial(_expert_down_kernel, chunk=512),
        grid=(e, d // tn),
        in_specs=[pl.BlockSpec((1, m, f), lambda i, j: (i, 0, 0)),
                  pl.BlockSpec((1, f, tn), lambda i, j: (i, 0, j)),
                  pl.BlockSpec((1, m, 1), lambda i, j: (i, 0, 0))],
        out_specs=pl.BlockSpec((1, m, tn), lambda i, j: (i, 0, j)),
        out_shape=jax.ShapeDtypeStruct((e, m, d), BF16),
        compiler_params=_cparams(2, 56),
        name="expert_down",
    )(hidden, w_down, aff)


def kernel(x, mem, norm_mix_g, norm_mem_g, w_in, pool_w, pool_scale, fourier_w, w_kv_mem, proj_pool, proj_fourier,
           proj_mem, w_out, norm_ffn_g, w_router, w_expert_gate, w_expert_up, w_expert_down, norm_final_g):
    b, s, d = x.shape
    depth = w_in.shape[0]
    n_tok = b * s
    pool_width = pool_w.shape[1] * pool_w.shape[2]
    four_width = fourier_w.shape[1] * fourier_w.shape[2]
    mem_width = proj_mem.shape[1]
    mix_width = pool_width + four_width + mem_width
    n_heads = 4
    e = w_router.shape[2]
    cap = CAPACITY_FACTOR * s // e

    xf = x.reshape(n_tok, d)
    for l in range(depth):
        h, zp, zq = _in_proj(xf, norm_mix_g[l], w_in[l], pool_width=pool_width, mix_width=mix_width, tm=512)
        zq3 = zq.reshape(b, s, mix_width - pool_width)
        ya = _pool_mixer(zp.reshape(b, s, pool_width), pool_w[l], pool_scale[l])
        yb = _fourier_mixer(zq3, fourier_w[l], col_block=0, tk=512)
        memn = _rmsnorm(mem.reshape(-1, d), norm_mem_g[l], tm=512, out_dtype=BF16)
        kv = _matmul(memn, w_kv_mem[l], n_cols=2 * mem_width, tm=1024, tn=512, out_dtype=BF16, name="kv_proj")
        yc = _mem_attention(zq3, kv.reshape(b, -1, 2 * mem_width), col_block=four_width // mem_width,
                            n_heads=n_heads, tm=512)
        merged = _gated_merge(ya.reshape(n_tok, -1), yb.reshape(n_tok, -1), yc.reshape(n_tok, -1), h,
                              proj_pool[l], proj_fourier[l], proj_mem[l], w_in[l],
                              gate_col0=mix_width, tm=1024, tn=256)

        x1, h2, aff_t = _out_proj(merged, w_out[l], xf, norm_ffn_g[l], w_router[l], batch=b, tm=512)
        slot_cols, starts, idx, bounds = _select(aff_t.reshape(b * e, s), cap=cap, tile=COMBINE_TILE,
                                                 block=DISPATCH_BLOCK)
        xin, aff_slot = _dispatch(bounds, idx, aff_t, h2.reshape(b, s, d), n_experts=e, cap=cap,
                                  block=DISPATCH_BLOCK, window=DISPATCH_WINDOW)
        hidden = _expert_up(xin, w_expert_gate[l], w_expert_up[l], tf=512)
        y = _expert_down(hidden, w_expert_down[l], aff_slot, tn=512)
        last = l + 1 == depth
        xf = _combine(starts[:, :s // COMBINE_TILE + 1], slot_cols, y, x1, norm_final_g, batch=b, cap=cap,
                      tt=COMBINE_TILE, win=COMBINE_WINDOW, final_norm=last)
    return xf.reshape(b, s, d)
```

```python
import functools

import numpy as np
import jax
import jax.numpy as jnp
from jax import lax
from jax.experimental import pallas as pl
from jax.experimental.pallas import tpu as pltpu

F32 = jnp.float32
BF16 = jnp.bfloat16

EPS = 1e-6
POOL_WINDOWS = (2, 4, 8, 16)
N_EXPERTS = 16
CAPACITY_FACTOR = 2
LANES = 128
BF16_ROWS = 16
POOL_HALO = 16
ROW_CHUNK = 256
DISPATCH_BLOCK = 64
DISPATCH_WINDOW = 768
COMBINE_TILE = 256
COMBINE_WINDOW = 64
MIB = 1024 * 1024


def _cparams(n_axes, vmem_mib):
    return pltpu.CompilerParams(
        dimension_semantics=("arbitrary",) * n_axes,
        vmem_limit_bytes=vmem_mib * MIB,
    )


def _rmsnorm_kernel(x_ref, g_ref, o_ref):
    x = x_ref[...]
    inv = lax.rsqrt(jnp.mean(x * x, axis=-1, keepdims=True) + EPS)
    o_ref[...] = ((x * inv) * g_ref[...]).astype(o_ref.dtype)


def _rmsnorm(x, g, *, tm, out_dtype):
    m, d = x.shape
    return pl.pallas_call(
        _rmsnorm_kernel,
        grid=(m // tm,),
        in_specs=[pl.BlockSpec((tm, d), lambda i: (i, 0)), pl.BlockSpec((1, d), lambda i: (0, 0))],
        out_specs=pl.BlockSpec((tm, d), lambda i: (i, 0)),
        out_shape=jax.ShapeDtypeStruct((m, d), out_dtype),
        compiler_params=_cparams(1, 40),
        name="rmsnorm",
    )(x, g.reshape(1, d))


def _mm_kernel(a_ref, w_ref, o_ref, wc_ref):
    @pl.when(pl.program_id(1) == 0)
    def _():
        wc_ref[...] = w_ref[...].astype(BF16)

    o_ref[...] = jnp.dot(a_ref[...], wc_ref[...], preferred_element_type=F32).astype(o_ref.dtype)


def _matmul(a, w, *, n_cols, tm, tn, out_dtype, name):
    m, k = a.shape
    return pl.pallas_call(
        _mm_kernel,
        grid=(n_cols // tn, m // tm),
        in_specs=[pl.BlockSpec((tm, k), lambda n, i: (i, 0)), pl.BlockSpec((k, tn), lambda n, i: (0, n))],
        out_specs=pl.BlockSpec((tm, tn), lambda n, i: (i, n)),
        out_shape=jax.ShapeDtypeStruct((m, n_cols), out_dtype),
        scratch_shapes=[pltpu.VMEM((k, tn), BF16)],
        compiler_params=_cparams(2, 48),
        name=name,
    )(a, w)


def _in_proj_kernel(x_ref, g_ref, w_ref, kv_ref, h_ref, zp_ref, zf_ref, yc_ref, wc_ref, *, n_heads):
    @pl.when(pl.program_id(0) == 0)
    def _():
        wc_ref[...] = w_ref[...].astype(BF16)

    wp = zp_ref.shape[1]
    wf = zf_ref.shape[1]
    wm = yc_ref.shape[1]
    dh = wm // n_heads
    scale = float(dh ** -0.5)
    chunks = [pl.ds(r0, ROW_CHUNK) for r0 in range(0, x_ref.shape[0], ROW_CHUNK)]
    queries = []
    for rs in chunks:
        x = x_ref[rs, :]
        inv = lax.rsqrt(jnp.mean(x * x, axis=-1, keepdims=True) + EPS)
        hb = ((x * inv) * g_ref[...]).astype(BF16)
        h_ref[rs, :] = hb
        for c0 in range(0, wp, wf):
            zp_ref[rs, c0:c0 + wf] = jnp.dot(hb, wc_ref[:, c0:c0 + wf], preferred_element_type=F32)
        zf_ref[rs, :] = jnp.dot(hb, wc_ref[:, wp:wp + wf], preferred_element_type=F32).astype(BF16)
        queries.append(jnp.dot(hb, wc_ref[:, wp + wf:wp + wf + wm], preferred_element_type=F32).astype(BF16))
    pairs = [(ci, hd) for ci in range(len(chunks)) for hd in range(n_heads)]
    scores = [lax.dot_general(queries[ci][:, hd * dh:(hd + 1) * dh], kv_ref[0, :, hd * dh:(hd + 1) * dh],
                              (((1,), (1,)), ((), ())), preferred_element_type=F32) * scale for ci, hd in pairs]
    probs = []
    for sc in scores:
        p = jnp.exp(sc - jnp.max(sc, axis=-1, keepdims=True))
        probs.append((p / jnp.sum(p, axis=-1, keepdims=True)).astype(BF16))
    for (ci, hd), p in zip(pairs, probs):
        o = jnp.dot(p, kv_ref[0, :, wm + hd * dh:wm + (hd + 1) * dh], preferred_element_type=F32)
        yc_ref[chunks[ci], hd * dh:(hd + 1) * dh] = o.astype(BF16)


def _in_proj(x, g, w_in, kv3, *, pool_width, four_width, n_heads, tm):
    m, d = x.shape
    bsz, mlen, kvw = kv3.shape
    mem_width = kvw // 2
    mix_width = pool_width + four_width + mem_width
    tps = m // bsz // tm
    row = lambda width: pl.BlockSpec((tm, width), lambda i: (i, 0))
    return pl.pallas_call(
        functools.partial(_in_proj_kernel, n_heads=n_heads),
        grid=(m // tm,),
        in_specs=[row(d), pl.BlockSpec((1, d), lambda i: (0, 0)),
                  pl.BlockSpec((d, mix_width), lambda i: (0, 0), pipeline_mode=pl.Buffered(1)),
                  pl.BlockSpec((1, mlen, kvw), lambda i: (i // tps, 0, 0))],
        out_specs=[row(d), row(pool_width), row(four_width), row(mem_width)],
        out_shape=[jax.ShapeDtypeStruct((m, d), BF16), jax.ShapeDtypeStruct((m, pool_width), F32),
                   jax.ShapeDtypeStruct((m, four_width), BF16), jax.ShapeDtypeStruct((m, mem_width), BF16)],
        scratch_shapes=[pltpu.VMEM((d, mix_width), BF16)],
        compiler_params=_cparams(1, 56),
        name="in_proj",
    )(x, g.reshape(1, d), w_in, kv3)


def _out_proj_kernel(m_ref, w_ref, r_ref, g_ref, wr_ref, x1_ref, h2_ref, afft_ref, wc_ref, *, chunk, n_experts):
    @pl.when(pl.program_id(0) == 0)
    def _():
        wc_ref[...] = w_ref[...].astype(BF16)

    wr = wr_ref[...].astype(BF16)
    for r0 in range(0, m_ref.shape[0], ROW_CHUNK):
        rs = pl.ds(r0, ROW_CHUNK)
        a = m_ref[rs, :]
        for c0 in range(0, wc_ref.shape[1], chunk):
            cols = slice(c0, c0 + chunk)
            x1_ref[rs, cols] = r_ref[rs, cols] + jnp.dot(a, wc_ref[:, cols], preferred_element_type=F32)

        x = x1_ref[rs, :]
        inv = lax.rsqrt(jnp.mean(x * x, axis=-1, keepdims=True) + EPS)
        hb = ((x * inv) * g_ref[...]).astype(BF16)
        h2_ref[rs, :] = hb
        logits = jnp.dot(hb, wr, preferred_element_type=F32)
        lane = lax.broadcasted_iota(jnp.int32, logits.shape, 1)
        logits = jnp.where(lane < n_experts, logits, -1e30)
        mx = jnp.max(logits, axis=-1, keepdims=True)
        p = jnp.exp(logits - mx)
        aff = p / jnp.sum(p, axis=-1, keepdims=True)
        afft_ref[0, :, rs] = aff.T[:n_experts, :]


def _out_proj(merged, w_out, resid, g, w_router, *, batch, tm):
    m, d = resid.shape
    e = w_router.shape[1]
    s = m // batch
    tps = s // tm
    wr = jnp.pad(w_router, ((0, 0), (0, LANES - e)))
    row = pl.BlockSpec((tm, d), lambda i: (i, 0))
    return pl.pallas_call(
        functools.partial(_out_proj_kernel, chunk=512, n_experts=e),
        grid=(m // tm,),
        in_specs=[row, pl.BlockSpec((d, d), lambda i: (0, 0), pipeline_mode=pl.Buffered(1)), row,
                  pl.BlockSpec((1, d), lambda i: (0, 0)), pl.BlockSpec((d, LANES), lambda i: (0, 0))],
        out_specs=[row, row, pl.BlockSpec((1, e, tm), lambda i: (i // tps, 0, i % tps))],
        out_shape=[jax.ShapeDtypeStruct((m, d), F32), jax.ShapeDtypeStruct((m, d), BF16),
                   jax.ShapeDtypeStruct((batch, e, s), F32)],
        scratch_shapes=[pltpu.VMEM((d, d), BF16)],
        compiler_params=_cparams(1, 56),
        name="out_proj",
    )(merged, w_out, resid, g.reshape(1, d), wr)


def _pool_kernel(u_ref, pw_ref, ps_ref, o_ref, pad_ref, *, chunk):
    s = u_ref.shape[1]
    c = pw_ref.shape[1]
    h = POOL_HALO
    zeros = jnp.zeros((h, c), F32)
    pad_ref[pl.ds(0, h), :] = zeros
    pad_ref[pl.ds(h + s, h), :] = zeros
    for g, w in enumerate(POOL_WINDOWS):
        cols = slice(g * c, (g + 1) * c)
        pad_ref[pl.ds(h, s), :] = u_ref[0, :, cols]
        wg = pw_ref[g].astype(BF16)
        scale = ps_ref[:, cols]
        for r0 in range(0, s, chunk):
            pos = r0 + lax.broadcasted_iota(jnp.int32, (chunk, 1), 0)
            lo = jnp.maximum(pos - w // 2, 0)
            hi = jnp.minimum(pos + (w - w // 2), s)
            cnt = (hi - lo).astype(F32)
            tot = pad_ref[pl.ds(h + r0 - w // 2, chunk), :]
            for k in range(1, w):
                tot = tot + pad_ref[pl.ds(h + r0 - w // 2 + k, chunk), :]
            pooled = tot / cnt - pad_ref[pl.ds(h + r0, chunk), :]
            y = jnp.dot(pooled.astype(BF16), wg, preferred_element_type=F32)
            o_ref[0, pl.ds(r0, chunk), cols] = (y * scale).astype(o_ref.dtype)


def _pool_mixer(z3, pool_w, pool_scale):
    b, s, _ = z3.shape
    g, c, _ = pool_w.shape
    width = g * c
    return pl.pallas_call(
        functools.partial(_pool_kernel, chunk=256),
        grid=(b,),
        in_specs=[pl.BlockSpec((1, s, width), lambda i: (i, 0, 0)),
                  pl.BlockSpec((g, c, c), lambda i: (0, 0, 0)),
                  pl.BlockSpec((1, width), lambda i: (0, 0))],
        out_specs=pl.BlockSpec((1, s, width), lambda i: (i, 0, 0)),
        out_shape=jax.ShapeDtypeStruct((b, s, width), BF16),
        scratch_shapes=[pltpu.VMEM((s + 2 * POOL_HALO, c), F32)],
        compiler_params=_cparams(1, 48),
        name="pool_mixer",
    )(z3, pool_w, pool_scale.reshape(1, width))


def _dft_tables(s, c):
    def tab(n):
        r = np.outer(np.arange(n), np.arange(n)) % n
        ang = r * (2.0 * np.pi / n)
        return np.cos(ang), np.sin(ang)
    cs, ss = tab(s)
    cc, sc = tab(c)
    return (jnp.asarray(np.concatenate([cs, -ss], axis=1), dtype=F32).astype(BF16),
            jnp.asarray(np.concatenate([cc, sc], axis=1), dtype=F32).astype(BF16))


def _fourier_kernel(u_ref, dft_ref, ccsc_ref, fw_ref, o_ref, t_ref, *, norm):
    s = u_ref.shape[1]
    ng, c, _ = fw_ref.shape

    @pl.when(pl.program_id(1) == 0)
    def _():
        for g in range(ng):
            ug = u_ref[0, :, g * c:(g + 1) * c]
            ab = jnp.dot(ug, ccsc_ref[...], preferred_element_type=F32)
            t_ref[pl.ds(0, s), g * c:(g + 1) * c] = ab[:, :c].astype(BF16)
            t_ref[pl.ds(s, s), g * c:(g + 1) * c] = ab[:, c:].astype(BF16)

    f = jnp.dot(dft_ref[...], t_ref[...], preferred_element_type=F32) * norm
    for g in range(ng):
        y = jnp.dot(f[:, g * c:(g + 1) * c].astype(BF16), fw_ref[g].astype(BF16), preferred_element_type=F32)
        o_ref[0, :, g * c:(g + 1) * c] = y.astype(o_ref.dtype)


def _fourier_mixer(z3, fourier_w, *, col_block, tk):
    b, s, _ = z3.shape
    ng, c, _ = fourier_w.shape
    width = ng * c
    dft, ccsc = _dft_tables(s, c)
    return pl.pallas_call(
        functools.partial(_fourier_kernel, norm=float((s * c) ** -0.5)),
        grid=(b, s // tk),
        in_specs=[pl.BlockSpec((1, s, width), lambda i, k: (i, 0, col_block)),
                  pl.BlockSpec((tk, 2 * s), lambda i, k: (k, 0)),
                  pl.BlockSpec((c, 2 * c), lambda i, k: (0, 0)),
                  pl.BlockSpec((ng, c, c), lambda i, k: (0, 0, 0))],
        out_specs=pl.BlockSpec((1, tk, width), lambda i, k: (i, k, 0)),
        out_shape=jax.ShapeDtypeStruct((b, s, width), BF16),
        scratch_shapes=[pltpu.VMEM((2 * s, width), BF16)],
        compiler_params=_cparams(2, 48),
        name="fourier_mixer",
    )(z3, dft, ccsc, fourier_w)


def _merge_kernel(ya_ref, yb_ref, yc_ref, h_ref, pp_ref, pf_ref, pm_ref, g0_ref, g1_ref, g2_ref, o_ref,
                  cpp, cpf, cpm, cg0, cg1, cg2):
    @pl.when(pl.program_id(1) == 0)
    def _():
        for src, dst in ((pp_ref, cpp), (pf_ref, cpf), (pm_ref, cpm), (g0_ref, cg0), (g1_ref, cg1), (g2_ref, cg2)):
            dst[...] = src[...].astype(BF16)

    h = h_ref[...]

    def branch(y_ref, proj, gate_w):
        gate = jax.nn.sigmoid(jnp.dot(h, gate_w[...], preferred_element_type=F32))
        return gate * jnp.dot(y_ref[...], proj[...], preferred_element_type=F32)

    acc = branch(ya_ref, cpp, cg0)
    acc = acc + branch(yb_ref, cpf, cg1)
    acc = acc + branch(yc_ref, cpm, cg2)
    o_ref[...] = acc.astype(o_ref.dtype)


def _gated_merge(ya, yb, yc, h, proj_pool, proj_fourier, proj_mem, w_in, *, gate_col0, tm, tn):
    m, d = h.shape
    gb = gate_col0 // tn
    nb = d // tn
    act = lambda width: pl.BlockSpec((tm, width), lambda n, i: (i, 0))
    wsp = lambda rows, off: pl.BlockSpec((rows, tn), lambda n, i: (0, n + off))
    return pl.pallas_call(
        _merge_kernel,
        grid=(d // tn, m // tm),
        in_specs=[act(ya.shape[1]), act(yb.shape[1]), act(yc.shape[1]), act(d),
                  wsp(proj_pool.shape[0], 0), wsp(proj_fourier.shape[0], 0), wsp(proj_mem.shape[0], 0),
                  wsp(d, gb), wsp(d, gb + nb), wsp(d, gb + 2 * nb)],
        out_specs=pl.BlockSpec((tm, tn), lambda n, i: (i, n)),
        out_shape=jax.ShapeDtypeStruct((m, d), BF16),
        scratch_shapes=[pltpu.VMEM((proj_pool.shape[0], tn), BF16), pltpu.VMEM((proj_fourier.shape[0], tn), BF16),
                        pltpu.VMEM((proj_mem.shape[0], tn), BF16), pltpu.VMEM((d, tn), BF16),
                        pltpu.VMEM((d, tn), BF16), pltpu.VMEM((d, tn), BF16)],
        compiler_params=_cparams(2, 56),
        name="gated_merge",
    )(ya, yb, yc, h, proj_pool, proj_fourier, proj_mem, w_in, w_in, w_in)


def _select_kernel(aff_ref, tri_ref, slot_t_ref, starts_ref, idx_ref, bounds_ref, slot_ref,
                   *, cap, tile, block):
    a = aff_ref[...]
    rows = a.shape[0]
    capf = float(cap)

    def count(pred):
        return jnp.sum(pred.astype(F32), axis=-1, keepdims=True)

    def body(i, t_bits):
        cand = t_bits | jnp.left_shift(jnp.int32(1), 30 - i)
        return jnp.where(count(a >= pltpu.bitcast(cand, F32)) >= capf, cand, t_bits)

    t = pltpu.bitcast(lax.fori_loop(0, 31, body, jnp.zeros((rows, 1), jnp.int32)), F32)
    gt = a > t
    eq = a == t
    need = capf - count(gt)
    tri = tri_ref[...]
    eq_rank = jnp.dot(eq.astype(BF16), tri, preferred_element_type=F32)
    sel = gt | (eq & (eq_rank < need))
    pos = jnp.dot(sel.astype(BF16), tri, preferred_element_type=F32)
    slot = jnp.where(sel, pos, -1.0)
    slot_ref[...] = slot
    slot_t_ref[...] = slot.T
    s = a.shape[1]
    lane = lax.broadcasted_iota(jnp.int32, (rows, LANES), 1)
    starts = jnp.where(lane == s // tile, capf, 0.0)
    for t in range(s // tile):
        starts = jnp.where(lane == t, pos[:, t * tile:t * tile + 1], starts)
    starts_ref[...] = starts.astype(jnp.int32)

    jcol = lax.broadcasted_iota(jnp.int32, (cap, 1), 0).astype(F32)
    tok = lax.broadcasted_iota(jnp.int32, (1, s), 1).astype(F32)
    lane_r = lax.broadcasted_iota(jnp.int32, (cap, LANES), 1)

    def invert(r, idx):
        hit = slot_ref[pl.ds(r, 1), :] == jcol
        tok_col = jnp.sum(jnp.where(hit, tok, 0.0), axis=-1, keepdims=True)
        return jnp.where(lane_r == r, tok_col, idx)

    idx = lax.fori_loop(0, rows, invert, jnp.zeros((cap, LANES), F32))
    idx_ref[...] = idx
    nblk = cap // block
    sub = lax.broadcasted_iota(jnp.int32, (2 * nblk, LANES), 0)
    bounds = jnp.zeros((2 * nblk, LANES), F32)
    for i in range(nblk):
        bounds = jnp.where(sub == i, idx[i * block:i * block + 1, :], bounds)
        bounds = jnp.where(sub == nblk + i, idx[(i + 1) * block - 1:(i + 1) * block, :] + 1.0, bounds)
    bounds_ref[...] = bounds.astype(jnp.int32)


def _select(aff_rows, *, cap, tile, block):
    rows, s = aff_rows.shape
    assert rows == LANES
    idx = jnp.arange(s, dtype=jnp.int32)
    tri = (idx[:, None] < idx[None, :]).astype(BF16)
    full = lambda shape: pl.BlockSpec(shape, lambda i: (0, 0))
    nb2 = 2 * cap // block
    return pl.pallas_call(
        functools.partial(_select_kernel, cap=cap, tile=tile, block=block),
        grid=(1,),
        in_specs=[full((rows, s)), full((s, s))],
        out_specs=[full((s, rows)), full((rows, LANES)), full((cap, rows)), full((nb2, rows))],
        out_shape=[jax.ShapeDtypeStruct((s, rows), F32), jax.ShapeDtypeStruct((rows, LANES), jnp.int32),
                   jax.ShapeDtypeStruct((cap, rows), F32), jax.ShapeDtypeStruct((nb2, rows), jnp.int32)],
        scratch_shapes=[pltpu.VMEM((rows, s), F32)],
        compiler_params=_cparams(1, 48),
        name="expert_select",
    )(aff_rows, tri)


def _dispatch_kernel(bounds_ref, idx_ref, aff_ref, h_ref, o_ref, oa_ref, *, window, group):
    bi = pl.program_id(0)
    ji = pl.program_id(1)
    nblk = pl.num_programs(1)
    ne, block, d = o_ref.shape
    s = h_ref.shape[1]

    lo = hi = None
    for k in range(ne):
        l = bounds_ref[ji, bi * ne + k]
        h = bounds_ref[nblk + ji, bi * ne + k]
        lo = l if lo is None else jnp.minimum(lo, l)
        hi = h if hi is None else jnp.maximum(hi, h)
    w0 = jnp.minimum(jnp.bitwise_and(lo, -BF16_ROWS), s - window)
    fits = hi - w0 <= window

    rows = idx_ref[pl.ds(pl.multiple_of(ji * block, block), block), :]
    lane = lax.broadcasted_iota(jnp.int32, (block, LANES), 1)
    tok_all = lax.broadcasted_iota(jnp.int32, (1, s), 1).astype(F32)

    def gather(width, first_tok, src):
        t = lax.broadcasted_iota(jnp.int32, (group * block, width), 1).astype(F32) + first_tok
        for g0 in range(0, ne, group):
            cols = [jnp.sum(jnp.where(lane == bi * ne + k, rows, 0.0), axis=-1, keepdims=True)
                    for k in range(g0, g0 + group)]
            toks = jnp.concatenate(cols, axis=0)
            out = jnp.dot((toks == t).astype(BF16), src, preferred_element_type=F32)
            o_ref[g0:g0 + group] = out.reshape(group, block, d).astype(o_ref.dtype)
            for k, col in zip(range(g0, g0 + group), cols):
                oa_ref[k] = jnp.sum(jnp.where(col == tok_all, aff_ref[0, k:k + 1, :], 0.0), axis=-1, keepdims=True)

    @pl.when(fits)
    def _():
        gather(window, w0.astype(F32), h_ref[0, pl.ds(pl.multiple_of(w0, BF16_ROWS), window), :])

    @pl.when(jnp.logical_not(fits))
    def _():
        gather(s, 0.0, h_ref[0])


def _dispatch(bounds, idx, aff_t, h3, *, n_experts, cap, block, window):
    b, s, d = h3.shape
    assert window % BF16_ROWS == 0 and window <= s and cap % block == 0 and block % BF16_ROWS == 0
    nblk = cap // block
    grid_spec = pltpu.PrefetchScalarGridSpec(
        num_scalar_prefetch=1,
        grid=(b, nblk),
        in_specs=[pl.BlockSpec((cap, LANES), lambda i, j, bd: (0, 0)),
                  pl.BlockSpec((1, n_experts, s), lambda i, j, bd: (i, 0, 0)),
                  pl.BlockSpec((1, s, d), lambda i, j, bd: (i, 0, 0))],
        out_specs=[pl.BlockSpec((n_experts, block, d), lambda i, j, bd: (0, i * nblk + j, 0)),
                   pl.BlockSpec((n_experts, block, 1), lambda i, j, bd: (0, i * nblk + j, 0))],
    )
    return pl.pallas_call(
        functools.partial(_dispatch_kernel, window=window, group=4),
        grid_spec=grid_spec,
        out_shape=[jax.ShapeDtypeStruct((n_experts, b * cap, d), BF16),
                   jax.ShapeDtypeStruct((n_experts, b * cap, 1), F32)],
        compiler_params=_cparams(2, 48),
        name="dispatch",
    )(bounds, idx, aff_t, h3)


def _combine_kernel(starts_ref, slot_ref, y_ref, x_ref, g_ref, o_ref, *, win, final_norm):
    bi = pl.program_id(0)
    ti = pl.program_id(1)
    tt, d = x_ref.shape
    ne, cap, _ = y_ref.shape

    wins = []
    fits = None
    for k in range(ne):
        lo = starts_ref[bi * ne + k, ti]
        hi = starts_ref[bi * ne + k, ti + 1]
        w0 = jnp.minimum(jnp.bitwise_and(lo, -BF16_ROWS), cap - win)
        ok = hi - w0 <= win
        fits = ok if fits is None else jnp.logical_and(fits, ok)
        wins.append(w0)

    slots = slot_ref[...].astype(BF16)

    def onehot(width, first_slot):
        shift = width.bit_length() - 1
        n = ne * width
        lane_of = bi * ne + lax.shift_right_logical(lax.broadcasted_iota(jnp.int32, (LANES, n), 1), shift)
        spread = (lax.broadcasted_iota(jnp.int32, (LANES, n), 0) == lane_of).astype(BF16)
        slot_b = jnp.dot(slots, spread, preferred_element_type=F32)
        c = lax.broadcasted_iota(jnp.int32, (1, n), 1)
        target = jnp.bitwise_and(c, width - 1)
        if first_slot is not None:
            kk = lax.shift_right_logical(c, shift)
            for k in range(ne):
                target = target + jnp.where(kk == k, first_slot[k], 0)
        return (slot_b == target.astype(F32)).astype(BF16)

    def finish(contrib):
        x = x_ref[...] + contrib
        if final_norm:
            inv = lax.rsqrt(jnp.mean(x * x, axis=-1, keepdims=True) + EPS)
            x = (x * inv) * g_ref[...]
        o_ref[...] = x

    @pl.when(fits)
    def _():
        ywin = jnp.concatenate([y_ref[k, pl.ds(pl.multiple_of(wins[k], BF16_ROWS), win), :] for k in range(ne)],
                               axis=0)
        finish(jnp.dot(onehot(win, wins), ywin, preferred_element_type=F32))

    @pl.when(jnp.logical_not(fits))
    def _():
        finish(jnp.dot(onehot(cap, None), y_ref[...].reshape(ne * cap, d), preferred_element_type=F32))


def _combine(starts, slot_cols, y, x1, g, *, batch, cap, tt, win, final_norm):
    m, d = x1.shape
    ne = y.shape[0]
    s = m // batch
    tps = s // tt
    assert slot_cols.shape == (s, LANES) and batch * ne == LANES
    assert win % BF16_ROWS == 0 and win <= cap and win & (win - 1) == 0 and cap & (cap - 1) == 0
    grid_spec = pltpu.PrefetchScalarGridSpec(
        num_scalar_prefetch=1,
        grid=(batch, tps),
        in_specs=[pl.BlockSpec((tt, LANES), lambda b, t, st: (t, 0)),
                  pl.BlockSpec((ne, cap, d), lambda b, t, st: (0, b, 0)),
                  pl.BlockSpec((tt, d), lambda b, t, st: (b * tps + t, 0)),
                  pl.BlockSpec((1, d), lambda b, t, st: (0, 0))],
        out_specs=pl.BlockSpec((tt, d), lambda b, t, st: (b * tps + t, 0)),
    )
    return pl.pallas_call(
        functools.partial(_combine_kernel, win=win, final_norm=final_norm),
        grid_spec=grid_spec,
        out_shape=jax.ShapeDtypeStruct((m, d), F32),
        compiler_params=_cparams(2, 56),
        name="combine",
    )(starts, slot_cols, y, x1, g.reshape(1, d))


def _expert_up_kernel(x_ref, wg_ref, wu_ref, o_ref, *, chunk):
    wg = wg_ref[0].astype(BF16)
    wu = wu_ref[0].astype(BF16)
    for r0 in range(0, x_ref.shape[1], chunk):
        x = x_ref[0, pl.ds(r0, chunk), :]
        gate = jnp.dot(x, wg, preferred_element_type=F32)
        up = jnp.dot(x, wu, preferred_element_type=F32)
        o_ref[0, pl.ds(r0, chunk), :] = (jax.nn.silu(gate) * up).astype(o_ref.dtype)


def _expert_up(xin, w_gate, w_up, *, tf):
    e, m, d = xin.shape
    f = w_gate.shape[2]
    return pl.pallas_call(
        functools.partial(_expert_up_kernel, chunk=512),
        grid=(e, f // tf),
        in_specs=[pl.BlockSpec((1, m, d), lambda i, j: (i, 0, 0)),
                  pl.BlockSpec((1, d, tf), lambda i, j: (i, 0, j)),
                  pl.BlockSpec((1, d, tf), lambda i, j: (i, 0, j))],
        out_specs=pl.BlockSpec((1, m, tf), lambda i, j: (i, 0, j)),
        out_shape=jax.ShapeDtypeStruct((e, m, f), BF16),
        compiler_params=_cparams(2, 56),
        name="expert_up",
    )(xin, w_gate, w_up)


def _expert_down_kernel(h_ref, wd_ref, a_ref, o_ref, *, chunk):
    wd = wd_ref[0].astype(BF16)
    for r0 in range(0, h_ref.shape[1], chunk):
        y = jnp.dot(h_ref[0, pl.ds(r0, chunk), :], wd, preferred_element_type=F32)
        o_ref[0, pl.ds(r0, chunk), :] = (y * a_ref[0, pl.ds(r0, chunk), :]).astype(o_ref.dtype)


def _expert_down(hidden, w_down, aff, *, tn):
    e, m, f = hidden.shape
    d = w_down.shape[2]
    return pl.pallas_call(
        functools.partial(_expert_down_kernel, chunk=512),
        grid=(e, d // tn),
        in_specs=[pl.BlockSpec((1, m, f), lambda i, j: (i, 0, 0)),
                  pl.BlockSpec((1, f, tn), lambda i, j: (i, 0, j)),
                  pl.BlockSpec((1, m, 1), lambda i, j: (i, 0, 0))],
        out_specs=pl.BlockSpec((1, m, tn), lambda i, j: (i, 0, j)),
        out_shape=jax.ShapeDtypeStruct((e, m, d), BF16),
        compiler_params=_cparams(2, 56),
        name="expert_down",
    )(hidden, w_down, aff)


def kernel(x, mem, norm_mix_g, norm_mem_g, w_in, pool_w, pool_scale, fourier_w, w_kv_mem, proj_pool, proj_fourier,
           proj_mem, w_out, norm_ffn_g, w_router, w_expert_gate, w_expert_up, w_expert_down, norm_final_g):
    b, s, d = x.shape
    depth = w_in.shape[0]
    n_tok = b * s
    pool_width = pool_w.shape[1] * pool_w.shape[2]
    four_width = fourier_w.shape[1] * fourier_w.shape[2]
    mem_width = proj_mem.shape[1]
    mix_width = pool_width + four_width + mem_width
    n_heads = 4
    e = w_router.shape[2]
    cap = CAPACITY_FACTOR * s // e

    xf = x.reshape(n_tok, d)
    for l in range(depth):
        memn = _rmsnorm(mem.reshape(-1, d), norm_mem_g[l], tm=512, out_dtype=BF16)
        kv = _matmul(memn, w_kv_mem[l], n_cols=2 * mem_width, tm=1024, tn=512, out_dtype=BF16, name="kv_proj")
        h, zp, zf, yc = _in_proj(xf, norm_mix_g[l], w_in[l], kv.reshape(b, -1, 2 * mem_width),
                                 pool_width=pool_width, four_width=four_width, n_heads=n_heads, tm=512)
        ya = _pool_mixer(zp.reshape(b, s, pool_width), pool_w[l], pool_scale[l])
        yb = _fourier_mixer(zf.reshape(b, s, four_width), fourier_w[l], col_block=0, tk=512)
        merged = _gated_merge(ya.reshape(n_tok, -1), yb.reshape(n_tok, -1), yc, h,
                              proj_pool[l], proj_fourier[l], proj_mem[l], w_in[l],
                              gate_col0=mix_width, tm=1024, tn=256)

        x1, h2, aff_t = _out_proj(merged, w_out[l], xf, norm_ffn_g[l], w_router[l], batch=b, tm=512)
        slot_cols, starts, idx, bounds = _select(aff_t.reshape(b * e, s), cap=cap, tile=COMBINE_TILE,
                                                 block=DISPATCH_BLOCK)
        xin, aff_slot = _dispatch(bounds, idx, aff_t, h2.reshape(b, s, d), n_experts=e, cap=cap,
                                  block=DISPATCH_BLOCK, window=DISPATCH_WINDOW)
        hidden = _expert_up(xin, w_expert_gate[l], w_expert_up[l], tf=512)
        y = _expert_down(hidden, w_expert_down[l], aff_slot, tn=1024)
        last = l + 1 == depth
        xf = _combine(starts[:, :s // COMBINE_TILE + 1], slot_cols, y, x1, norm_final_g, batch=b, cap=cap,
                      tt=COMBINE_TILE, win=COMBINE_WINDOW, final_norm=last)
    return xf.reshape(b, s, d)
```

```python
import functools

import numpy as np
import jax
import jax.numpy as jnp
from jax import lax
from jax.experimental import pallas as pl
from jax.experimental.pallas import tpu as pltpu

F32 = jnp.float32
BF16 = jnp.bfloat16

EPS = 1e-6
POOL_WINDOWS = (2, 4, 8, 16)
N_EXPERTS = 16
CAPACITY_FACTOR = 2
LANES = 128
BF16_ROWS = 16
POOL_HALO = 32
ROW_CHUNK = 256
DISPATCH_BLOCK = 64
DISPATCH_WINDOW = 768
COMBINE_TILE = 256
COMBINE_WINDOW = 64
MIB = 1024 * 1024


def _cparams(n_axes, vmem_mib):
    return pltpu.CompilerParams(
        dimension_semantics=("arbitrary",) * n_axes,
        vmem_limit_bytes=vmem_mib * MIB,
    )


def _rmsnorm_kernel(x_ref, g_ref, o_ref):
    x = x_ref[...]
    inv = lax.rsqrt(jnp.mean(x * x, axis=-1, keepdims=True) + EPS)
    o_ref[...] = ((x * inv) * g_ref[...]).astype(o_ref.dtype)


def _rmsnorm(x, g, *, tm, out_dtype):
    m, d = x.shape
    return pl.pallas_call(
        _rmsnorm_kernel,
        grid=(m // tm,),
        in_specs=[pl.BlockSpec((tm, d), lambda i: (i, 0)), pl.BlockSpec((1, d), lambda i: (0, 0))],
        out_specs=pl.BlockSpec((tm, d), lambda i: (i, 0)),
        out_shape=jax.ShapeDtypeStruct((m, d), out_dtype),
        compiler_params=_cparams(1, 40),
        name="rmsnorm",
    )(x, g.reshape(1, d))


def _mm_kernel(a_ref, w_ref, o_ref, wc_ref):
    @pl.when(pl.program_id(1) == 0)
    def _():
        wc_ref[...] = w_ref[...].astype(BF16)

    o_ref[...] = jnp.dot(a_ref[...], wc_ref[...], preferred_element_type=F32).astype(o_ref.dtype)


def _matmul(a, w, *, n_cols, tm, tn, out_dtype, name):
    m, k = a.shape
    return pl.pallas_call(
        _mm_kernel,
        grid=(n_cols // tn, m // tm),
        in_specs=[pl.BlockSpec((tm, k), lambda n, i: (i, 0)), pl.BlockSpec((k, tn), lambda n, i: (0, n))],
        out_specs=pl.BlockSpec((tm, tn), lambda n, i: (i, n)),
        out_shape=jax.ShapeDtypeStruct((m, n_cols), out_dtype),
        scratch_shapes=[pltpu.VMEM((k, tn), BF16)],
        compiler_params=_cparams(2, 48),
        name=name,
    )(a, w)


def _in_proj_kernel(x_ref, g_ref, w_ref, kv_ref, h_ref, zp_ref, zf_ref, yc_ref, wc_ref, *, n_heads):
    @pl.when(pl.program_id(0) == 0)
    def _():
        wc_ref[...] = w_ref[...].astype(BF16)

    wp = zp_ref.shape[1]
    wf = zf_ref.shape[1]
    wm = yc_ref.shape[1]
    dh = wm // n_heads
    scale = float(dh ** -0.5)
    chunks = [pl.ds(r0, ROW_CHUNK) for r0 in range(0, x_ref.shape[0], ROW_CHUNK)]
    queries = []
    for rs in chunks:
        x = x_ref[rs, :]
        inv = lax.rsqrt(jnp.mean(x * x, axis=-1, keepdims=True) + EPS)
        hb = ((x * inv) * g_ref[...]).astype(BF16)
        h_ref[rs, :] = hb
        for c0 in range(0, wp, wf):
            zp_ref[rs, c0:c0 + wf] = jnp.dot(hb, wc_ref[:, c0:c0 + wf], preferred_element_type=F32)
        zf_ref[rs, :] = jnp.dot(hb, wc_ref[:, wp:wp + wf], preferred_element_type=F32).astype(BF16)
        queries.append(jnp.dot(hb, wc_ref[:, wp + wf:wp + wf + wm], preferred_element_type=F32).astype(BF16))
    pairs = [(ci, hd) for ci in range(len(chunks)) for hd in range(n_heads)]
    scores = [lax.dot_general(queries[ci][:, hd * dh:(hd + 1) * dh], kv_ref[0, :, hd * dh:(hd + 1) * dh],
                              (((1,), (1,)), ((), ())), preferred_element_type=F32) * scale for ci, hd in pairs]
    probs = []
    for sc in scores:
        p = jnp.exp(sc - jnp.max(sc, axis=-1, keepdims=True))
        probs.append((p / jnp.sum(p, axis=-1, keepdims=True)).astype(BF16))
    for (ci, hd), p in zip(pairs, probs):
        o = jnp.dot(p, kv_ref[0, :, wm + hd * dh:wm + (hd + 1) * dh], preferred_element_type=F32)
        yc_ref[chunks[ci], hd * dh:(hd + 1) * dh] = o.astype(BF16)


def _in_proj(x, g, w_in, kv3, *, pool_width, four_width, n_heads, tm):
    m, d = x.shape
    bsz, mlen, kvw = kv3.shape
    mem_width = kvw // 2
    mix_width = pool_width + four_width + mem_width
    tps = m // bsz // tm
    row = lambda width: pl.BlockSpec((tm, width), lambda i: (i, 0))
    return pl.pallas_call(
        functools.partial(_in_proj_kernel, n_heads=n_heads),
        grid=(m // tm,),
        in_specs=[row(d), pl.BlockSpec((1, d), lambda i: (0, 0)),
                  pl.BlockSpec((d, mix_width), lambda i: (0, 0), pipeline_mode=pl.Buffered(1)),
                  pl.BlockSpec((1, mlen, kvw), lambda i: (i // tps, 0, 0))],
        out_specs=[row(d), row(pool_width), row(four_width), row(mem_width)],
        out_shape=[jax.ShapeDtypeStruct((m, d), BF16), jax.ShapeDtypeStruct((m, pool_width), F32),
                   jax.ShapeDtypeStruct((m, four_width), BF16), jax.ShapeDtypeStruct((m, mem_width), BF16)],
        scratch_shapes=[pltpu.VMEM((d, mix_width), BF16)],
        compiler_params=_cparams(1, 56),
        name="in_proj",
    )(x, g.reshape(1, d), w_in, kv3)


def _out_proj_kernel(m_ref, w_ref, r_ref, g_ref, wr_ref, x1_ref, h2_ref, afft_ref, wc_ref, *, chunk, n_experts):
    @pl.when(pl.program_id(0) == 0)
    def _():
        wc_ref[...] = w_ref[...].astype(BF16)

    wr = wr_ref[...].astype(BF16)
    for r0 in range(0, m_ref.shape[0], ROW_CHUNK):
        rs = pl.ds(r0, ROW_CHUNK)
        a = m_ref[rs, :]
        for c0 in range(0, wc_ref.shape[1], chunk):
            cols = slice(c0, c0 + chunk)
            x1_ref[rs, cols] = r_ref[rs, cols] + jnp.dot(a, wc_ref[:, cols], preferred_element_type=F32)

        x = x1_ref[rs, :]
        inv = lax.rsqrt(jnp.mean(x * x, axis=-1, keepdims=True) + EPS)
        hb = ((x * inv) * g_ref[...]).astype(BF16)
        h2_ref[rs, :] = hb
        logits = jnp.dot(hb, wr, preferred_element_type=F32)
        lane = lax.broadcasted_iota(jnp.int32, logits.shape, 1)
        logits = jnp.where(lane < n_experts, logits, -1e30)
        mx = jnp.max(logits, axis=-1, keepdims=True)
        p = jnp.exp(logits - mx)
        aff = p / jnp.sum(p, axis=-1, keepdims=True)
        afft_ref[0, :, rs] = aff.T[:n_experts, :]


def _out_proj(merged, w_out, resid, g, w_router, *, batch, tm):
    m, d = resid.shape
    e = w_router.shape[1]
    s = m // batch
    tps = s // tm
    wr = jnp.pad(w_router, ((0, 0), (0, LANES - e)))
    row = pl.BlockSpec((tm, d), lambda i: (i, 0))
    return pl.pallas_call(
        functools.partial(_out_proj_kernel, chunk=512, n_experts=e),
        grid=(m // tm,),
        in_specs=[row, pl.BlockSpec((d, d), lambda i: (0, 0), pipeline_mode=pl.Buffered(1)), row,
                  pl.BlockSpec((1, d), lambda i: (0, 0)), pl.BlockSpec((d, LANES), lambda i: (0, 0))],
        out_specs=[row, row, pl.BlockSpec((1, e, tm), lambda i: (i // tps, 0, i % tps))],
        out_shape=[jax.ShapeDtypeStruct((m, d), F32), jax.ShapeDtypeStruct((m, d), BF16),
                   jax.ShapeDtypeStruct((batch, e, s), F32)],
        scratch_shapes=[pltpu.VMEM((d, d), BF16)],
        compiler_params=_cparams(1, 56),
        name="out_proj",
    )(merged, w_out, resid, g.reshape(1, d), wr)


def _pool_kernel(u_ref, pw_ref, ps_ref, o_ref, a_ref, b_ref, *, chunk):
    s = u_ref.shape[1]
    c = pw_ref.shape[1]
    h = POOL_HALO
    rows = s + 2 * h
    zeros = jnp.zeros((h, c), F32)

    def level(src, dst, off_lo, off_hi, margin):
        for r0 in range(margin, rows - margin, chunk):
            n = min(chunk, rows - margin - r0)
            dst[pl.ds(r0, n), :] = src[pl.ds(r0 + off_lo, n), :] + src[pl.ds(r0 + off_hi, n), :]

    for g, w in enumerate(POOL_WINDOWS):
        cols = slice(g * c, (g + 1) * c)
        a_ref[pl.ds(0, h), :] = zeros
        a_ref[pl.ds(h + s, h), :] = zeros
        a_ref[pl.ds(h, s), :] = u_ref[0, :, cols]
        src, off_lo, off_hi = a_ref, -1, 0
        if w >= 4:
            level(a_ref, b_ref, -1, 0, 8)
            src, off_lo, off_hi = b_ref, -1, 1
        if w >= 8:
            level(b_ref, a_ref, -1, 1, 16)
            src, off_lo, off_hi = a_ref, -2, 2
        if w >= 16:
            level(a_ref, b_ref, -2, 2, 24)
            src, off_lo, off_hi = b_ref, -4, 4
        wg = pw_ref[g].astype(BF16)
        scale = ps_ref[:, cols]
        for r0 in range(0, s, chunk):
            pos = r0 + lax.broadcasted_iota(jnp.int32, (chunk, 1), 0)
            lo = jnp.maximum(pos - w // 2, 0)
            hi = jnp.minimum(pos + (w - w // 2), s)
            cnt = (hi - lo).astype(F32)
            tot = src[pl.ds(h + r0 + off_lo, chunk), :] + src[pl.ds(h + r0 + off_hi, chunk), :]
            pooled = tot / cnt - u_ref[0, pl.ds(r0, chunk), cols]
            y = jnp.dot(pooled.astype(BF16), wg, preferred_element_type=F32)
            o_ref[0, pl.ds(r0, chunk), cols] = (y * scale).astype(o_ref.dtype)


def _pool_mixer(z3, pool_w, pool_scale):
    b, s, _ = z3.shape
    g, c, _ = pool_w.shape
    width = g * c
    assert POOL_WINDOWS == (2, 4, 8, 16) and g == len(POOL_WINDOWS)
    return pl.pallas_call(
        functools.partial(_pool_kernel, chunk=256),
        grid=(b,),
        in_specs=[pl.BlockSpec((1, s, width), lambda i: (i, 0, 0)),
                  pl.BlockSpec((g, c, c), lambda i: (0, 0, 0)),
                  pl.BlockSpec((1, width), lambda i: (0, 0))],
        out_specs=pl.BlockSpec((1, s, width), lambda i: (i, 0, 0)),
        out_shape=jax.ShapeDtypeStruct((b, s, width), BF16),
        scratch_shapes=[pltpu.VMEM((s + 2 * POOL_HALO, c), F32), pltpu.VMEM((s + 2 * POOL_HALO, c), F32)],
        compiler_params=_cparams(1, 48),
        name="pool_mixer",
    )(z3, pool_w, pool_scale.reshape(1, width))


def _dft_tables(s, c):
    def tab(n):
        r = np.outer(np.arange(n), np.arange(n)) % n
        ang = r * (2.0 * np.pi / n)
        return np.cos(ang), np.sin(ang)
    cs, ss = tab(s)
    cc, sc = tab(c)
    return (jnp.asarray(np.concatenate([cs, -ss], axis=1), dtype=F32).astype(BF16),
            jnp.asarray(np.concatenate([cc, sc], axis=1), dtype=F32).astype(BF16))


def _fourier_kernel(u_ref, dft_ref, ccsc_ref, fw_ref, o_ref, t_ref, *, norm):
    s = u_ref.shape[1]
    ng, c, _ = fw_ref.shape

    @pl.when(pl.program_id(1) == 0)
    def _():
        for g in range(ng):
            ug = u_ref[0, :, g * c:(g + 1) * c]
            ab = jnp.dot(ug, ccsc_ref[...], preferred_element_type=F32)
            t_ref[pl.ds(0, s), g * c:(g + 1) * c] = ab[:, :c].astype(BF16)
            t_ref[pl.ds(s, s), g * c:(g + 1) * c] = ab[:, c:].astype(BF16)

    f = jnp.dot(dft_ref[...], t_ref[...], preferred_element_type=F32) * norm
    for g in range(ng):
        y = jnp.dot(f[:, g * c:(g + 1) * c].astype(BF16), fw_ref[g].astype(BF16), preferred_element_type=F32)
        o_ref[0, :, g * c:(g + 1) * c] = y.astype(o_ref.dtype)


def _fourier_mixer(z3, fourier_w, *, col_block, tk):
    b, s, _ = z3.shape
    ng, c, _ = fourier_w.shape
    width = ng * c
    dft, ccsc = _dft_tables(s, c)
    return pl.pallas_call(
        functools.partial(_fourier_kernel, norm=float((s * c) ** -0.5)),
        grid=(b, s // tk),
        in_specs=[pl.BlockSpec((1, s, width), lambda i, k: (i, 0, col_block)),
                  pl.BlockSpec((tk, 2 * s), lambda i, k: (k, 0)),
                  pl.BlockSpec((c, 2 * c), lambda i, k: (0, 0)),
                  pl.BlockSpec((ng, c, c), lambda i, k: (0, 0, 0))],
        out_specs=pl.BlockSpec((1, tk, width), lambda i, k: (i, k, 0)),
        out_shape=jax.ShapeDtypeStruct((b, s, width), BF16),
        scratch_shapes=[pltpu.VMEM((2 * s, width), BF16)],
        compiler_params=_cparams(2, 48),
        name="fourier_mixer",
    )(z3, dft, ccsc, fourier_w)


def _merge_kernel(ya_ref, yb_ref, yc_ref, h_ref, pp_ref, pf_ref, pm_ref, g0_ref, g1_ref, g2_ref, o_ref,
                  cpp, cpf, cpm, cg0, cg1, cg2):
    @pl.when(pl.program_id(1) == 0)
    def _():
        for src, dst in ((pp_ref, cpp), (pf_ref, cpf), (pm_ref, cpm), (g0_ref, cg0), (g1_ref, cg1), (g2_ref, cg2)):
            dst[...] = src[...].astype(BF16)

    h = h_ref[...]

    def branch(y_ref, proj, gate_w):
        gate = jax.nn.sigmoid(jnp.dot(h, gate_w[...], preferred_element_type=F32))
        return gate * jnp.dot(y_ref[...], proj[...], preferred_element_type=F32)

    acc = branch(ya_ref, cpp, cg0)
    acc = acc + branch(yb_ref, cpf, cg1)
    acc = acc + branch(yc_ref, cpm, cg2)
    o_ref[...] = acc.astype(o_ref.dtype)


def _gated_merge(ya, yb, yc, h, proj_pool, proj_fourier, proj_mem, w_in, *, gate_col0, tm, tn):
    m, d = h.shape
    gb = gate_col0 // tn
    nb = d // tn
    act = lambda width: pl.BlockSpec((tm, width), lambda n, i: (i, 0))
    wsp = lambda rows, off: pl.BlockSpec((rows, tn), lambda n, i: (0, n + off))
    return pl.pallas_call(
        _merge_kernel,
        grid=(d // tn, m // tm),
        in_specs=[act(ya.shape[1]), act(yb.shape[1]), act(yc.shape[1]), act(d),
                  wsp(proj_pool.shape[0], 0), wsp(proj_fourier.shape[0], 0), wsp(proj_mem.shape[0], 0),
                  wsp(d, gb), wsp(d, gb + nb), wsp(d, gb + 2 * nb)],
        out_specs=pl.BlockSpec((tm, tn), lambda n, i: (i, n)),
        out_shape=jax.ShapeDtypeStruct((m, d), BF16),
        scratch_shapes=[pltpu.VMEM((proj_pool.shape[0], tn), BF16), pltpu.VMEM((proj_fourier.shape[0], tn), BF16),
                        pltpu.VMEM((proj_mem.shape[0], tn), BF16), pltpu.VMEM((d, tn), BF16),
                        pltpu.VMEM((d, tn), BF16), pltpu.VMEM((d, tn), BF16)],
        compiler_params=_cparams(2, 56),
        name="gated_merge",
    )(ya, yb, yc, h, proj_pool, proj_fourier, proj_mem, w_in, w_in, w_in)


def _select_kernel(aff_ref, tri_ref, slot_t_ref, starts_ref, idx_ref, bounds_ref, slot_ref,
                   *, cap, tile, block):
    a = aff_ref[...]
    rows = a.shape[0]
    capf = float(cap)

    def count(pred):
        return jnp.sum(pred.astype(F32), axis=-1, keepdims=True)

    def body(i, t_bits):
        cand = t_bits | jnp.left_shift(jnp.int32(1), 30 - i)
        return jnp.where(count(a >= pltpu.bitcast(cand, F32)) >= capf, cand, t_bits)

    t = pltpu.bitcast(lax.fori_loop(0, 31, body, jnp.zeros((rows, 1), jnp.int32)), F32)
    gt = a > t
    eq = a == t
    need = capf - count(gt)
    tri = tri_ref[...]
    eq_rank = jnp.dot(eq.astype(BF16), tri, preferred_element_type=F32)
    sel = gt | (eq & (eq_rank < need))
    pos = jnp.dot(sel.astype(BF16), tri, preferred_element_type=F32)
    slot = jnp.where(sel, pos, -1.0)
    slot_ref[...] = slot
    slot_t_ref[...] = slot.T
    s = a.shape[1]
    lane = lax.broadcasted_iota(jnp.int32, (rows, LANES), 1)
    starts = jnp.where(lane == s // tile, capf, 0.0)
    for t in range(s // tile):
        starts = jnp.where(lane == t, pos[:, t * tile:t * tile + 1], starts)
    starts_ref[...] = starts.astype(jnp.int32)

    jcol = lax.broadcasted_iota(jnp.int32, (cap, 1), 0).astype(F32)
    tok = lax.broadcasted_iota(jnp.int32, (1, s), 1).astype(F32)
    lane_r = lax.broadcasted_iota(jnp.int32, (cap, LANES), 1)

    def invert(r, idx):
        hit = slot_ref[pl.ds(r, 1), :] == jcol
        tok_col = jnp.sum(jnp.where(hit, tok, 0.0), axis=-1, keepdims=True)
        return jnp.where(lane_r == r, tok_col, idx)

    idx = lax.fori_loop(0, rows, invert, jnp.zeros((cap, LANES), F32))
    idx_ref[...] = idx
    nblk = cap // block
    sub = lax.broadcasted_iota(jnp.int32, (2 * nblk, LANES), 0)
    bounds = jnp.zeros((2 * nblk, LANES), F32)
    for i in range(nblk):
        bounds = jnp.where(sub == i, idx[i * block:i * block + 1, :], bounds)
        bounds = jnp.where(sub == nblk + i, idx[(i + 1) * block - 1:(i + 1) * block, :] + 1.0, bounds)
    bounds_ref[...] = bounds.astype(jnp.int32)


def _select(aff_rows, *, cap, tile, block):
    rows, s = aff_rows.shape
    assert rows == LANES
    idx = jnp.arange(s, dtype=jnp.int32)
    tri = (idx[:, None] < idx[None, :]).astype(BF16)
    full = lambda shape: pl.BlockSpec(shape, lambda i: (0, 0))
    nb2 = 2 * cap // block
    return pl.pallas_call(
        functools.partial(_select_kernel, cap=cap, tile=tile, block=block),
        grid=(1,),
        in_specs=[full((rows, s)), full((s, s))],
        out_specs=[full((s, rows)), full((rows, LANES)), full((cap, rows)), full((nb2, rows))],
        out_shape=[jax.ShapeDtypeStruct((s, rows), F32), jax.ShapeDtypeStruct((rows, LANES), jnp.int32),
                   jax.ShapeDtypeStruct((cap, rows), F32), jax.ShapeDtypeStruct((nb2, rows), jnp.int32)],
        scratch_shapes=[pltpu.VMEM((rows, s), F32)],
        compiler_params=_cparams(1, 48),
        name="expert_select",
    )(aff_rows, tri)


def _dispatch_kernel(bounds_ref, idx_ref, aff_ref, h_ref, o_ref, oa_ref, *, window, group):
    bi = pl.program_id(0)
    ji = pl.program_id(1)
    nblk = pl.num_programs(1)
    ne, block, d = o_ref.shape
    s = h_ref.shape[1]

    lo = hi = None
    for k in range(ne):
        l = bounds_ref[ji, bi * ne + k]
        h = bounds_ref[nblk + ji, bi * ne + k]
        lo = l if lo is None else jnp.minimum(lo, l)
        hi = h if hi is None else jnp.maximum(hi, h)
    w0 = jnp.minimum(jnp.bitwise_and(lo, -BF16_ROWS), s - window)
    fits = hi - w0 <= window

    rows = idx_ref[pl.ds(pl.multiple_of(ji * block, block), block), :]
    lane = lax.broadcasted_iota(jnp.int32, (block, LANES), 1)
    tok_all = lax.broadcasted_iota(jnp.int32, (1, s), 1).astype(F32)

    def gather(width, first_tok, src):
        t = lax.broadcasted_iota(jnp.int32, (group * block, width), 1).astype(F32) + first_tok
        for g0 in range(0, ne, group):
            cols = [jnp.sum(jnp.where(lane == bi * ne + k, rows, 0.0), axis=-1, keepdims=True)
                    for k in range(g0, g0 + group)]
            toks = jnp.concatenate(cols, axis=0)
            out = jnp.dot((toks == t).astype(BF16), src, preferred_element_type=F32)
            o_ref[g0:g0 + group] = out.reshape(group, block, d).astype(o_ref.dtype)
            for k, col in zip(range(g0, g0 + group), cols):
                oa_ref[k] = jnp.sum(jnp.where(col == tok_all, aff_ref[0, k:k + 1, :], 0.0), axis=-1, keepdims=True)

    @pl.when(fits)
    def _():
        gather(window, w0.astype(F32), h_ref[0, pl.ds(pl.multiple_of(w0, BF16_ROWS), window), :])

    @pl.when(jnp.logical_not(fits))
    def _():
        gather(s, 0.0, h_ref[0])


def _dispatch(bounds, idx, aff_t, h3, *, n_experts, cap, block, window):
    b, s, d = h3.shape
    assert window % BF16_ROWS == 0 and window <= s and cap % block == 0 and block % BF16_ROWS == 0
    nblk = cap // block
    grid_spec = pltpu.PrefetchScalarGridSpec(
        num_scalar_prefetch=1,
        grid=(b, nblk),
        in_specs=[pl.BlockSpec((cap, LANES), lambda i, j, bd: (0, 0)),
                  pl.BlockSpec((1, n_experts, s), lambda i, j, bd: (i, 0, 0)),
                  pl.BlockSpec((1, s, d), lambda i, j, bd: (i, 0, 0))],
        out_specs=[pl.BlockSpec((n_experts, block, d), lambda i, j, bd: (0, i * nblk + j, 0)),
                   pl.BlockSpec((n_experts, block, 1), lambda i, j, bd: (0, i * nblk + j, 0))],
    )
    return pl.pallas_call(
        functools.partial(_dispatch_kernel, window=window, group=4),
        grid_spec=grid_spec,
        out_shape=[jax.ShapeDtypeStruct((n_experts, b * cap, d), BF16),
                   jax.ShapeDtypeStruct((n_experts, b * cap, 1), F32)],
        compiler_params=_cparams(2, 48),
        name="dispatch",
    )(bounds, idx, aff_t, h3)


def _combine_kernel(starts_ref, slot_ref, y_ref, x_ref, g_ref, o_ref, *, win, final_norm):
    bi = pl.program_id(0)
    ti = pl.program_id(1)
    tt, d = x_ref.shape
    ne, cap, _ = y_ref.shape

    wins = []
    fits = None
    for k in range(ne):
        lo = starts_ref[bi * ne + k, ti]
        hi = starts_ref[bi * ne + k, ti + 1]
        w0 = jnp.minimum(jnp.bitwise_and(lo, -BF16_ROWS), cap - win)
        ok = hi - w0 <= win
        fits = ok if fits is None else jnp.logical_and(fits, ok)
        wins.append(w0)

    slots = slot_ref[...].astype(BF16)

    def onehot(width, first_slot):
        shift = width.bit_length() - 1
        n = ne * width
        lane_of = bi * ne + lax.shift_right_logical(lax.broadcasted_iota(jnp.int32, (LANES, n), 1), shift)
        spread = (lax.broadcasted_iota(jnp.int32, (LANES, n), 0) == lane_of).astype(BF16)
        slot_b = jnp.dot(slots, spread, preferred_element_type=F32)
        c = lax.broadcasted_iota(jnp.int32, (1, n), 1)
        target = jnp.bitwise_and(c, width - 1)
        if first_slot is not None:
            kk = lax.shift_right_logical(c, shift)
            for k in range(ne):
                target = target + jnp.where(kk == k, first_slot[k], 0)
        return (slot_b == target.astype(F32)).astype(BF16)

    def finish(contrib):
        x = x_ref[...] + contrib
        if final_norm:
            inv = lax.rsqrt(jnp.mean(x * x, axis=-1, keepdims=True) + EPS)
            x = (x * inv) * g_ref[...]
        o_ref[...] = x

    @pl.when(fits)
    def _():
        ywin = jnp.concatenate([y_ref[k, pl.ds(pl.multiple_of(wins[k], BF16_ROWS), win), :] for k in range(ne)],
                               axis=0)
        finish(jnp.dot(onehot(win, wins), ywin, preferred_element_type=F32))

    @pl.when(jnp.logical_not(fits))
    def _():
        finish(jnp.dot(onehot(cap, None), y_ref[...].reshape(ne * cap, d), preferred_element_type=F32))


def _combine(starts, slot_cols, y, x1, g, *, batch, cap, tt, win, final_norm):
    m, d = x1.shape
    ne = y.shape[0]
    s = m // batch
    tps = s // tt
    assert slot_cols.shape == (s, LANES) and batch * ne == LANES
    assert win % BF16_ROWS == 0 and win <= cap and win & (win - 1) == 0 and cap & (cap - 1) == 0
    grid_spec = pltpu.PrefetchScalarGridSpec(
        num_scalar_prefetch=1,
        grid=(batch, tps),
        in_specs=[pl.BlockSpec((tt, LANES), lambda b, t, st: (t, 0)),
                  pl.BlockSpec((ne, cap, d), lambda b, t, st: (0, b, 0)),
                  pl.BlockSpec((tt, d), lambda b, t, st: (b * tps + t, 0)),
                  pl.BlockSpec((1, d), lambda b, t, st: (0, 0))],
        out_specs=pl.BlockSpec((tt, d), lambda b, t, st: (b * tps + t, 0)),
    )
    return pl.pallas_call(
        functools.partial(_combine_kernel, win=win, final_norm=final_norm),
        grid_spec=grid_spec,
        out_shape=jax.ShapeDtypeStruct((m, d), F32),
        compiler_params=_cparams(2, 56),
        name="combine",
    )(starts, slot_cols, y, x1, g.reshape(1, d))


def _expert_up_kernel(x_ref, wg_ref, wu_ref, o_ref, *, chunk):
    wg = wg_ref[0].astype(BF16)
    wu = wu_ref[0].astype(BF16)
    for r0 in range(0, x_ref.shape[1], chunk):
        x = x_ref[0, pl.ds(r0, chunk), :]
        gate = jnp.dot(x, wg, preferred_element_type=F32)
        up = jnp.dot(x, wu, preferred_element_type=F32)
        o_ref[0, pl.ds(r0, chunk), :] = (jax.nn.silu(gate) * up).astype(o_ref.dtype)


def _expert_up(xin, w_gate, w_up, *, tf):
    e, m, d = xin.shape
    f = w_gate.shape[2]
    return pl.pallas_call(
        functools.partial(_expert_up_kernel, chunk=512),
        grid=(e, f // tf),
        in_specs=[pl.BlockSpec((1, m, d), lambda i, j: (i, 0, 0)),
                  pl.BlockSpec((1, d, tf), lambda i, j: (i, 0, j)),
                  pl.BlockSpec((1, d, tf), lambda i, j: (i, 0, j))],
        out_specs=pl.BlockSpec((1, m, tf), lambda i, j: (i, 0, j)),
        out_shape=jax.ShapeDtypeStruct((e, m, f), BF16),
        compiler_params=_cparams(2, 56),
        name="expert_up",
    )(xin, w_gate, w_up)


def _expert_down_kernel(h_ref, wd_ref, a_ref, o_ref, *, chunk):
    wd = wd_ref[0].astype(BF16)
    for r0 in range(0, h_ref.shape[1], chunk):
        y = jnp.dot(h_ref[0, pl.ds(r0, chunk), :], wd, preferred_element_type=F32)
        o_ref[0, pl.ds(r0, chunk), :] = (y * a_ref[0, pl.ds(r0, chunk), :]).astype(o_ref.dtype)


def _expert_down(hidden, w_down, aff, *, tn):
    e, m, f = hidden.shape
    d = w_down.shape[2]
    return pl.pallas_call(
        functools.partial(_expert_down_kernel, chunk=512),
        grid=(e, d // tn),
        in_specs=[pl.BlockSpec((1, m, f), lambda i, j: (i, 0, 0)),
                  pl.BlockSpec((1, f, tn), lambda i, j: (i, 0, j)),
                  pl.BlockSpec((1, m, 1), lambda i, j: (i, 0, 0))],
        out_specs=pl.BlockSpec((1, m, tn), lambda i, j: (i, 0, j)),
        out_shape=jax.ShapeDtypeStruct((e, m, d), BF16),
        compiler_params=_cparams(2, 56),
        name="expert_down",
    )(hidden, w_down, aff)


def kernel(x, mem, norm_mix_g, norm_mem_g, w_in, pool_w, pool_scale, fourier_w, w_kv_mem, proj_pool, proj_fourier,
           proj_mem, w_out, norm_ffn_g, w_router, w_expert_gate, w_expert_up, w_expert_down, norm_final_g):
    b, s, d = x.shape
    depth = w_in.shape[0]
    n_tok = b * s
    pool_width = pool_w.shape[1] * pool_w.shape[2]
    four_width = fourier_w.shape[1] * fourier_w.shape[2]
    mem_width = proj_mem.shape[1]
    mix_width = pool_width + four_width + mem_width
    n_heads = 4
    e = w_router.shape[2]
    cap = CAPACITY_FACTOR * s // e

    xf = x.reshape(n_tok, d)
    for l in range(depth):
        memn = _rmsnorm(mem.reshape(-1, d), norm_mem_g[l], tm=512, out_dtype=BF16)
        kv = _matmul(memn, w_kv_mem[l], n_cols=2 * mem_width, tm=1024, tn=512, out_dtype=BF16, name="kv_proj")
        h, zp, zf, yc = _in_proj(xf, norm_mix_g[l], w_in[l], kv.reshape(b, -1, 2 * mem_width),
                                 pool_width=pool_width, four_width=four_width, n_heads=n_heads, tm=512)
        ya = _pool_mixer(zp.reshape(b, s, pool_width), pool_w[l], pool_scale[l])
        yb = _fourier_mixer(zf.reshape(b, s, four_width), fourier_w[l], col_block=0, tk=1024)
        merged = _gated_merge(ya.reshape(n_tok, -1), yb.reshape(n_tok, -1), yc, h,
                              proj_pool[l], proj_fourier[l], proj_mem[l], w_in[l],
                              gate_col0=mix_width, tm=1024, tn=256)

        x1, h2, aff_t = _out_proj(merged, w_out[l], xf, norm_ffn_g[l], w_router[l], batch=b, tm=512)
        slot_cols, starts, idx, bounds = _select(aff_t.reshape(b * e, s), cap=cap, tile=COMBINE_TILE,
                                                 block=DISPATCH_BLOCK)
        xin, aff_slot = _dispatch(bounds, idx, aff_t, h2.reshape(b, s, d), n_experts=e, cap=cap,
                                  block=DISPATCH_BLOCK, window=DISPATCH_WINDOW)
        hidden = _expert_up(xin, w_expert_gate[l], w_expert_up[l], tf=512)
        y = _expert_down(hidden, w_expert_down[l], aff_slot, tn=1024)
        last = l + 1 == depth
        xf = _combine(starts[:, :s // COMBINE_TILE + 1], slot_cols, y, x1, norm_final_g, batch=b, cap=cap,
                      tt=COMBINE_TILE, win=COMBINE_WINDOW, final_norm=last)
    return xf.reshape(b, s, d)
```

```python
import functools

import numpy as np
import jax
import jax.numpy as jnp
from jax import lax
from jax.experimental import pallas as pl
from jax.experimental.pallas import tpu as pltpu

F32 = jnp.float32
BF16 = jnp.bfloat16

EPS = 1e-6
POOL_WINDOWS = (2, 4, 8, 16)
N_EXPERTS = 16
CAPACITY_FACTOR = 2
LANES = 128
BF16_ROWS = 16
POOL_HALO = 32
ROW_CHUNK = 256
GATE_CAST_SLAB = 256
DISPATCH_BLOCK = 64
DISPATCH_WINDOW = 768
COMBINE_TILE = 256
COMBINE_WINDOW = 64
MIB = 1024 * 1024


def _cparams(n_axes, vmem_mib):
    return pltpu.CompilerParams(
        dimension_semantics=("arbitrary",) * n_axes,
        vmem_limit_bytes=vmem_mib * MIB,
    )


def _rmsnorm_kernel(x_ref, g_ref, o_ref):
    x = x_ref[...]
    inv = lax.rsqrt(jnp.mean(x * x, axis=-1, keepdims=True) + EPS)
    o_ref[...] = ((x * inv) * g_ref[...]).astype(o_ref.dtype)


def _rmsnorm(x, g, *, tm, out_dtype):
    m, d = x.shape
    return pl.pallas_call(
        _rmsnorm_kernel,
        grid=(m // tm,),
        in_specs=[pl.BlockSpec((tm, d), lambda i: (i, 0)), pl.BlockSpec((1, d), lambda i: (0, 0))],
        out_specs=pl.BlockSpec((tm, d), lambda i: (i, 0)),
        out_shape=jax.ShapeDtypeStruct((m, d), out_dtype),
        compiler_params=_cparams(1, 40),
        name="rmsnorm",
    )(x, g.reshape(1, d))


def _mm_kernel(a_ref, w_ref, o_ref, wc_ref):
    @pl.when(pl.program_id(1) == 0)
    def _():
        wc_ref[...] = w_ref[...].astype(BF16)

    o_ref[...] = jnp.dot(a_ref[...], wc_ref[...], preferred_element_type=F32).astype(o_ref.dtype)


def _matmul(a, w, *, n_cols, tm, tn, out_dtype, name):
    m, k = a.shape
    return pl.pallas_call(
        _mm_kernel,
        grid=(n_cols // tn, m // tm),
        in_specs=[pl.BlockSpec((tm, k), lambda n, i: (i, 0)), pl.BlockSpec((k, tn), lambda n, i: (0, n))],
        out_specs=pl.BlockSpec((tm, tn), lambda n, i: (i, n)),
        out_shape=jax.ShapeDtypeStruct((m, n_cols), out_dtype),
        scratch_shapes=[pltpu.VMEM((k, tn), BF16)],
        compiler_params=_cparams(2, 48),
        name=name,
    )(a, w)


def _in_proj_kernel(x_ref, g_ref, w_ref, kv_ref, wg_ref, h_ref, zp_ref, zf_ref, yc_ref, wgb_ref, wc_ref, *, n_heads):
    @pl.when(pl.program_id(0) == 0)
    def _():
        wc_ref[...] = w_ref[...].astype(BF16)

    wgb_ref[...] = wg_ref[...].astype(BF16)

    wp = zp_ref.shape[1]
    wf = zf_ref.shape[1]
    wm = yc_ref.shape[1]
    dh = wm // n_heads
    scale = float(dh ** -0.5)
    chunks = [pl.ds(r0, ROW_CHUNK) for r0 in range(0, x_ref.shape[0], ROW_CHUNK)]
    queries = []
    for rs in chunks:
        x = x_ref[rs, :]
        inv = lax.rsqrt(jnp.mean(x * x, axis=-1, keepdims=True) + EPS)
        hb = ((x * inv) * g_ref[...]).astype(BF16)
        h_ref[rs, :] = hb
        for c0 in range(0, wp, wf):
            zp_ref[rs, c0:c0 + wf] = jnp.dot(hb, wc_ref[:, c0:c0 + wf], preferred_element_type=F32)
        zf_ref[rs, :] = jnp.dot(hb, wc_ref[:, wp:wp + wf], preferred_element_type=F32).astype(BF16)
        queries.append(jnp.dot(hb, wc_ref[:, wp + wf:wp + wf + wm], preferred_element_type=F32).astype(BF16))
    pairs = [(ci, hd) for ci in range(len(chunks)) for hd in range(n_heads)]
    scores = [lax.dot_general(queries[ci][:, hd * dh:(hd + 1) * dh], kv_ref[0, :, hd * dh:(hd + 1) * dh],
                              (((1,), (1,)), ((), ())), preferred_element_type=F32) * scale for ci, hd in pairs]
    probs = []
    for sc in scores:
        p = jnp.exp(sc - jnp.max(sc, axis=-1, keepdims=True))
        probs.append((p / jnp.sum(p, axis=-1, keepdims=True)).astype(BF16))
    for (ci, hd), p in zip(pairs, probs):
        o = jnp.dot(p, kv_ref[0, :, wm + hd * dh:wm + (hd + 1) * dh], preferred_element_type=F32)
        yc_ref[chunks[ci], hd * dh:(hd + 1) * dh] = o.astype(BF16)


def _in_proj(x, g, w_in, kv3, *, pool_width, four_width, n_heads, tm):
    m, d = x.shape
    bsz, mlen, kvw = kv3.shape
    mem_width = kvw // 2
    mix_width = pool_width + four_width + mem_width
    tps = m // bsz // tm
    gate_width = w_in.shape[1] - mix_width
    slab = GATE_CAST_SLAB
    n_slabs = gate_width // slab
    assert gate_width % slab == 0 and mix_width % slab == 0 and n_slabs <= m // tm
    row = lambda width: pl.BlockSpec((tm, width), lambda i: (i, 0))
    return pl.pallas_call(
        functools.partial(_in_proj_kernel, n_heads=n_heads),
        grid=(m // tm,),
        in_specs=[row(d), pl.BlockSpec((1, d), lambda i: (0, 0)),
                  pl.BlockSpec((d, mix_width), lambda i: (0, 0), pipeline_mode=pl.Buffered(1)),
                  pl.BlockSpec((1, mlen, kvw), lambda i: (i // tps, 0, 0)),
                  pl.BlockSpec((d, slab), lambda i: (0, mix_width // slab + jnp.minimum(i, n_slabs - 1)))],
        out_specs=[row(d), row(pool_width), row(four_width), row(mem_width),
                   pl.BlockSpec((d, slab), lambda i: (0, jnp.minimum(i, n_slabs - 1)))],
        out_shape=[jax.ShapeDtypeStruct((m, d), BF16), jax.ShapeDtypeStruct((m, pool_width), F32),
                   jax.ShapeDtypeStruct((m, four_width), BF16), jax.ShapeDtypeStruct((m, mem_width), BF16),
                   jax.ShapeDtypeStruct((d, gate_width), BF16)],
        scratch_shapes=[pltpu.VMEM((d, mix_width), BF16)],
        compiler_params=_cparams(1, 56),
        name="in_proj",
    )(x, g.reshape(1, d), w_in, kv3, w_in)


def _out_proj_kernel(m_ref, w_ref, r_ref, g_ref, wr_ref, x1_ref, h2_ref, afft_ref, wc_ref, *, chunk, n_experts):
    @pl.when(pl.program_id(0) == 0)
    def _():
        wc_ref[...] = w_ref[...].astype(BF16)

    wr = wr_ref[...].astype(BF16)
    for r0 in range(0, m_ref.shape[0], ROW_CHUNK):
        rs = pl.ds(r0, ROW_CHUNK)
        a = m_ref[rs, :]
        for c0 in range(0, wc_ref.shape[1], chunk):
            cols = slice(c0, c0 + chunk)
            x1_ref[rs, cols] = r_ref[rs, cols] + jnp.dot(a, wc_ref[:, cols], preferred_element_type=F32)

        x = x1_ref[rs, :]
        inv = lax.rsqrt(jnp.mean(x * x, axis=-1, keepdims=True) + EPS)
        hb = ((x * inv) * g_ref[...]).astype(BF16)
        h2_ref[rs, :] = hb
        logits = jnp.dot(hb, wr, preferred_element_type=F32)
        lane = lax.broadcasted_iota(jnp.int32, logits.shape, 1)
        logits = jnp.where(lane < n_experts, logits, -1e30)
        mx = jnp.max(logits, axis=-1, keepdims=True)
        p = jnp.exp(logits - mx)
        aff = p / jnp.sum(p, axis=-1, keepdims=True)
        afft_ref[0, :, rs] = aff.T[:n_experts, :]


def _out_proj(merged, w_out, resid, g, w_router, *, batch, tm):
    m, d = resid.shape
    e = w_router.shape[1]
    s = m // batch
    tps = s // tm
    wr = jnp.pad(w_router, ((0, 0), (0, LANES - e)))
    row = pl.BlockSpec((tm, d), lambda i: (i, 0))
    return pl.pallas_call(
        functools.partial(_out_proj_kernel, chunk=512, n_experts=e),
        grid=(m // tm,),
        in_specs=[row, pl.BlockSpec((d, d), lambda i: (0, 0), pipeline_mode=pl.Buffered(1)), row,
                  pl.BlockSpec((1, d), lambda i: (0, 0)), pl.BlockSpec((d, LANES), lambda i: (0, 0))],
        out_specs=[row, row, pl.BlockSpec((1, e, tm), lambda i: (i // tps, 0, i % tps))],
        out_shape=[jax.ShapeDtypeStruct((m, d), F32), jax.ShapeDtypeStruct((m, d), BF16),
                   jax.ShapeDtypeStruct((batch, e, s), F32)],
        scratch_shapes=[pltpu.VMEM((d, d), BF16)],
        compiler_params=_cparams(1, 56),
        name="out_proj",
    )(merged, w_out, resid, g.reshape(1, d), wr)


def _pool_kernel(u_ref, pw_ref, ps_ref, o_ref, a_ref, b_ref, *, chunk):
    s = u_ref.shape[1]
    c = pw_ref.shape[1]
    h = POOL_HALO
    rows = s + 2 * h
    zeros = jnp.zeros((h, c), F32)

    def level(src, dst, off_lo, off_hi, margin):
        for r0 in range(margin, rows - margin, chunk):
            n = min(chunk, rows - margin - r0)
            dst[pl.ds(r0, n), :] = src[pl.ds(r0 + off_lo, n), :] + src[pl.ds(r0 + off_hi, n), :]

    for g, w in enumerate(POOL_WINDOWS):
        cols = slice(g * c, (g + 1) * c)
        a_ref[pl.ds(0, h), :] = zeros
        a_ref[pl.ds(h + s, h), :] = zeros
        a_ref[pl.ds(h, s), :] = u_ref[0, :, cols]
        src, off_lo, off_hi = a_ref, -1, 0
        if w >= 4:
            level(a_ref, b_ref, -1, 0, 8)
            src, off_lo, off_hi = b_ref, -1, 1
        if w >= 8:
            level(b_ref, a_ref, -1, 1, 16)
            src, off_lo, off_hi = a_ref, -2, 2
        if w >= 16:
            level(a_ref, b_ref, -2, 2, 24)
            src, off_lo, off_hi = b_ref, -4, 4
        wg = pw_ref[g].astype(BF16)
        scale = ps_ref[:, cols]
        for r0 in range(0, s, chunk):
            pos = r0 + lax.broadcasted_iota(jnp.int32, (chunk, 1), 0)
            lo = jnp.maximum(pos - w // 2, 0)
            hi = jnp.minimum(pos + (w - w // 2), s)
            cnt = (hi - lo).astype(F32)
            tot = src[pl.ds(h + r0 + off_lo, chunk), :] + src[pl.ds(h + r0 + off_hi, chunk), :]
            pooled = tot / cnt - u_ref[0, pl.ds(r0, chunk), cols]
            y = jnp.dot(pooled.astype(BF16), wg, preferred_element_type=F32)
            o_ref[0, pl.ds(r0, chunk), cols] = (y * scale).astype(o_ref.dtype)


def _pool_mixer(z3, pool_w, pool_scale):
    b, s, _ = z3.shape
    g, c, _ = pool_w.shape
    width = g * c
    assert POOL_WINDOWS == (2, 4, 8, 16) and g == len(POOL_WINDOWS)
    return pl.pallas_call(
        functools.partial(_pool_kernel, chunk=256),
        grid=(b,),
        in_specs=[pl.BlockSpec((1, s, width), lambda i: (i, 0, 0)),
                  pl.BlockSpec((g, c, c), lambda i: (0, 0, 0)),
                  pl.BlockSpec((1, width), lambda i: (0, 0))],
        out_specs=pl.BlockSpec((1, s, width), lambda i: (i, 0, 0)),
        out_shape=jax.ShapeDtypeStruct((b, s, width), BF16),
        scratch_shapes=[pltpu.VMEM((s + 2 * POOL_HALO, c), F32), pltpu.VMEM((s + 2 * POOL_HALO, c), F32)],
        compiler_params=_cparams(1, 48),
        name="pool_mixer",
    )(z3, pool_w, pool_scale.reshape(1, width))


def _dft_tables(s, c):
    def tab(n):
        r = np.outer(np.arange(n), np.arange(n)) % n
        ang = r * (2.0 * np.pi / n)
        return np.cos(ang), np.sin(ang)
    cs, ss = tab(s)
    cc, sc = tab(c)
    return (jnp.asarray(np.concatenate([cs, -ss], axis=1), dtype=F32).astype(BF16),
            jnp.asarray(np.concatenate([cc, sc], axis=1), dtype=F32).astype(BF16))


def _fourier_kernel(u_ref, dft_ref, ccsc_ref, fw_ref, o_ref, t_ref, *, norm):
    s = u_ref.shape[1]
    ng, c, _ = fw_ref.shape

    @pl.when(pl.program_id(1) == 0)
    def _():
        for g in range(ng):
            ug = u_ref[0, :, g * c:(g + 1) * c]
            ab = jnp.dot(ug, ccsc_ref[...], preferred_element_type=F32)
            t_ref[pl.ds(0, s), g * c:(g + 1) * c] = ab[:, :c].astype(BF16)
            t_ref[pl.ds(s, s), g * c:(g + 1) * c] = ab[:, c:].astype(BF16)

    f = jnp.dot(dft_ref[...], t_ref[...], preferred_element_type=F32) * norm
    for g in range(ng):
        y = jnp.dot(f[:, g * c:(g + 1) * c].astype(BF16), fw_ref[g].astype(BF16), preferred_element_type=F32)
        o_ref[0, :, g * c:(g + 1) * c] = y.astype(o_ref.dtype)


def _fourier_mixer(z3, fourier_w, *, col_block, tk):
    b, s, _ = z3.shape
    ng, c, _ = fourier_w.shape
    width = ng * c
    dft, ccsc = _dft_tables(s, c)
    return pl.pallas_call(
        functools.partial(_fourier_kernel, norm=float((s * c) ** -0.5)),
        grid=(b, s // tk),
        in_specs=[pl.BlockSpec((1, s, width), lambda i, k: (i, 0, col_block)),
                  pl.BlockSpec((tk, 2 * s), lambda i, k: (k, 0)),
                  pl.BlockSpec((c, 2 * c), lambda i, k: (0, 0)),
                  pl.BlockSpec((ng, c, c), lambda i, k: (0, 0, 0))],
        out_specs=pl.BlockSpec((1, tk, width), lambda i, k: (i, k, 0)),
        out_shape=jax.ShapeDtypeStruct((b, s, width), BF16),
        scratch_shapes=[pltpu.VMEM((2 * s, width), BF16)],
        compiler_params=_cparams(2, 48),
        name="fourier_mixer",
    )(z3, dft, ccsc, fourier_w)


def _merge_kernel(ya_ref, yb_ref, yc_ref, h_ref, pp_ref, pf_ref, pm_ref, cg0, cg1, cg2, o_ref, cpp, cpf, cpm):
    @pl.when(pl.program_id(1) == 0)
    def _():
        for src, dst in ((pp_ref, cpp), (pf_ref, cpf), (pm_ref, cpm)):
            dst[...] = src[...].astype(BF16)

    h = h_ref[...]

    def branch(y_ref, proj, gate_w):
        gate = jax.nn.sigmoid(jnp.dot(h, gate_w[...], preferred_element_type=F32))
        return gate * jnp.dot(y_ref[...], proj[...], preferred_element_type=F32)

    acc = branch(ya_ref, cpp, cg0)
    acc = acc + branch(yb_ref, cpf, cg1)
    acc = acc + branch(yc_ref, cpm, cg2)
    o_ref[...] = acc.astype(o_ref.dtype)


def _gated_merge(ya, yb, yc, h, proj_pool, proj_fourier, proj_mem, w_gates, *, tm, tn):
    m, d = h.shape
    nb = d // tn
    act = lambda width: pl.BlockSpec((tm, width), lambda n, i: (i, 0))
    wsp = lambda rows, off: pl.BlockSpec((rows, tn), lambda n, i: (0, n + off))
    return pl.pallas_call(
        _merge_kernel,
        grid=(d // tn, m // tm),
        in_specs=[act(ya.shape[1]), act(yb.shape[1]), act(yc.shape[1]), act(d),
                  wsp(proj_pool.shape[0], 0), wsp(proj_fourier.shape[0], 0), wsp(proj_mem.shape[0], 0),
                  wsp(d, 0), wsp(d, nb), wsp(d, 2 * nb)],
        out_specs=pl.BlockSpec((tm, tn), lambda n, i: (i, n)),
        out_shape=jax.ShapeDtypeStruct((m, d), BF16),
        scratch_shapes=[pltpu.VMEM((proj_pool.shape[0], tn), BF16), pltpu.VMEM((proj_fourier.shape[0], tn), BF16),
                        pltpu.VMEM((proj_mem.shape[0], tn), BF16)],
        compiler_params=_cparams(2, 56),
        name="gated_merge",
    )(ya, yb, yc, h, proj_pool, proj_fourier, proj_mem, w_gates, w_gates, w_gates)


def _select_kernel(aff_ref, tri_ref, slot_ref, slot_t_ref, starts_ref, bounds_ref, *, cap, tile, block):
    a = aff_ref[...]
    rows = a.shape[0]
    capf = float(cap)

    def count(pred):
        return jnp.sum(pred.astype(F32), axis=-1, keepdims=True)

    def body(i, t_bits):
        cand = t_bits | jnp.left_shift(jnp.int32(1), 30 - i)
        return jnp.where(count(a >= pltpu.bitcast(cand, F32)) >= capf, cand, t_bits)

    t = pltpu.bitcast(lax.fori_loop(0, 31, body, jnp.zeros((rows, 1), jnp.int32)), F32)
    gt = a > t
    eq = a == t
    need = capf - count(gt)
    tri = tri_ref[...]
    eq_rank = jnp.dot(eq.astype(BF16), tri, preferred_element_type=F32)
    sel = gt | (eq & (eq_rank < need))
    pos = jnp.dot(sel.astype(BF16), tri, preferred_element_type=F32)
    slot = jnp.where(sel, pos, -1.0)
    slot_ref[...] = slot
    slot_t_ref[...] = slot.T
    s = a.shape[1]
    lane = lax.broadcasted_iota(jnp.int32, (rows, LANES), 1)
    starts = jnp.where(lane == s // tile, capf, 0.0)
    for t in range(s // tile):
        starts = jnp.where(lane == t, pos[:, t * tile:t * tile + 1], starts)
    starts_ref[...] = starts.astype(jnp.int32)

    nblk = cap // block
    tok = lax.broadcasted_iota(jnp.int32, (1, s), 1).astype(F32)

    def token_of(j):
        return jnp.sum(jnp.where(slot == float(j), tok, 0.0), axis=-1, keepdims=True)

    bounds = jnp.zeros((rows, LANES), F32)
    for i in range(nblk):
        bounds = jnp.where(lane == i, token_of(i * block), bounds)
        bounds = jnp.where(lane == nblk + i, token_of((i + 1) * block - 1) + 1.0, bounds)
    bounds_ref[...] = bounds.astype(jnp.int32)


def _select(aff_rows, *, cap, tile, block):
    rows, s = aff_rows.shape
    assert rows == LANES
    idx = jnp.arange(s, dtype=jnp.int32)
    tri = (idx[:, None] < idx[None, :]).astype(BF16)
    full = lambda shape: pl.BlockSpec(shape, lambda i: (0, 0))
    assert 2 * cap // block <= LANES and s // tile + 1 <= LANES
    return pl.pallas_call(
        functools.partial(_select_kernel, cap=cap, tile=tile, block=block),
        grid=(1,),
        in_specs=[full((rows, s)), full((s, s))],
        out_specs=[full((rows, s)), full((s, rows)), full((rows, LANES)), full((rows, LANES))],
        out_shape=[jax.ShapeDtypeStruct((rows, s), F32), jax.ShapeDtypeStruct((s, rows), F32),
                   jax.ShapeDtypeStruct((rows, LANES), jnp.int32), jax.ShapeDtypeStruct((rows, LANES), jnp.int32)],
        compiler_params=_cparams(1, 48),
        name="expert_select",
    )(aff_rows, tri)


def _dispatch_kernel(bounds_ref, slot_ref, aff_ref, h_ref, o_ref, oa_ref, *, window, group):
    bi = pl.program_id(0)
    ji = pl.program_id(1)
    nblk = pl.num_programs(1)
    ne, block, d = o_ref.shape
    s = h_ref.shape[1]

    lo = hi = None
    for k in range(ne):
        l = bounds_ref[bi * ne + k, ji]
        h = bounds_ref[bi * ne + k, nblk + ji]
        lo = l if lo is None else jnp.minimum(lo, l)
        hi = h if hi is None else jnp.maximum(hi, h)
    w0 = jnp.minimum(jnp.bitwise_and(lo, -BF16_ROWS), s - window)
    fits = hi - w0 <= window

    tok_all = lax.broadcasted_iota(jnp.int32, (1, s), 1).astype(F32)
    slot_ids = (ji * block + lax.broadcasted_iota(jnp.int32, (block, 1), 0)).astype(F32)

    def tokens_of(g0):
        cols = []
        for k in range(g0, g0 + group):
            hit = slot_ref[0, k:k + 1, :] == slot_ids
            cols.append(jnp.sum(jnp.where(hit, tok_all, 0.0), axis=-1, keepdims=True))
            oa_ref[k] = jnp.sum(jnp.where(hit, aff_ref[0, k:k + 1, :], 0.0), axis=-1, keepdims=True)
        return jnp.concatenate(cols, axis=0)

    def gather(width, first_tok, src):
        t = lax.broadcasted_iota(jnp.int32, (group * block, width), 1).astype(F32) + first_tok
        toks = tokens_of(0)
        for g0 in range(0, ne, group):
            onehot = (toks == t).astype(BF16)
            if g0 + group < ne:
                toks = tokens_of(g0 + group)
            out = jnp.dot(onehot, src, preferred_element_type=F32)
            o_ref[g0:g0 + group] = out.reshape(group, block, d).astype(o_ref.dtype)

    @pl.when(fits)
    def _():
        gather(window, w0.astype(F32), h_ref[0, pl.ds(pl.multiple_of(w0, BF16_ROWS), window), :])

    @pl.when(jnp.logical_not(fits))
    def _():
        gather(s, 0.0, h_ref[0])


def _dispatch(bounds, slot3, aff_t, h3, *, n_experts, cap, block, window):
    b, s, d = h3.shape
    assert window % BF16_ROWS == 0 and window <= s and cap % block == 0 and block % BF16_ROWS == 0
    nblk = cap // block
    grid_spec = pltpu.PrefetchScalarGridSpec(
        num_scalar_prefetch=1,
        grid=(b, nblk),
        in_specs=[pl.BlockSpec((1, n_experts, s), lambda i, j, bd: (i, 0, 0)),
                  pl.BlockSpec((1, n_experts, s), lambda i, j, bd: (i, 0, 0)),
                  pl.BlockSpec((1, s, d), lambda i, j, bd: (i, 0, 0))],
        out_specs=[pl.BlockSpec((n_experts, block, d), lambda i, j, bd: (0, i * nblk + j, 0)),
                   pl.BlockSpec((n_experts, block, 1), lambda i, j, bd: (0, i * nblk + j, 0))],
    )
    return pl.pallas_call(
        functools.partial(_dispatch_kernel, window=window, group=4),
        grid_spec=grid_spec,
        out_shape=[jax.ShapeDtypeStruct((n_experts, b * cap, d), BF16),
                   jax.ShapeDtypeStruct((n_experts, b * cap, 1), F32)],
        compiler_params=_cparams(2, 48),
        name="dispatch",
    )(bounds, slot3, aff_t, h3)


def _combine_kernel(starts_ref, slot_ref, y_ref, x_ref, g_ref, o_ref, *, win, final_norm):
    bi = pl.program_id(0)
    ti = pl.program_id(1)
    tt, d = x_ref.shape
    ne, cap, _ = y_ref.shape

    wins = []
    fits = None
    for k in range(ne):
        lo = starts_ref[bi * ne + k, ti]
        hi = starts_ref[bi * ne + k, ti + 1]
        w0 = jnp.minimum(jnp.bitwise_and(lo, -BF16_ROWS), cap - win)
        ok = hi - w0 <= win
        fits = ok if fits is None else jnp.logical_and(fits, ok)
        wins.append(w0)

    slots = slot_ref[...].astype(BF16)

    def onehot(width, first_slot):
        shift = width.bit_length() - 1
        n = ne * width
        lane_of = bi * ne + lax.shift_right_logical(lax.broadcasted_iota(jnp.int32, (LANES, n), 1), shift)
        spread = (lax.broadcasted_iota(jnp.int32, (LANES, n), 0) == lane_of).astype(BF16)
        slot_b = jnp.dot(slots, spread, preferred_element_type=F32)
        c = lax.broadcasted_iota(jnp.int32, (1, n), 1)
        target = jnp.bitwise_and(c, width - 1)
        if first_slot is not None:
            kk = lax.shift_right_logical(c, shift)
            for k in range(ne):
                target = target + jnp.where(kk == k, first_slot[k], 0)
        return (slot_b == target.astype(F32)).astype(BF16)

    def finish(contrib):
        x = x_ref[...] + contrib
        if final_norm:
            inv = lax.rsqrt(jnp.mean(x * x, axis=-1, keepdims=True) + EPS)
            x = (x * inv) * g_ref[...]
        o_ref[...] = x

    @pl.when(fits)
    def _():
        ywin = jnp.concatenate([y_ref[k, pl.ds(pl.multiple_of(wins[k], BF16_ROWS), win), :] for k in range(ne)],
                               axis=0)
        finish(jnp.dot(onehot(win, wins), ywin, preferred_element_type=F32))

    @pl.when(jnp.logical_not(fits))
    def _():
        finish(jnp.dot(onehot(cap, None), y_ref[...].reshape(ne * cap, d), preferred_element_type=F32))


def _combine(starts, slot_cols, y, x1, g, *, batch, cap, tt, win, final_norm):
    m, d = x1.shape
    ne = y.shape[0]
    s = m // batch
    tps = s // tt
    assert slot_cols.shape == (s, LANES) and batch * ne == LANES
    assert win % BF16_ROWS == 0 and win <= cap and win & (win - 1) == 0 and cap & (cap - 1) == 0
    grid_spec = pltpu.PrefetchScalarGridSpec(
        num_scalar_prefetch=1,
        grid=(batch, tps),
        in_specs=[pl.BlockSpec((tt, LANES), lambda b, t, st: (t, 0)),
                  pl.BlockSpec((ne, cap, d), lambda b, t, st: (0, b, 0)),
                  pl.BlockSpec((tt, d), lambda b, t, st: (b * tps + t, 0)),
                  pl.BlockSpec((1, d), lambda b, t, st: (0, 0))],
        out_specs=pl.BlockSpec((tt, d), lambda b, t, st: (b * tps + t, 0)),
    )
    return pl.pallas_call(
        functools.partial(_combine_kernel, win=win, final_norm=final_norm),
        grid_spec=grid_spec,
        out_shape=jax.ShapeDtypeStruct((m, d), F32),
        compiler_params=_cparams(2, 56),
        name="combine",
    )(starts, slot_cols, y, x1, g.reshape(1, d))


def _expert_up_kernel(x_ref, wg_ref, wu_ref, o_ref, *, chunk):
    wg = wg_ref[0].astype(BF16)
    wu = wu_ref[0].astype(BF16)
    for r0 in range(0, x_ref.shape[1], chunk):
        x = x_ref[0, pl.ds(r0, chunk), :]
        gate = jnp.dot(x, wg, preferred_element_type=F32)
        up = jnp.dot(x, wu, preferred_element_type=F32)
        o_ref[0, pl.ds(r0, chunk), :] = (jax.nn.silu(gate) * up).astype(o_ref.dtype)


def _expert_up(xin, w_gate, w_up, *, tf):
    e, m, d = xin.shape
    f = w_gate.shape[2]
    return pl.pallas_call(
        functools.partial(_expert_up_kernel, chunk=512),
        grid=(e, f // tf),
        in_specs=[pl.BlockSpec((1, m, d), lambda i, j: (i, 0, 0)),
                  pl.BlockSpec((1, d, tf), lambda i, j: (i, 0, j)),
                  pl.BlockSpec((1, d, tf), lambda i, j: (i, 0, j))],
        out_specs=pl.BlockSpec((1, m, tf), lambda i, j: (i, 0, j)),
        out_shape=jax.ShapeDtypeStruct((e, m, f), BF16),
        compiler_params=_cparams(2, 56),
        name="expert_up",
    )(xin, w_gate, w_up)


def _expert_down_kernel(h_ref, wd_ref, a_ref, o_ref, *, chunk):
    wd = wd_ref[0].astype(BF16)
    for r0 in range(0, h_ref.shape[1], chunk):
        y = jnp.dot(h_ref[0, pl.ds(r0, chunk), :], wd, preferred_element_type=F32)
        o_ref[0, pl.ds(r0, chunk), :] = (y * a_ref[0, pl.ds(r0, chunk), :]).astype(o_ref.dtype)


def _expert_down(hidden, w_down, aff, *, tn):
    e, m, f = hidden.shape
    d = w_down.shape[2]
    return pl.pallas_call(
        functools.partial(_expert_down_kernel, chunk=512),
        grid=(e, d // tn),
        in_specs=[pl.BlockSpec((1, m, f), lambda i, j: (i, 0, 0)),
                  pl.BlockSpec((1, f, tn), lambda i, j: (i, 0, j)),
                  pl.BlockSpec((1, m, 1), lambda i, j: (i, 0, 0))],
        out_specs=pl.BlockSpec((1, m, tn), lambda i, j: (i, 0, j)),
        out_shape=jax.ShapeDtypeStruct((e, m, d), BF16),
        compiler_params=_cparams(2, 56),
        name="expert_down",
    )(hidden, w_down, aff)


def kernel(x, mem, norm_mix_g, norm_mem_g, w_in, pool_w, pool_scale, fourier_w, w_kv_mem, proj_pool, proj_fourier,
           proj_mem, w_out, norm_ffn_g, w_router, w_expert_gate, w_expert_up, w_expert_down, norm_final_g):
    b, s, d = x.shape
    depth = w_in.shape[0]
    n_tok = b * s
    pool_width = pool_w.shape[1] * pool_w.shape[2]
    four_width = fourier_w.shape[1] * fourier_w.shape[2]
    mem_width = proj_mem.shape[1]
    mix_width = pool_width + four_width + mem_width
    n_heads = 4
    e = w_router.shape[2]
    cap = CAPACITY_FACTOR * s // e

    xf = x.reshape(n_tok, d)
    for l in range(depth):
        memn = _rmsnorm(mem.reshape(-1, d), norm_mem_g[l], tm=512, out_dtype=BF16)
        kv = _matmul(memn, w_kv_mem[l], n_cols=2 * mem_width, tm=1024, tn=512, out_dtype=BF16, name="kv_proj")
        h, zp, zf, yc, w_gates = _in_proj(xf, norm_mix_g[l], w_in[l], kv.reshape(b, -1, 2 * mem_width),
                                          pool_width=pool_width, four_width=four_width, n_heads=n_heads, tm=512)
        ya = _pool_mixer(zp.reshape(b, s, pool_width), pool_w[l], pool_scale[l])
        yb = _fourier_mixer(zf.reshape(b, s, four_width), fourier_w[l], col_block=0, tk=1024)
        merged = _gated_merge(ya.reshape(n_tok, -1), yb.reshape(n_tok, -1), yc, h,
                              proj_pool[l], proj_fourier[l], proj_mem[l], w_gates, tm=1024, tn=512)

        x1, h2, aff_t = _out_proj(merged, w_out[l], xf, norm_ffn_g[l], w_router[l], batch=b, tm=512)
        slot_rows, slot_cols, starts, bounds = _select(aff_t.reshape(b * e, s), cap=cap, tile=COMBINE_TILE,
                                                       block=DISPATCH_BLOCK)
        xin, aff_slot = _dispatch(bounds[:, :2 * cap // DISPATCH_BLOCK], slot_rows.reshape(b, e, s), aff_t,
                                  h2.reshape(b, s, d), n_experts=e, cap=cap, block=DISPATCH_BLOCK,
                                  window=DISPATCH_WINDOW)
        hidden = _expert_up(xin, w_expert_gate[l], w_expert_up[l], tf=512)
        y = _expert_down(hidden, w_expert_down[l], aff_slot, tn=1024)
        last = l + 1 == depth
        xf = _combine(starts[:, :s // COMBINE_TILE + 1], slot_cols, y, x1, norm_final_g, batch=b, cap=cap,
                      tt=COMBINE_TILE, win=COMBINE_WINDOW, final_norm=last)
    return xf.reshape(b, s, d)
```

```python
import functools

import numpy as np
import jax
import jax.numpy as jnp
from jax import lax
from jax.experimental import pallas as pl
from jax.experimental.pallas import tpu as pltpu

F32 = jnp.float32
BF16 = jnp.bfloat16

EPS = 1e-6
POOL_WINDOWS = (2, 4, 8, 16)
CAPACITY_FACTOR = 2
MEM_HEADS = 4
LANES = 128
BF16_ROWS = 16
POOL_HALO = 32
ROW_CHUNK = 256
GATE_CAST_SLAB = 256
DISPATCH_BLOCK = 64
DISPATCH_WINDOW = 768
COMBINE_TILE = 256
COMBINE_WINDOW = 64
MIB = 1024 * 1024
ROW_TILE = 512
KV_TILE = (1024, 512)
MERGE_TILE = (1024, 512)
FOURIER_ROW_TILE = 1024
EXPERT_UP_COLS = 512
EXPERT_DOWN_COLS = 1024


def _cparams(n_axes, vmem_mib):
    return pltpu.CompilerParams(
        dimension_semantics=("arbitrary",) * n_axes,
        vmem_limit_bytes=vmem_mib * MIB,
    )


def _kv_proj_kernel(x_ref, g_ref, w_ref, o_ref, wc_ref):
    @pl.when(pl.program_id(1) == 0)
    def _():
        wc_ref[...] = w_ref[...].astype(BF16)

    x = x_ref[...]
    inv = lax.rsqrt(jnp.mean(x * x, axis=-1, keepdims=True) + EPS)
    hb = ((x * inv) * g_ref[...]).astype(BF16)
    o_ref[...] = jnp.dot(hb, wc_ref[...], preferred_element_type=F32).astype(o_ref.dtype)


def _kv_proj(x, g, w, *, tm, tn):
    m, k = x.shape
    n = w.shape[1]
    return pl.pallas_call(
        _kv_proj_kernel,
        grid=(n // tn, m // tm),
        in_specs=[pl.BlockSpec((tm, k), lambda c, i: (i, 0)), pl.BlockSpec((1, k), lambda c, i: (0, 0)),
                  pl.BlockSpec((k, tn), lambda c, i: (0, c))],
        out_specs=pl.BlockSpec((tm, tn), lambda c, i: (i, c)),
        out_shape=jax.ShapeDtypeStruct((m, n), BF16),
        scratch_shapes=[pltpu.VMEM((k, tn), BF16)],
        compiler_params=_cparams(2, 48),
        name="kv_proj",
    )(x, g.reshape(1, k), w)


def _in_proj_kernel(x_ref, g_ref, w_ref, kv_ref, wg_ref, h_ref, zp_ref, zf_ref, yc_ref, wgb_ref, wc_ref, *, n_heads):
    @pl.when(pl.program_id(0) == 0)
    def _():
        wc_ref[...] = w_ref[...].astype(BF16)

    wgb_ref[...] = wg_ref[...].astype(BF16)

    wp = zp_ref.shape[1]
    wf = zf_ref.shape[1]
    wm = yc_ref.shape[1]
    dh = wm // n_heads
    scale = float(dh ** -0.5)
    chunks = [pl.ds(r0, ROW_CHUNK) for r0 in range(0, x_ref.shape[0], ROW_CHUNK)]
    queries = []
    for rs in chunks:
        x = x_ref[rs, :]
        inv = lax.rsqrt(jnp.mean(x * x, axis=-1, keepdims=True) + EPS)
        hb = ((x * inv) * g_ref[...]).astype(BF16)
        h_ref[rs, :] = hb
        for c0 in range(0, wp, wf):
            zp_ref[rs, c0:c0 + wf] = jnp.dot(hb, wc_ref[:, c0:c0 + wf], preferred_element_type=F32)
        zf_ref[rs, :] = jnp.dot(hb, wc_ref[:, wp:wp + wf], preferred_element_type=F32).astype(BF16)
        queries.append(jnp.dot(hb, wc_ref[:, wp + wf:wp + wf + wm], preferred_element_type=F32).astype(BF16))
    pairs = [(ci, hd) for ci in range(len(chunks)) for hd in range(n_heads)]
    scores = [lax.dot_general(queries[ci][:, hd * dh:(hd + 1) * dh], kv_ref[0, :, hd * dh:(hd + 1) * dh],
                              (((1,), (1,)), ((), ())), preferred_element_type=F32) * scale for ci, hd in pairs]
    probs = []
    for sc in scores:
        p = jnp.exp(sc - jnp.max(sc, axis=-1, keepdims=True))
        probs.append((p / jnp.sum(p, axis=-1, keepdims=True)).astype(BF16))
    for (ci, hd), p in zip(pairs, probs):
        o = jnp.dot(p, kv_ref[0, :, wm + hd * dh:wm + (hd + 1) * dh], preferred_element_type=F32)
        yc_ref[chunks[ci], hd * dh:(hd + 1) * dh] = o.astype(BF16)


def _in_proj(x, g, w_in, kv3, *, pool_width, four_width, n_heads, tm):
    m, d = x.shape
    bsz, mlen, kvw = kv3.shape
    mem_width = kvw // 2
    mix_width = pool_width + four_width + mem_width
    tps = m // bsz // tm
    gate_width = w_in.shape[1] - mix_width
    slab = GATE_CAST_SLAB
    n_slabs = gate_width // slab
    assert gate_width % slab == 0 and mix_width % slab == 0 and n_slabs <= m // tm
    row = lambda width: pl.BlockSpec((tm, width), lambda i: (i, 0))
    return pl.pallas_call(
        functools.partial(_in_proj_kernel, n_heads=n_heads),
        grid=(m // tm,),
        in_specs=[row(d), pl.BlockSpec((1, d), lambda i: (0, 0)),
                  pl.BlockSpec((d, mix_width), lambda i: (0, 0), pipeline_mode=pl.Buffered(1)),
                  pl.BlockSpec((1, mlen, kvw), lambda i: (i // tps, 0, 0)),
                  pl.BlockSpec((d, slab), lambda i: (0, mix_width // slab + jnp.minimum(i, n_slabs - 1)))],
        out_specs=[row(d), row(pool_width), row(four_width), row(mem_width),
                   pl.BlockSpec((d, slab), lambda i: (0, jnp.minimum(i, n_slabs - 1)))],
        out_shape=[jax.ShapeDtypeStruct((m, d), BF16), jax.ShapeDtypeStruct((m, pool_width), F32),
                   jax.ShapeDtypeStruct((m, four_width), BF16), jax.ShapeDtypeStruct((m, mem_width), BF16),
                   jax.ShapeDtypeStruct((d, gate_width), BF16)],
        scratch_shapes=[pltpu.VMEM((d, mix_width), BF16)],
        compiler_params=_cparams(1, 56),
        name="in_proj",
    )(x, g.reshape(1, d), w_in, kv3, w_in)


def _out_proj_kernel(m_ref, w_ref, r_ref, g_ref, wr_ref, x1_ref, h2_ref, afft_ref, wc_ref, *, chunk, n_experts):
    @pl.when(pl.program_id(0) == 0)
    def _():
        wc_ref[...] = w_ref[...].astype(BF16)

    wr = wr_ref[...].astype(BF16)
    for r0 in range(0, m_ref.shape[0], ROW_CHUNK):
        rs = pl.ds(r0, ROW_CHUNK)
        a = m_ref[rs, :]
        for c0 in range(0, wc_ref.shape[1], chunk):
            cols = slice(c0, c0 + chunk)
            x1_ref[rs, cols] = r_ref[rs, cols] + jnp.dot(a, wc_ref[:, cols], preferred_element_type=F32)

        x = x1_ref[rs, :]
        inv = lax.rsqrt(jnp.mean(x * x, axis=-1, keepdims=True) + EPS)
        hb = ((x * inv) * g_ref[...]).astype(BF16)
        h2_ref[rs, :] = hb
        logits = jnp.dot(hb, wr, preferred_element_type=F32)
        lane = lax.broadcasted_iota(jnp.int32, logits.shape, 1)
        logits = jnp.where(lane < n_experts, logits, -1e30)
        mx = jnp.max(logits, axis=-1, keepdims=True)
        p = jnp.exp(logits - mx)
        aff = p / jnp.sum(p, axis=-1, keepdims=True)
        afft_ref[0, :, rs] = aff.T[:n_experts, :]


def _out_proj(merged, w_out, resid, g, w_router, *, batch, tm):
    m, d = resid.shape
    e = w_router.shape[1]
    s = m // batch
    tps = s // tm
    wr = jnp.pad(w_router, ((0, 0), (0, LANES - e)))
    row = pl.BlockSpec((tm, d), lambda i: (i, 0))
    return pl.pallas_call(
        functools.partial(_out_proj_kernel, chunk=512, n_experts=e),
        grid=(m // tm,),
        in_specs=[row, pl.BlockSpec((d, d), lambda i: (0, 0), pipeline_mode=pl.Buffered(1)), row,
                  pl.BlockSpec((1, d), lambda i: (0, 0)), pl.BlockSpec((d, LANES), lambda i: (0, 0))],
        out_specs=[row, row, pl.BlockSpec((1, e, tm), lambda i: (i // tps, 0, i % tps))],
        out_shape=[jax.ShapeDtypeStruct((m, d), F32), jax.ShapeDtypeStruct((m, d), BF16),
                   jax.ShapeDtypeStruct((batch, e, s), F32)],
        scratch_shapes=[pltpu.VMEM((d, d), BF16)],
        compiler_params=_cparams(1, 56),
        name="out_proj",
    )(merged, w_out, resid, g.reshape(1, d), wr)


def _pool_kernel(u_ref, pw_ref, ps_ref, o_ref, a_ref, b_ref, *, chunk):
    s = u_ref.shape[1]
    c = pw_ref.shape[1]
    h = POOL_HALO
    rows = s + 2 * h
    zeros = jnp.zeros((h, c), F32)

    def level(src, dst, off_lo, off_hi, margin):
        for r0 in range(margin, rows - margin, chunk):
            n = min(chunk, rows - margin - r0)
            dst[pl.ds(r0, n), :] = src[pl.ds(r0 + off_lo, n), :] + src[pl.ds(r0 + off_hi, n), :]

    for g, w in enumerate(POOL_WINDOWS):
        cols = slice(g * c, (g + 1) * c)
        a_ref[pl.ds(0, h), :] = zeros
        a_ref[pl.ds(h + s, h), :] = zeros
        a_ref[pl.ds(h, s), :] = u_ref[0, :, cols]
        src, off_lo, off_hi = a_ref, -1, 0
        if w >= 4:
            level(a_ref, b_ref, -1, 0, 8)
            src, off_lo, off_hi = b_ref, -1, 1
        if w >= 8:
            level(b_ref, a_ref, -1, 1, 16)
            src, off_lo, off_hi = a_ref, -2, 2
        if w >= 16:
            level(a_ref, b_ref, -2, 2, 24)
            src, off_lo, off_hi = b_ref, -4, 4
        wg = pw_ref[g].astype(BF16)
        scale = ps_ref[:, cols]
        for r0 in range(0, s, chunk):
            pos = r0 + lax.broadcasted_iota(jnp.int32, (chunk, 1), 0)
            lo = jnp.maximum(pos - w // 2, 0)
            hi = jnp.minimum(pos + (w - w // 2), s)
            cnt = (hi - lo).astype(F32)
            tot = src[pl.ds(h + r0 + off_lo, chunk), :] + src[pl.ds(h + r0 + off_hi, chunk), :]
            pooled = tot / cnt - u_ref[0, pl.ds(r0, chunk), cols]
            y = jnp.dot(pooled.astype(BF16), wg, preferred_element_type=F32)
            o_ref[0, pl.ds(r0, chunk), cols] = (y * scale).astype(o_ref.dtype)


def _pool_mixer(z3, pool_w, pool_scale):
    b, s, _ = z3.shape
    g, c, _ = pool_w.shape
    width = g * c
    assert POOL_WINDOWS == (2, 4, 8, 16) and g == len(POOL_WINDOWS)
    return pl.pallas_call(
        functools.partial(_pool_kernel, chunk=256),
        grid=(b,),
        in_specs=[pl.BlockSpec((1, s, width), lambda i: (i, 0, 0)),
                  pl.BlockSpec((g, c, c), lambda i: (0, 0, 0)),
                  pl.BlockSpec((1, width), lambda i: (0, 0))],
        out_specs=pl.BlockSpec((1, s, width), lambda i: (i, 0, 0)),
        out_shape=jax.ShapeDtypeStruct((b, s, width), BF16),
        scratch_shapes=[pltpu.VMEM((s + 2 * POOL_HALO, c), F32), pltpu.VMEM((s + 2 * POOL_HALO, c), F32)],
        compiler_params=_cparams(1, 48),
        name="pool_mixer",
    )(z3, pool_w, pool_scale.reshape(1, width))


def _dft_tables(s, c):
    def tab(n):
        r = np.outer(np.arange(n), np.arange(n)) % n
        ang = r * (2.0 * np.pi / n)
        return np.cos(ang), np.sin(ang)
    cs, ss = tab(s)
    cc, sc = tab(c)
    return (jnp.asarray(np.concatenate([cs, -ss], axis=1), dtype=F32).astype(BF16),
            jnp.asarray(np.concatenate([cc, sc], axis=1), dtype=F32).astype(BF16))


def _fourier_kernel(u_ref, dft_ref, ccsc_ref, fw_ref, o_ref, t_ref, *, norm):
    s = u_ref.shape[1]
    ng, c, _ = fw_ref.shape

    @pl.when(pl.program_id(1) == 0)
    def _():
        for g in range(ng):
            ug = u_ref[0, :, g * c:(g + 1) * c]
            ab = jnp.dot(ug, ccsc_ref[...], preferred_element_type=F32)
            t_ref[pl.ds(0, s), g * c:(g + 1) * c] = ab[:, :c].astype(BF16)
            t_ref[pl.ds(s, s), g * c:(g + 1) * c] = ab[:, c:].astype(BF16)

    f = jnp.dot(dft_ref[...], t_ref[...], preferred_element_type=F32) * norm
    for g in range(ng):
        y = jnp.dot(f[:, g * c:(g + 1) * c].astype(BF16), fw_ref[g].astype(BF16), preferred_element_type=F32)
        o_ref[0, :, g * c:(g + 1) * c] = y.astype(o_ref.dtype)


def _fourier_mixer(z3, fourier_w, *, tk):
    b, s, _ = z3.shape
    ng, c, _ = fourier_w.shape
    width = ng * c
    dft, ccsc = _dft_tables(s, c)
    return pl.pallas_call(
        functools.partial(_fourier_kernel, norm=float((s * c) ** -0.5)),
        grid=(b, s // tk),
        in_specs=[pl.BlockSpec((1, s, width), lambda i, k: (i, 0, 0)),
                  pl.BlockSpec((tk, 2 * s), lambda i, k: (k, 0)),
                  pl.BlockSpec((c, 2 * c), lambda i, k: (0, 0)),
                  pl.BlockSpec((ng, c, c), lambda i, k: (0, 0, 0))],
        out_specs=pl.BlockSpec((1, tk, width), lambda i, k: (i, k, 0)),
        out_shape=jax.ShapeDtypeStruct((b, s, width), BF16),
        scratch_shapes=[pltpu.VMEM((2 * s, width), BF16)],
        compiler_params=_cparams(2, 48),
        name="fourier_mixer",
    )(z3, dft, ccsc, fourier_w)


def _merge_kernel(ya_ref, yb_ref, yc_ref, h_ref, pp_ref, pf_ref, pm_ref, cg0, cg1, cg2, o_ref, cpp, cpf, cpm):
    @pl.when(pl.program_id(1) == 0)
    def _():
        for src, dst in ((pp_ref, cpp), (pf_ref, cpf), (pm_ref, cpm)):
            dst[...] = src[...].astype(BF16)

    h = h_ref[...]

    def branch(y_ref, proj, gate_w):
        gate = jax.nn.sigmoid(jnp.dot(h, gate_w[...], preferred_element_type=F32))
        return gate * jnp.dot(y_ref[...], proj[...], preferred_element_type=F32)

    acc = branch(ya_ref, cpp, cg0)
    acc = acc + branch(yb_ref, cpf, cg1)
    acc = acc + branch(yc_ref, cpm, cg2)
    o_ref[...] = acc.astype(o_ref.dtype)


def _gated_merge(ya, yb, yc, h, proj_pool, proj_fourier, proj_mem, w_gates, *, tm, tn):
    m, d = h.shape
    nb = d // tn
    act = lambda width: pl.BlockSpec((tm, width), lambda n, i: (i, 0))
    wsp = lambda rows, off: pl.BlockSpec((rows, tn), lambda n, i: (0, n + off))
    return pl.pallas_call(
        _merge_kernel,
        grid=(d // tn, m // tm),
        in_specs=[act(ya.shape[1]), act(yb.shape[1]), act(yc.shape[1]), act(d),
                  wsp(proj_pool.shape[0], 0), wsp(proj_fourier.shape[0], 0), wsp(proj_mem.shape[0], 0),
                  wsp(d, 0), wsp(d, nb), wsp(d, 2 * nb)],
        out_specs=pl.BlockSpec((tm, tn), lambda n, i: (i, n)),
        out_shape=jax.ShapeDtypeStruct((m, d), BF16),
        scratch_shapes=[pltpu.VMEM((proj_pool.shape[0], tn), BF16), pltpu.VMEM((proj_fourier.shape[0], tn), BF16),
                        pltpu.VMEM((proj_mem.shape[0], tn), BF16)],
        compiler_params=_cparams(2, 56),
        name="gated_merge",
    )(ya, yb, yc, h, proj_pool, proj_fourier, proj_mem, w_gates, w_gates, w_gates)


def _select_kernel(aff_ref, tri_ref, slot_ref, slot_t_ref, starts_ref, bounds_ref, *, cap, tile, block):
    a = aff_ref[...]
    rows = a.shape[0]
    capf = float(cap)

    def count(pred):
        return jnp.sum(pred.astype(F32), axis=-1, keepdims=True)

    def body(i, t_bits):
        cand = t_bits | jnp.left_shift(jnp.int32(1), 30 - i)
        return jnp.where(count(a >= pltpu.bitcast(cand, F32)) >= capf, cand, t_bits)

    t = pltpu.bitcast(lax.fori_loop(0, 31, body, jnp.zeros((rows, 1), jnp.int32)), F32)
    gt = a > t
    eq = a == t
    need = capf - count(gt)
    tri = tri_ref[...]
    eq_rank = jnp.dot(eq.astype(BF16), tri, preferred_element_type=F32)
    sel = gt | (eq & (eq_rank < need))
    pos = jnp.dot(sel.astype(BF16), tri, preferred_element_type=F32)
    slot = jnp.where(sel, pos, -1.0)
    slot_ref[...] = slot
    slot_t_ref[...] = slot.T
    s = a.shape[1]
    lane = lax.broadcasted_iota(jnp.int32, (rows, LANES), 1)
    starts = jnp.where(lane == s // tile, capf, 0.0)
    for t in range(s // tile):
        starts = jnp.where(lane == t, pos[:, t * tile:t * tile + 1], starts)
    starts_ref[...] = starts.astype(jnp.int32)

    nblk = cap // block
    tok = lax.broadcasted_iota(jnp.int32, (1, s), 1).astype(F32)

    def token_of(j):
        return jnp.sum(jnp.where(slot == float(j), tok, 0.0), axis=-1, keepdims=True)

    bounds = jnp.zeros((rows, LANES), F32)
    for i in range(nblk):
        bounds = jnp.where(lane == i, token_of(i * block), bounds)
        bounds = jnp.where(lane == nblk + i, token_of((i + 1) * block - 1) + 1.0, bounds)
    bounds_ref[...] = bounds.astype(jnp.int32)


def _select(aff_rows, *, cap, tile, block):
    rows, s = aff_rows.shape
    assert rows == LANES
    idx = jnp.arange(s, dtype=jnp.int32)
    tri = (idx[:, None] < idx[None, :]).astype(BF16)
    full = lambda shape: pl.BlockSpec(shape, lambda i: (0, 0))
    assert 2 * cap // block <= LANES and s // tile + 1 <= LANES
    return pl.pallas_call(
        functools.partial(_select_kernel, cap=cap, tile=tile, block=block),
        grid=(1,),
        in_specs=[full((rows, s)), full((s, s))],
        out_specs=[full((rows, s)), full((s, rows)), full((rows, LANES)), full((rows, LANES))],
        out_shape=[jax.ShapeDtypeStruct((rows, s), F32), jax.ShapeDtypeStruct((s, rows), F32),
                   jax.ShapeDtypeStruct((rows, LANES), jnp.int32), jax.ShapeDtypeStruct((rows, LANES), jnp.int32)],
        compiler_params=_cparams(1, 48),
        name="expert_select",
    )(aff_rows, tri)


def _dispatch_kernel(bounds_ref, slot_ref, aff_ref, h_ref, o_ref, oa_ref, *, window, group):
    bi = pl.program_id(0)
    ji = pl.program_id(1)
    nblk = pl.num_programs(1)
    ne, block, d = o_ref.shape
    s = h_ref.shape[1]

    lo = hi = None
    for k in range(ne):
        l = bounds_ref[bi * ne + k, ji]
        h = bounds_ref[bi * ne + k, nblk + ji]
        lo = l if lo is None else jnp.minimum(lo, l)
        hi = h if hi is None else jnp.maximum(hi, h)
    w0 = jnp.minimum(jnp.bitwise_and(lo, -BF16_ROWS), s - window)
    fits = hi - w0 <= window

    tok_all = lax.broadcasted_iota(jnp.int32, (1, s), 1).astype(F32)
    slot_ids = (ji * block + lax.broadcasted_iota(jnp.int32, (block, 1), 0)).astype(F32)

    def tokens_of(g0):
        cols = []
        for k in range(g0, g0 + group):
            hit = slot_ref[0, k:k + 1, :] == slot_ids
            cols.append(jnp.sum(jnp.where(hit, tok_all, 0.0), axis=-1, keepdims=True))
            oa_ref[k] = jnp.sum(jnp.where(hit, aff_ref[0, k:k + 1, :], 0.0), axis=-1, keepdims=True)
        return jnp.concatenate(cols, axis=0)

    def gather(width, first_tok, src):
        t = lax.broadcasted_iota(jnp.int32, (group * block, width), 1).astype(F32) + first_tok
        toks = tokens_of(0)
        for g0 in range(0, ne, group):
            onehot = (toks == t).astype(BF16)
            if g0 + group < ne:
                toks = tokens_of(g0 + group)
            out = jnp.dot(onehot, src, preferred_element_type=F32)
            o_ref[g0:g0 + group] = out.reshape(group, block, d).astype(o_ref.dtype)

    @pl.when(fits)
    def _():
        gather(window, w0.astype(F32), h_ref[0, pl.ds(pl.multiple_of(w0, BF16_ROWS), window), :])

    @pl.when(jnp.logical_not(fits))
    def _():
        gather(s, 0.0, h_ref[0])


def _dispatch(bounds, slot3, aff_t, h3, *, n_experts, cap, block, window):
    b, s, d = h3.shape
    assert window % BF16_ROWS == 0 and window <= s and cap % block == 0 and block % BF16_ROWS == 0
    nblk = cap // block
    grid_spec = pltpu.PrefetchScalarGridSpec(
        num_scalar_prefetch=1,
        grid=(b, nblk),
        in_specs=[pl.BlockSpec((1, n_experts, s), lambda i, j, bd: (i, 0, 0)),
                  pl.BlockSpec((1, n_experts, s), lambda i, j, bd: (i, 0, 0)),
                  pl.BlockSpec((1, s, d), lambda i, j, bd: (i, 0, 0))],
        out_specs=[pl.BlockSpec((n_experts, block, d), lambda i, j, bd: (0, i * nblk + j, 0)),
                   pl.BlockSpec((n_experts, block, 1), lambda i, j, bd: (0, i * nblk + j, 0))],
    )
    return pl.pallas_call(
        functools.partial(_dispatch_kernel, window=window, group=4),
        grid_spec=grid_spec,
        out_shape=[jax.ShapeDtypeStruct((n_experts, b * cap, d), BF16),
                   jax.ShapeDtypeStruct((n_experts, b * cap, 1), F32)],
        compiler_params=_cparams(2, 48),
        name="dispatch",
    )(bounds, slot3, aff_t, h3)


def _combine_kernel(starts_ref, slot_ref, y_ref, x_ref, g_ref, o_ref, *, win, final_norm):
    bi = pl.program_id(0)
    ti = pl.program_id(1)
    tt, d = x_ref.shape
    ne, cap, _ = y_ref.shape

    wins = []
    fits = None
    for k in range(ne):
        lo = starts_ref[bi * ne + k, ti]
        hi = starts_ref[bi * ne + k, ti + 1]
        w0 = jnp.minimum(jnp.bitwise_and(lo, -BF16_ROWS), cap - win)
        ok = hi - w0 <= win
        fits = ok if fits is None else jnp.logical_and(fits, ok)
        wins.append(w0)

    slots = slot_ref[...].astype(BF16)

    def onehot(width, first_slot):
        shift = width.bit_length() - 1
        n = ne * width
        lane_of = bi * ne + lax.shift_right_logical(lax.broadcasted_iota(jnp.int32, (LANES, n), 1), shift)
        spread = (lax.broadcasted_iota(jnp.int32, (LANES, n), 0) == lane_of).astype(BF16)
        slot_b = jnp.dot(slots, spread, preferred_element_type=F32)
        c = lax.broadcasted_iota(jnp.int32, (1, n), 1)
        target = jnp.bitwise_and(c, width - 1)
        if first_slot is not None:
            kk = lax.shift_right_logical(c, shift)
            for k in range(ne):
                target = target + jnp.where(kk == k, first_slot[k], 0)
        return (slot_b == target.astype(F32)).astype(BF16)

    def finish(contrib):
        x = x_ref[...] + contrib
        if final_norm:
            inv = lax.rsqrt(jnp.mean(x * x, axis=-1, keepdims=True) + EPS)
            x = (x * inv) * g_ref[...]
        o_ref[...] = x

    @pl.when(fits)
    def _():
        ywin = jnp.concatenate([y_ref[k, pl.ds(pl.multiple_of(wins[k], BF16_ROWS), win), :] for k in range(ne)],
                               axis=0)
        finish(jnp.dot(onehot(win, wins), ywin, preferred_element_type=F32))

    @pl.when(jnp.logical_not(fits))
    def _():
        finish(jnp.dot(onehot(cap, None), y_ref[...].reshape(ne * cap, d), preferred_element_type=F32))


def _combine(starts, slot_cols, y, x1, g, *, batch, cap, tt, win, final_norm):
    m, d = x1.shape
    ne = y.shape[0]
    s = m // batch
    tps = s // tt
    assert slot_cols.shape == (s, LANES) and batch * ne == LANES
    assert win % BF16_ROWS == 0 and win <= cap and win & (win - 1) == 0 and cap & (cap - 1) == 0
    grid_spec = pltpu.PrefetchScalarGridSpec(
        num_scalar_prefetch=1,
        grid=(batch, tps),
        in_specs=[pl.BlockSpec((tt, LANES), lambda b, t, st: (t, 0)),
                  pl.BlockSpec((ne, cap, d), lambda b, t, st: (0, b, 0)),
                  pl.BlockSpec((tt, d), lambda b, t, st: (b * tps + t, 0)),
                  pl.BlockSpec((1, d), lambda b, t, st: (0, 0))],
        out_specs=pl.BlockSpec((tt, d), lambda b, t, st: (b * tps + t, 0)),
    )
    return pl.pallas_call(
        functools.partial(_combine_kernel, win=win, final_norm=final_norm),
        grid_spec=grid_spec,
        out_shape=jax.ShapeDtypeStruct((m, d), F32),
        compiler_params=_cparams(2, 56),
        name="combine",
    )(starts, slot_cols, y, x1, g.reshape(1, d))


def _expert_up_kernel(x_ref, wg_ref, wu_ref, o_ref, *, chunk):
    wg = wg_ref[0].astype(BF16)
    wu = wu_ref[0].astype(BF16)
    for r0 in range(0, x_ref.shape[1], chunk):
        x = x_ref[0, pl.ds(r0, chunk), :]
        gate = jnp.dot(x, wg, preferred_element_type=F32)
        up = jnp.dot(x, wu, preferred_element_type=F32)
        o_ref[0, pl.ds(r0, chunk), :] = (jax.nn.silu(gate) * up).astype(o_ref.dtype)


def _expert_up(xin, w_gate, w_up, *, tf):
    e, m, d = xin.shape
    f = w_gate.shape[2]
    return pl.pallas_call(
        functools.partial(_expert_up_kernel, chunk=512),
        grid=(e, f // tf),
        in_specs=[pl.BlockSpec((1, m, d), lambda i, j: (i, 0, 0)),
                  pl.BlockSpec((1, d, tf), lambda i, j: (i, 0, j)),
                  pl.BlockSpec((1, d, tf), lambda i, j: (i, 0, j))],
        out_specs=pl.BlockSpec((1, m, tf), lambda i, j: (i, 0, j)),
        out_shape=jax.ShapeDtypeStruct((e, m, f), BF16),
        compiler_params=_cparams(2, 56),
        name="expert_up",
    )(xin, w_gate, w_up)


def _expert_down_kernel(h_ref, wd_ref, a_ref, o_ref, *, chunk):
    wd = wd_ref[0].astype(BF16)
    for r0 in range(0, h_ref.shape[1], chunk):
        y = jnp.dot(h_ref[0, pl.ds(r0, chunk), :], wd, preferred_element_type=F32)
        o_ref[0, pl.ds(r0, chunk), :] = (y * a_ref[0, pl.ds(r0, chunk), :]).astype(o_ref.dtype)


def _expert_down(hidden, w_down, aff, *, tn):
    e, m, f = hidden.shape
    d = w_down.shape[2]
    return pl.pallas_call(
        functools.partial(_expert_down_kernel, chunk=512),
        grid=(e, d // tn),
        in_specs=[pl.BlockSpec((1, m, f), lambda i, j: (i, 0, 0)),
                  pl.BlockSpec((1, f, tn), lambda i, j: (i, 0, j)),
                  pl.BlockSpec((1, m, 1), lambda i, j: (i, 0, 0))],
        out_specs=pl.BlockSpec((1, m, tn), lambda i, j: (i, 0, j)),
        out_shape=jax.ShapeDtypeStruct((e, m, d), BF16),
        compiler_params=_cparams(2, 56),
        name="expert_down",
    )(hidden, w_down, aff)


def kernel(x, mem, norm_mix_g, norm_mem_g, w_in, pool_w, pool_scale, fourier_w, w_kv_mem, proj_pool, proj_fourier,
           proj_mem, w_out, norm_ffn_g, w_router, w_expert_gate, w_expert_up, w_expert_down, norm_final_g):
    b, s, d = x.shape
    depth = w_in.shape[0]
    n_tok = b * s
    pool_width = pool_w.shape[1] * pool_w.shape[2]
    four_width = fourier_w.shape[1] * fourier_w.shape[2]
    mem_width = proj_mem.shape[1]
    mix_width = pool_width + four_width + mem_width
    e = w_router.shape[2]
    cap = CAPACITY_FACTOR * s // e

    xf = x.reshape(n_tok, d)
    for l in range(depth):
        kv = _kv_proj(mem.reshape(-1, d), norm_mem_g[l], w_kv_mem[l], tm=KV_TILE[0], tn=KV_TILE[1])
        h, zp, zf, yc, w_gates = _in_proj(xf, norm_mix_g[l], w_in[l], kv.reshape(b, -1, 2 * mem_width),
                                          pool_width=pool_width, four_width=four_width, n_heads=MEM_HEADS,
                                          tm=ROW_TILE)
        ya = _pool_mixer(zp.reshape(b, s, pool_width), pool_w[l], pool_scale[l])
        yb = _fourier_mixer(zf.reshape(b, s, four_width), fourier_w[l], tk=FOURIER_ROW_TILE)
        merged = _gated_merge(ya.reshape(n_tok, -1), yb.reshape(n_tok, -1), yc, h,
                              proj_pool[l], proj_fourier[l], proj_mem[l], w_gates, tm=MERGE_TILE[0], tn=MERGE_TILE[1])

        x1, h2, aff_t = _out_proj(merged, w_out[l], xf, norm_ffn_g[l], w_router[l], batch=b, tm=ROW_TILE)
        slot_rows, slot_cols, starts, bounds = _select(aff_t.reshape(b * e, s), cap=cap, tile=COMBINE_TILE,
                                                       block=DISPATCH_BLOCK)
        xin, aff_slot = _dispatch(bounds[:, :2 * cap // DISPATCH_BLOCK], slot_rows.reshape(b, e, s), aff_t,
                                  h2.reshape(b, s, d), n_experts=e, cap=cap, block=DISPATCH_BLOCK,
                                  window=DISPATCH_WINDOW)
        hidden = _expert_up(xin, w_expert_gate[l], w_expert_up[l], tf=EXPERT_UP_COLS)
        y = _expert_down(hidden, w_expert_down[l], aff_slot, tn=EXPERT_DOWN_COLS)
        last = l + 1 == depth
        xf = _combine(starts[:, :s // COMBINE_TILE + 1], slot_cols, y, x1, norm_final_g, batch=b, cap=cap,
                      tt=COMBINE_TILE, win=COMBINE_WINDOW, final_norm=last)
    return xf.reshape(b, s, d)
```

```python
import functools

import numpy as np
import jax
import jax.numpy as jnp
from jax import lax
from jax.experimental import pallas as pl
from jax.experimental.pallas import tpu as pltpu

F32 = jnp.float32
BF16 = jnp.bfloat16

EPS = 1e-6
POOL_WINDOWS = (2, 4, 8, 16)
CAPACITY_FACTOR = 2
MEM_HEADS = 4
LANES = 128
BF16_ROWS = 16
POOL_HALO = 32
ROW_CHUNK = 256
MERGE_ROW_CHUNK = 512
GATE_CAST_SLAB = 256
DISPATCH_BLOCK = 64
DISPATCH_WINDOW = 768
DISPATCH_GROUP = 2
COMBINE_TILE = 256
COMBINE_WINDOW = 64
MIB = 1024 * 1024
ROW_TILE = 512
KV_TILE = (1024, 512)
MERGE_TILE = (1024, 512)
FOURIER_ROW_TILE = 1024
EXPERT_UP_COLS = 512
EXPERT_DOWN_COLS = 1024


def _cparams(n_axes, vmem_mib):
    return pltpu.CompilerParams(
        dimension_semantics=("arbitrary",) * n_axes,
        vmem_limit_bytes=vmem_mib * MIB,
    )


def _kv_proj_kernel(x_ref, g_ref, w_ref, o_ref, wc_ref):
    @pl.when(pl.program_id(1) == 0)
    def _():
        wc_ref[...] = w_ref[...].astype(BF16)

    x = x_ref[...]
    inv = lax.rsqrt(jnp.mean(x * x, axis=-1, keepdims=True) + EPS)
    hb = ((x * inv) * g_ref[...]).astype(BF16)
    o_ref[...] = jnp.dot(hb, wc_ref[...], preferred_element_type=F32).astype(o_ref.dtype)


def _kv_proj(x, g, w, *, tm, tn):
    m, k = x.shape
    n = w.shape[1]
    return pl.pallas_call(
        _kv_proj_kernel,
        grid=(n // tn, m // tm),
        in_specs=[pl.BlockSpec((tm, k), lambda c, i: (i, 0)), pl.BlockSpec((1, k), lambda c, i: (0, 0)),
                  pl.BlockSpec((k, tn), lambda c, i: (0, c))],
        out_specs=pl.BlockSpec((tm, tn), lambda c, i: (i, c)),
        out_shape=jax.ShapeDtypeStruct((m, n), BF16),
        scratch_shapes=[pltpu.VMEM((k, tn), BF16)],
        compiler_params=_cparams(2, 48),
        name="kv_proj",
    )(x, g.reshape(1, k), w)


def _in_proj_kernel(x_ref, g_ref, w_ref, kv_ref, wg_ref, h_ref, zp_ref, zf_ref, yc_ref, wgb_ref, wc_ref, *, n_heads):
    @pl.when(pl.program_id(0) == 0)
    def _():
        wc_ref[...] = w_ref[...].astype(BF16)

    wgb_ref[...] = wg_ref[...].astype(BF16)

    wp = zp_ref.shape[1]
    wf = zf_ref.shape[1]
    wm = yc_ref.shape[1]
    dh = wm // n_heads
    scale = float(dh ** -0.5)
    chunks = [pl.ds(r0, ROW_CHUNK) for r0 in range(0, x_ref.shape[0], ROW_CHUNK)]
    queries = []
    for rs in chunks:
        x = x_ref[rs, :]
        inv = lax.rsqrt(jnp.mean(x * x, axis=-1, keepdims=True) + EPS)
        hb = ((x * inv) * g_ref[...]).astype(BF16)
        h_ref[rs, :] = hb
        for c0 in range(0, wp, wf):
            zp_ref[rs, c0:c0 + wf] = jnp.dot(hb, wc_ref[:, c0:c0 + wf], preferred_element_type=F32)
        zf_ref[rs, :] = jnp.dot(hb, wc_ref[:, wp:wp + wf], preferred_element_type=F32).astype(BF16)
        queries.append(jnp.dot(hb, wc_ref[:, wp + wf:wp + wf + wm], preferred_element_type=F32).astype(BF16))
    pairs = [(ci, hd) for ci in range(len(chunks)) for hd in range(n_heads)]
    scores = [lax.dot_general(queries[ci][:, hd * dh:(hd + 1) * dh], kv_ref[0, :, hd * dh:(hd + 1) * dh],
                              (((1,), (1,)), ((), ())), preferred_element_type=F32) * scale for ci, hd in pairs]
    probs = []
    for sc in scores:
        p = jnp.exp(sc - jnp.max(sc, axis=-1, keepdims=True))
        probs.append((p / jnp.sum(p, axis=-1, keepdims=True)).astype(BF16))
    for (ci, hd), p in zip(pairs, probs):
        o = jnp.dot(p, kv_ref[0, :, wm + hd * dh:wm + (hd + 1) * dh], preferred_element_type=F32)
        yc_ref[chunks[ci], hd * dh:(hd + 1) * dh] = o.astype(BF16)


def _in_proj(x, g, w_in, kv3, *, pool_width, four_width, n_heads, tm):
    m, d = x.shape
    bsz, mlen, kvw = kv3.shape
    mem_width = kvw // 2
    mix_width = pool_width + four_width + mem_width
    tps = m // bsz // tm
    gate_width = w_in.shape[1] - mix_width
    slab = GATE_CAST_SLAB
    n_slabs = gate_width // slab
    assert gate_width % slab == 0 and mix_width % slab == 0 and n_slabs <= m // tm
    row = lambda width: pl.BlockSpec((tm, width), lambda i: (i, 0))
    return pl.pallas_call(
        functools.partial(_in_proj_kernel, n_heads=n_heads),
        grid=(m // tm,),
        in_specs=[row(d), pl.BlockSpec((1, d), lambda i: (0, 0)),
                  pl.BlockSpec((d, mix_width), lambda i: (0, 0), pipeline_mode=pl.Buffered(1)),
                  pl.BlockSpec((1, mlen, kvw), lambda i: (i // tps, 0, 0)),
                  pl.BlockSpec((d, slab), lambda i: (0, mix_width // slab + jnp.minimum(i, n_slabs - 1)))],
        out_specs=[row(d), row(pool_width), row(four_width), row(mem_width),
                   pl.BlockSpec((d, slab), lambda i: (0, jnp.minimum(i, n_slabs - 1)))],
        out_shape=[jax.ShapeDtypeStruct((m, d), BF16), jax.ShapeDtypeStruct((m, pool_width), F32),
                   jax.ShapeDtypeStruct((m, four_width), BF16), jax.ShapeDtypeStruct((m, mem_width), BF16),
                   jax.ShapeDtypeStruct((d, gate_width), BF16)],
        scratch_shapes=[pltpu.VMEM((d, mix_width), BF16)],
        compiler_params=_cparams(1, 56),
        name="in_proj",
    )(x, g.reshape(1, d), w_in, kv3, w_in)


def _out_proj_kernel(m_ref, w_ref, r_ref, g_ref, wr_ref, x1_ref, h2_ref, afft_ref, wc_ref, *, chunk, n_experts):
    @pl.when(pl.program_id(0) == 0)
    def _():
        wc_ref[...] = w_ref[...].astype(BF16)

    wr = wr_ref[...].astype(BF16)
    for r0 in range(0, m_ref.shape[0], ROW_CHUNK):
        rs = pl.ds(r0, ROW_CHUNK)
        a = m_ref[rs, :]
        for c0 in range(0, wc_ref.shape[1], chunk):
            cols = slice(c0, c0 + chunk)
            x1_ref[rs, cols] = r_ref[rs, cols] + jnp.dot(a, wc_ref[:, cols], preferred_element_type=F32)

        x = x1_ref[rs, :]
        inv = lax.rsqrt(jnp.mean(x * x, axis=-1, keepdims=True) + EPS)
        hb = ((x * inv) * g_ref[...]).astype(BF16)
        h2_ref[rs, :] = hb
        logits = jnp.dot(hb, wr, preferred_element_type=F32)
        lane = lax.broadcasted_iota(jnp.int32, logits.shape, 1)
        logits = jnp.where(lane < n_experts, logits, -1e30)
        mx = jnp.max(logits, axis=-1, keepdims=True)
        p = jnp.exp(logits - mx)
        aff = p / jnp.sum(p, axis=-1, keepdims=True)
        afft_ref[0, :, rs] = aff.T[:n_experts, :]


def _out_proj(merged, w_out, resid, g, w_router, *, batch, tm):
    m, d = resid.shape
    e = w_router.shape[1]
    s = m // batch
    tps = s // tm
    wr = jnp.pad(w_router, ((0, 0), (0, LANES - e)))
    row = pl.BlockSpec((tm, d), lambda i: (i, 0))
    return pl.pallas_call(
        functools.partial(_out_proj_kernel, chunk=512, n_experts=e),
        grid=(m // tm,),
        in_specs=[row, pl.BlockSpec((d, d), lambda i: (0, 0), pipeline_mode=pl.Buffered(1)), row,
                  pl.BlockSpec((1, d), lambda i: (0, 0)), pl.BlockSpec((d, LANES), lambda i: (0, 0))],
        out_specs=[row, row, pl.BlockSpec((1, e, tm), lambda i: (i // tps, 0, i % tps))],
        out_shape=[jax.ShapeDtypeStruct((m, d), F32), jax.ShapeDtypeStruct((m, d), BF16),
                   jax.ShapeDtypeStruct((batch, e, s), F32)],
        scratch_shapes=[pltpu.VMEM((d, d), BF16)],
        compiler_params=_cparams(1, 56),
        name="out_proj",
    )(merged, w_out, resid, g.reshape(1, d), wr)


def _pool_kernel(u_ref, pw_ref, ps_ref, o_ref, a_ref, b_ref, *, chunk):
    s = u_ref.shape[1]
    c = pw_ref.shape[1]
    h = POOL_HALO
    rows = s + 2 * h
    zeros = jnp.zeros((h, c), F32)

    def level(src, dst, off_lo, off_hi, margin):
        for r0 in range(margin, rows - margin, chunk):
            n = min(chunk, rows - margin - r0)
            dst[pl.ds(r0, n), :] = src[pl.ds(r0 + off_lo, n), :] + src[pl.ds(r0 + off_hi, n), :]

    for g, w in enumerate(POOL_WINDOWS):
        cols = slice(g * c, (g + 1) * c)
        a_ref[pl.ds(0, h), :] = zeros
        a_ref[pl.ds(h + s, h), :] = zeros
        a_ref[pl.ds(h, s), :] = u_ref[0, :, cols]
        src, off_lo, off_hi = a_ref, -1, 0
        if w >= 4:
            level(a_ref, b_ref, -1, 0, 8)
            src, off_lo, off_hi = b_ref, -1, 1
        if w >= 8:
            level(b_ref, a_ref, -1, 1, 16)
            src, off_lo, off_hi = a_ref, -2, 2
        if w >= 16:
            level(a_ref, b_ref, -2, 2, 24)
            src, off_lo, off_hi = b_ref, -4, 4
        wg = pw_ref[g].astype(BF16)
        scale = ps_ref[:, cols]
        for r0 in range(0, s, chunk):
            pos = r0 + lax.broadcasted_iota(jnp.int32, (chunk, 1), 0)
            lo = jnp.maximum(pos - w // 2, 0)
            hi = jnp.minimum(pos + (w - w // 2), s)
            cnt = (hi - lo).astype(F32)
            tot = src[pl.ds(h + r0 + off_lo, chunk), :] + src[pl.ds(h + r0 + off_hi, chunk), :]
            pooled = tot / cnt - u_ref[0, pl.ds(r0, chunk), cols]
            y = jnp.dot(pooled.astype(BF16), wg, preferred_element_type=F32)
            o_ref[0, pl.ds(r0, chunk), cols] = (y * scale).astype(o_ref.dtype)


def _pool_mixer(z3, pool_w, pool_scale):
    b, s, _ = z3.shape
    g, c, _ = pool_w.shape
    width = g * c
    assert POOL_WINDOWS == (2, 4, 8, 16) and g == len(POOL_WINDOWS)
    return pl.pallas_call(
        functools.partial(_pool_kernel, chunk=256),
        grid=(b,),
        in_specs=[pl.BlockSpec((1, s, width), lambda i: (i, 0, 0)),
                  pl.BlockSpec((g, c, c), lambda i: (0, 0, 0)),
                  pl.BlockSpec((1, width), lambda i: (0, 0))],
        out_specs=pl.BlockSpec((1, s, width), lambda i: (i, 0, 0)),
        out_shape=jax.ShapeDtypeStruct((b, s, width), BF16),
        scratch_shapes=[pltpu.VMEM((s + 2 * POOL_HALO, c), F32), pltpu.VMEM((s + 2 * POOL_HALO, c), F32)],
        compiler_params=_cparams(1, 48),
        name="pool_mixer",
    )(z3, pool_w, pool_scale.reshape(1, width))


def _dft_tables(s, c):
    def tab(n):
        r = np.outer(np.arange(n), np.arange(n)) % n
        ang = r * (2.0 * np.pi / n)
        return np.cos(ang), np.sin(ang)
    cs, ss = tab(s)
    cc, sc = tab(c)
    return (jnp.asarray(np.concatenate([cs, -ss], axis=1), dtype=F32).astype(BF16),
            jnp.asarray(np.concatenate([cc, sc], axis=1), dtype=F32).astype(BF16))


def _fourier_kernel(u_ref, dft_ref, ccsc_ref, fw_ref, o_ref, t_ref, *, norm):
    s = u_ref.shape[1]
    ng, c, _ = fw_ref.shape

    @pl.when(pl.program_id(1) == 0)
    def _():
        for g in range(ng):
            ug = u_ref[0, :, g * c:(g + 1) * c]
            ab = jnp.dot(ug, ccsc_ref[...], preferred_element_type=F32)
            t_ref[pl.ds(0, s), g * c:(g + 1) * c] = ab[:, :c].astype(BF16)
            t_ref[pl.ds(s, s), g * c:(g + 1) * c] = ab[:, c:].astype(BF16)

    f = jnp.dot(dft_ref[...], t_ref[...], preferred_element_type=F32) * norm
    for g in range(ng):
        y = jnp.dot(f[:, g * c:(g + 1) * c].astype(BF16), fw_ref[g].astype(BF16), preferred_element_type=F32)
        o_ref[0, :, g * c:(g + 1) * c] = y.astype(o_ref.dtype)


def _fourier_mixer(z3, fourier_w, *, tk):
    b, s, _ = z3.shape
    ng, c, _ = fourier_w.shape
    width = ng * c
    dft, ccsc = _dft_tables(s, c)
    return pl.pallas_call(
        functools.partial(_fourier_kernel, norm=float((s * c) ** -0.5)),
        grid=(b, s // tk),
        in_specs=[pl.BlockSpec((1, s, width), lambda i, k: (i, 0, 0)),
                  pl.BlockSpec((tk, 2 * s), lambda i, k: (k, 0)),
                  pl.BlockSpec((c, 2 * c), lambda i, k: (0, 0)),
                  pl.BlockSpec((ng, c, c), lambda i, k: (0, 0, 0))],
        out_specs=pl.BlockSpec((1, tk, width), lambda i, k: (i, k, 0)),
        out_shape=jax.ShapeDtypeStruct((b, s, width), BF16),
        scratch_shapes=[pltpu.VMEM((2 * s, width), BF16)],
        compiler_params=_cparams(2, 48),
        name="fourier_mixer",
    )(z3, dft, ccsc, fourier_w)


def _merge_kernel(ya_ref, yb_ref, yc_ref, h_ref, pp_ref, pf_ref, pm_ref, cg0, cg1, cg2, o_ref, cpp, cpf, cpm):
    @pl.when(pl.program_id(1) == 0)
    def _():
        for src, dst in ((pp_ref, cpp), (pf_ref, cpf), (pm_ref, cpm)):
            dst[...] = src[...].astype(BF16)

    for r0 in range(0, h_ref.shape[0], MERGE_ROW_CHUNK):
        rs = pl.ds(r0, MERGE_ROW_CHUNK)
        h = h_ref[rs, :]

        def branch(y_ref, proj, gate_w):
            gate = jax.nn.sigmoid(jnp.dot(h, gate_w[...], preferred_element_type=F32))
            return gate * jnp.dot(y_ref[rs, :], proj[...], preferred_element_type=F32)

        acc = branch(ya_ref, cpp, cg0)
        acc = acc + branch(yb_ref, cpf, cg1)
        acc = acc + branch(yc_ref, cpm, cg2)
        o_ref[rs, :] = acc.astype(o_ref.dtype)


def _gated_merge(ya, yb, yc, h, proj_pool, proj_fourier, proj_mem, w_gates, *, tm, tn):
    m, d = h.shape
    nb = d // tn
    act = lambda width: pl.BlockSpec((tm, width), lambda n, i: (i, 0))
    wsp = lambda rows, off: pl.BlockSpec((rows, tn), lambda n, i: (0, n + off))
    return pl.pallas_call(
        _merge_kernel,
        grid=(d // tn, m // tm),
        in_specs=[act(ya.shape[1]), act(yb.shape[1]), act(yc.shape[1]), act(d),
                  wsp(proj_pool.shape[0], 0), wsp(proj_fourier.shape[0], 0), wsp(proj_mem.shape[0], 0),
                  wsp(d, 0), wsp(d, nb), wsp(d, 2 * nb)],
        out_specs=pl.BlockSpec((tm, tn), lambda n, i: (i, n)),
        out_shape=jax.ShapeDtypeStruct((m, d), BF16),
        scratch_shapes=[pltpu.VMEM((proj_pool.shape[0], tn), BF16), pltpu.VMEM((proj_fourier.shape[0], tn), BF16),
                        pltpu.VMEM((proj_mem.shape[0], tn), BF16)],
        compiler_params=_cparams(2, 56),
        name="gated_merge",
    )(ya, yb, yc, h, proj_pool, proj_fourier, proj_mem, w_gates, w_gates, w_gates)


def _select_kernel(aff_ref, tri_ref, slot_ref, slot_t_ref, starts_ref, bounds_ref, *, cap, tile, block):
    a = aff_ref[...]
    rows = a.shape[0]
    capf = float(cap)

    def count(pred):
        return jnp.sum(pred.astype(F32), axis=-1, keepdims=True)

    def body(i, t_bits):
        cand = t_bits | jnp.left_shift(jnp.int32(1), 30 - i)
        return jnp.where(count(a >= pltpu.bitcast(cand, F32)) >= capf, cand, t_bits)

    t = pltpu.bitcast(lax.fori_loop(0, 31, body, jnp.zeros((rows, 1), jnp.int32)), F32)
    gt = a > t
    eq = a == t
    need = capf - count(gt)
    tri = tri_ref[...]
    eq_rank = jnp.dot(eq.astype(BF16), tri, preferred_element_type=F32)
    sel = gt | (eq & (eq_rank < need))
    pos = jnp.dot(sel.astype(BF16), tri, preferred_element_type=F32)
    slot = jnp.where(sel, pos, -1.0)
    slot_ref[...] = slot
    slot_t_ref[...] = slot.T
    s = a.shape[1]
    lane = lax.broadcasted_iota(jnp.int32, (rows, LANES), 1)
    starts = jnp.where(lane == s // tile, capf, 0.0)
    for t in range(s // tile):
        starts = jnp.where(lane == t, pos[:, t * tile:t * tile + 1], starts)
    starts_ref[...] = starts.astype(jnp.int32)

    nblk = cap // block
    tok = lax.broadcasted_iota(jnp.int32, (1, s), 1).astype(F32)

    def token_of(j):
        return jnp.sum(jnp.where(slot == float(j), tok, 0.0), axis=-1, keepdims=True)

    bounds = jnp.zeros((rows, LANES), F32)
    for i in range(nblk):
        bounds = jnp.where(lane == i, token_of(i * block), bounds)
        bounds = jnp.where(lane == nblk + i, token_of((i + 1) * block - 1) + 1.0, bounds)
    bounds_ref[...] = bounds.astype(jnp.int32)


def _select(aff_rows, *, cap, tile, block):
    rows, s = aff_rows.shape
    assert rows == LANES
    idx = jnp.arange(s, dtype=jnp.int32)
    tri = (idx[:, None] < idx[None, :]).astype(BF16)
    full = lambda shape: pl.BlockSpec(shape, lambda i: (0, 0))
    assert 2 * cap // block <= LANES and s // tile + 1 <= LANES
    return pl.pallas_call(
        functools.partial(_select_kernel, cap=cap, tile=tile, block=block),
        grid=(1,),
        in_specs=[full((rows, s)), full((s, s))],
        out_specs=[full((rows, s)), full((s, rows)), full((rows, LANES)), full((rows, LANES))],
        out_shape=[jax.ShapeDtypeStruct((rows, s), F32), jax.ShapeDtypeStruct((s, rows), F32),
                   jax.ShapeDtypeStruct((rows, LANES), jnp.int32), jax.ShapeDtypeStruct((rows, LANES), jnp.int32)],
        compiler_params=_cparams(1, 48),
        name="expert_select",
    )(aff_rows, tri)


def _dispatch_kernel(bounds_ref, slot_ref, aff_ref, h_ref, o_ref, oa_ref, *, window, group):
    bi = pl.program_id(0)
    ji = pl.program_id(1)
    nblk = pl.num_programs(1)
    ne, block, d = o_ref.shape
    s = h_ref.shape[1]

    lo = hi = None
    for k in range(ne):
        l = bounds_ref[bi * ne + k, ji]
        h = bounds_ref[bi * ne + k, nblk + ji]
        lo = l if lo is None else jnp.minimum(lo, l)
        hi = h if hi is None else jnp.maximum(hi, h)
    w0 = jnp.minimum(jnp.bitwise_and(lo, -BF16_ROWS), s - window)
    fits = hi - w0 <= window

    tok_all = lax.broadcasted_iota(jnp.int32, (1, s), 1).astype(F32)
    slot_ids = (ji * block + lax.broadcasted_iota(jnp.int32, (block, 1), 0)).astype(F32)

    def tokens_of(g0):
        cols = []
        for k in range(g0, g0 + group):
            hit = slot_ref[0, k:k + 1, :] == slot_ids
            cols.append(jnp.sum(jnp.where(hit, tok_all, 0.0), axis=-1, keepdims=True))
            oa_ref[k] = jnp.sum(jnp.where(hit, aff_ref[0, k:k + 1, :], 0.0), axis=-1, keepdims=True)
        return jnp.concatenate(cols, axis=0)

    def gather(width, first_tok, src):
        t = lax.broadcasted_iota(jnp.int32, (group * block, width), 1).astype(F32) + first_tok
        toks = tokens_of(0)
        for g0 in range(0, ne, group):
            onehot = (toks == t).astype(BF16)
            if g0 + group < ne:
                toks = tokens_of(g0 + group)
            out = jnp.dot(onehot, src, preferred_element_type=F32)
            o_ref[g0:g0 + group] = out.reshape(group, block, d).astype(o_ref.dtype)

    @pl.when(fits)
    def _():
        gather(window, w0.astype(F32), h_ref[0, pl.ds(pl.multiple_of(w0, BF16_ROWS), window), :])

    @pl.when(jnp.logical_not(fits))
    def _():
        gather(s, 0.0, h_ref[0])


def _dispatch(bounds, slot3, aff_t, h3, *, n_experts, cap, block, window):
    b, s, d = h3.shape
    assert window % BF16_ROWS == 0 and window <= s and cap % block == 0 and block % BF16_ROWS == 0
    nblk = cap // block
    grid_spec = pltpu.PrefetchScalarGridSpec(
        num_scalar_prefetch=1,
        grid=(b, nblk),
        in_specs=[pl.BlockSpec((1, n_experts, s), lambda i, j, bd: (i, 0, 0)),
                  pl.BlockSpec((1, n_experts, s), lambda i, j, bd: (i, 0, 0)),
                  pl.BlockSpec((1, s, d), lambda i, j, bd: (i, 0, 0))],
        out_specs=[pl.BlockSpec((n_experts, block, d), lambda i, j, bd: (0, i * nblk + j, 0)),
                   pl.BlockSpec((n_experts, block, 1), lambda i, j, bd: (0, i * nblk + j, 0))],
    )
    return pl.pallas_call(
        functools.partial(_dispatch_kernel, window=window, group=DISPATCH_GROUP),
        grid_spec=grid_spec,
        out_shape=[jax.ShapeDtypeStruct((n_experts, b * cap, d), BF16),
                   jax.ShapeDtypeStruct((n_experts, b * cap, 1), F32)],
        compiler_params=_cparams(2, 48),
        name="dispatch",
    )(bounds, slot3, aff_t, h3)


def _combine_kernel(starts_ref, slot_ref, y_ref, x_ref, g_ref, o_ref, *, win, final_norm):
    bi = pl.program_id(0)
    ti = pl.program_id(1)
    tt, d = x_ref.shape
    ne, cap, _ = y_ref.shape

    wins = []
    fits = None
    for k in range(ne):
        lo = starts_ref[bi * ne + k, ti]
        hi = starts_ref[bi * ne + k, ti + 1]
        w0 = jnp.minimum(jnp.bitwise_and(lo, -BF16_ROWS), cap - win)
        ok = hi - w0 <= win
        fits = ok if fits is None else jnp.logical_and(fits, ok)
        wins.append(w0)

    slots = slot_ref[...].astype(BF16)

    def onehot(width, first_slot):
        shift = width.bit_length() - 1
        n = ne * width
        lane_of = bi * ne + lax.shift_right_logical(lax.broadcasted_iota(jnp.int32, (LANES, n), 1), shift)
        spread = (lax.broadcasted_iota(jnp.int32, (LANES, n), 0) == lane_of).astype(BF16)
        slot_b = jnp.dot(slots, spread, preferred_element_type=F32)
        c = lax.broadcasted_iota(jnp.int32, (1, n), 1)
        target = jnp.bitwise_and(c, width - 1)
        if first_slot is not None:
            kk = lax.shift_right_logical(c, shift)
            for k in range(ne):
                target = target + jnp.where(kk == k, first_slot[k], 0)
        return (slot_b == target.astype(F32)).astype(BF16)

    def finish(contrib):
        x = x_ref[...] + contrib
        if final_norm:
            inv = lax.rsqrt(jnp.mean(x * x, axis=-1, keepdims=True) + EPS)
            x = (x * inv) * g_ref[...]
        o_ref[...] = x

    @pl.when(fits)
    def _():
        ywin = jnp.concatenate([y_ref[k, pl.ds(pl.multiple_of(wins[k], BF16_ROWS), win), :] for k in range(ne)],
                               axis=0)
        finish(jnp.dot(onehot(win, wins), ywin, preferred_element_type=F32))

    @pl.when(jnp.logical_not(fits))
    def _():
        finish(jnp.dot(onehot(cap, None), y_ref[...].reshape(ne * cap, d), preferred_element_type=F32))


def _combine(starts, slot_cols, y, x1, g, *, batch, cap, tt, win, final_norm):
    m, d = x1.shape
    ne = y.shape[0]
    s = m // batch
    tps = s // tt
    assert slot_cols.shape == (s, LANES) and batch * ne == LANES
    assert win % BF16_ROWS == 0 and win <= cap and win & (win - 1) == 0 and cap & (cap - 1) == 0
    grid_spec = pltpu.PrefetchScalarGridSpec(
        num_scalar_prefetch=1,
        grid=(batch, tps),
        in_specs=[pl.BlockSpec((tt, LANES), lambda b, t, st: (t, 0)),
                  pl.BlockSpec((ne, cap, d), lambda b, t, st: (0, b, 0)),
                  pl.BlockSpec((tt, d), lambda b, t, st: (b * tps + t, 0)),
                  pl.BlockSpec((1, d), lambda b, t, st: (0, 0))],
        out_specs=pl.BlockSpec((tt, d), lambda b, t, st: (b * tps + t, 0)),
    )
    return pl.pallas_call(
        functools.partial(_combine_kernel, win=win, final_norm=final_norm),
        grid_spec=grid_spec,
        out_shape=jax.ShapeDtypeStruct((m, d), F32),
        compiler_params=_cparams(2, 56),
        name="combine",
    )(starts, slot_cols, y, x1, g.reshape(1, d))


def _expert_up_kernel(x_ref, wg_ref, wu_ref, o_ref, *, chunk):
    wg = wg_ref[0].astype(BF16)
    wu = wu_ref[0].astype(BF16)
    for r0 in range(0, x_ref.shape[1], chunk):
        x = x_ref[0, pl.ds(r0, chunk), :]
        gate = jnp.dot(x, wg, preferred_element_type=F32)
        up = jnp.dot(x, wu, preferred_element_type=F32)
        o_ref[0, pl.ds(r0, chunk), :] = (jax.nn.silu(gate) * up).astype(o_ref.dtype)


def _expert_up(xin, w_gate, w_up, *, tf):
    e, m, d = xin.shape
    f = w_gate.shape[2]
    return pl.pallas_call(
        functools.partial(_expert_up_kernel, chunk=512),
        grid=(e, f // tf),
        in_specs=[pl.BlockSpec((1, m, d), lambda i, j: (i, 0, 0)),
                  pl.BlockSpec((1, d, tf), lambda i, j: (i, 0, j)),
                  pl.BlockSpec((1, d, tf), lambda i, j: (i, 0, j))],
        out_specs=pl.BlockSpec((1, m, tf), lambda i, j: (i, 0, j)),
        out_shape=jax.ShapeDtypeStruct((e, m, f), BF16),
        compiler_params=_cparams(2, 56),
        name="expert_up",
    )(xin, w_gate, w_up)


def _expert_down_kernel(h_ref, wd_ref, a_ref, o_ref, *, chunk):
    wd = wd_ref[0].astype(BF16)
    for r0 in range(0, h_ref.shape[1], chunk):
        y = jnp.dot(h_ref[0, pl.ds(r0, chunk), :], wd, preferred_element_type=F32)
        o_ref[0, pl.ds(r0, chunk), :] = (y * a_ref[0, pl.ds(r0, chunk), :]).astype(o_ref.dtype)


def _expert_down(hidden, w_down, aff, *, tn):
    e, m, f = hidden.shape
    d = w_down.shape[2]
    return pl.pallas_call(
        functools.partial(_expert_down_kernel, chunk=512),
        grid=(e, d // tn),
        in_specs=[pl.BlockSpec((1, m, f), lambda i, j: (i, 0, 0)),
                  pl.BlockSpec((1, f, tn), lambda i, j: (i, 0, j)),
                  pl.BlockSpec((1, m, 1), lambda i, j: (i, 0, 0))],
        out_specs=pl.BlockSpec((1, m, tn), lambda i, j: (i, 0, j)),
        out_shape=jax.ShapeDtypeStruct((e, m, d), BF16),
        compiler_params=_cparams(2, 56),
        name="expert_down",
    )(hidden, w_down, aff)


def kernel(x, mem, norm_mix_g, norm_mem_g, w_in, pool_w, pool_scale, fourier_w, w_kv_mem, proj_pool, proj_fourier,
           proj_mem, w_out, norm_ffn_g, w_router, w_expert_gate, w_expert_up, w_expert_down, norm_final_g):
    b, s, d = x.shape
    depth = w_in.shape[0]
    n_tok = b * s
    pool_width = pool_w.shape[1] * pool_w.shape[2]
    four_width = fourier_w.shape[1] * fourier_w.shape[2]
    mem_width = proj_mem.shape[1]
    mix_width = pool_width + four_width + mem_width
    e = w_router.shape[2]
    cap = CAPACITY_FACTOR * s // e

    xf = x.reshape(n_tok, d)
    for l in range(depth):
        kv = _kv_proj(mem.reshape(-1, d), norm_mem_g[l], w_kv_mem[l], tm=KV_TILE[0], tn=KV_TILE[1])
        h, zp, zf, yc, w_gates = _in_proj(xf, norm_mix_g[l], w_in[l], kv.reshape(b, -1, 2 * mem_width),
                                          pool_width=pool_width, four_width=four_width, n_heads=MEM_HEADS,
                                          tm=ROW_TILE)
        ya = _pool_mixer(zp.reshape(b, s, pool_width), pool_w[l], pool_scale[l])
        yb = _fourier_mixer(zf.reshape(b, s, four_width), fourier_w[l], tk=FOURIER_ROW_TILE)
        merged = _gated_merge(ya.reshape(n_tok, -1), yb.reshape(n_tok, -1), yc, h,
                              proj_pool[l], proj_fourier[l], proj_mem[l], w_gates, tm=MERGE_TILE[0], tn=MERGE_TILE[1])

        x1, h2, aff_t = _out_proj(merged, w_out[l], xf, norm_ffn_g[l], w_router[l], batch=b, tm=ROW_TILE)
        slot_rows, slot_cols, starts, bounds = _select(aff_t.reshape(b * e, s), cap=cap, tile=COMBINE_TILE,
                                                       block=DISPATCH_BLOCK)
        xin, aff_slot = _dispatch(bounds[:, :2 * cap // DISPATCH_BLOCK], slot_rows.reshape(b, e, s), aff_t,
                                  h2.reshape(b, s, d), n_experts=e, cap=cap, block=DISPATCH_BLOCK,
                                  window=DISPATCH_WINDOW)
        hidden = _expert_up(xin, w_expert_gate[l], w_expert_up[l], tf=EXPERT_UP_COLS)
        y = _expert_down(hidden, w_expert_down[l], aff_slot, tn=EXPERT_DOWN_COLS)
        last = l + 1 == depth
        xf = _combine(starts[:, :s // COMBINE_TILE + 1], slot_cols, y, x1, norm_final_g, batch=b, cap=cap,
                      tt=COMBINE_TILE, win=COMBINE_WINDOW, final_norm=last)
    return xf.reshape(b, s, d)
```

```python
import functools

import numpy as np
import jax
import jax.numpy as jnp
from jax import lax
from jax.experimental import pallas as pl
from jax.experimental.pallas import tpu as pltpu

F32 = jnp.float32
BF16 = jnp.bfloat16

EPS = 1e-6
POOL_WINDOWS = (2, 4, 8, 16)
CAPACITY_FACTOR = 2
MEM_HEADS = 4
LANES = 128
BF16_ROWS = 16
POOL_HALO = 32
ROW_CHUNK = 256
MERGE_ROW_CHUNK = 512
GATE_CAST_SLAB = 256
DISPATCH_BLOCK = 64
DISPATCH_WINDOW = 768
DISPATCH_GROUP = 2
COMBINE_TILE = 256
COMBINE_WINDOW = 64
MIB = 1024 * 1024
ROW_TILE = 512
KV_TILE = (1024, 512)
MERGE_TILE = (1024, 512)
EXPERT_UP_COLS = 512
EXPERT_DOWN_COLS = 1024


def _cparams(n_axes, vmem_mib):
    return pltpu.CompilerParams(
        dimension_semantics=("arbitrary",) * n_axes,
        vmem_limit_bytes=vmem_mib * MIB,
    )


def _kv_proj_kernel(x_ref, g_ref, w_ref, o_ref, wc_ref):
    @pl.when(pl.program_id(1) == 0)
    def _():
        wc_ref[...] = w_ref[...].astype(BF16)

    x = x_ref[...]
    inv = lax.rsqrt(jnp.mean(x * x, axis=-1, keepdims=True) + EPS)
    hb = ((x * inv) * g_ref[...]).astype(BF16)
    o_ref[...] = jnp.dot(hb, wc_ref[...], preferred_element_type=F32).astype(o_ref.dtype)


def _kv_proj(x, g, w, *, tm, tn):
    m, k = x.shape
    n = w.shape[1]
    return pl.pallas_call(
        _kv_proj_kernel,
        grid=(n // tn, m // tm),
        in_specs=[pl.BlockSpec((tm, k), lambda c, i: (i, 0)), pl.BlockSpec((1, k), lambda c, i: (0, 0)),
                  pl.BlockSpec((k, tn), lambda c, i: (0, c))],
        out_specs=pl.BlockSpec((tm, tn), lambda c, i: (i, c)),
        out_shape=jax.ShapeDtypeStruct((m, n), BF16),
        scratch_shapes=[pltpu.VMEM((k, tn), BF16)],
        compiler_params=_cparams(2, 48),
        name="kv_proj",
    )(x, g.reshape(1, k), w)


def _in_proj_kernel(x_ref, g_ref, w_ref, kv_ref, wg_ref, h_ref, zp_ref, zf_ref, yc_ref, wgb_ref, wc_ref, *, n_heads):
    @pl.when(pl.program_id(0) == 0)
    def _():
        wc_ref[...] = w_ref[...].astype(BF16)

    wgb_ref[...] = wg_ref[...].astype(BF16)

    wp = zp_ref.shape[1]
    wf = zf_ref.shape[1]
    wm = yc_ref.shape[1]
    dh = wm // n_heads
    scale = float(dh ** -0.5)
    chunks = [pl.ds(r0, ROW_CHUNK) for r0 in range(0, x_ref.shape[0], ROW_CHUNK)]
    queries = []
    for rs in chunks:
        x = x_ref[rs, :]
        inv = lax.rsqrt(jnp.mean(x * x, axis=-1, keepdims=True) + EPS)
        hb = ((x * inv) * g_ref[...]).astype(BF16)
        h_ref[rs, :] = hb
        for c0 in range(0, wp, wf):
            zp_ref[rs, c0:c0 + wf] = jnp.dot(hb, wc_ref[:, c0:c0 + wf], preferred_element_type=F32)
        zf_ref[rs, :] = jnp.dot(hb, wc_ref[:, wp:wp + wf], preferred_element_type=F32).astype(BF16)
        queries.append(jnp.dot(hb, wc_ref[:, wp + wf:wp + wf + wm], preferred_element_type=F32).astype(BF16))
    pairs = [(ci, hd) for ci in range(len(chunks)) for hd in range(n_heads)]
    scores = [lax.dot_general(queries[ci][:, hd * dh:(hd + 1) * dh], kv_ref[0, :, hd * dh:(hd + 1) * dh],
                              (((1,), (1,)), ((), ())), preferred_element_type=F32) * scale for ci, hd in pairs]
    probs = []
    for sc in scores:
        p = jnp.exp(sc - jnp.max(sc, axis=-1, keepdims=True))
        probs.append((p / jnp.sum(p, axis=-1, keepdims=True)).astype(BF16))
    for (ci, hd), p in zip(pairs, probs):
        o = jnp.dot(p, kv_ref[0, :, wm + hd * dh:wm + (hd + 1) * dh], preferred_element_type=F32)
        yc_ref[chunks[ci], hd * dh:(hd + 1) * dh] = o.astype(BF16)


def _in_proj(x, g, w_in, kv3, *, pool_width, four_width, n_heads, tm):
    m, d = x.shape
    bsz, mlen, kvw = kv3.shape
    mem_width = kvw // 2
    mix_width = pool_width + four_width + mem_width
    tps = m // bsz // tm
    gate_width = w_in.shape[1] - mix_width
    slab = GATE_CAST_SLAB
    n_slabs = gate_width // slab
    assert gate_width % slab == 0 and mix_width % slab == 0 and n_slabs <= m // tm
    row = lambda width: pl.BlockSpec((tm, width), lambda i: (i, 0))
    return pl.pallas_call(
        functools.partial(_in_proj_kernel, n_heads=n_heads),
        grid=(m // tm,),
        in_specs=[row(d), pl.BlockSpec((1, d), lambda i: (0, 0)),
                  pl.BlockSpec((d, mix_width), lambda i: (0, 0), pipeline_mode=pl.Buffered(1)),
                  pl.BlockSpec((1, mlen, kvw), lambda i: (i // tps, 0, 0)),
                  pl.BlockSpec((d, slab), lambda i: (0, mix_width // slab + jnp.minimum(i, n_slabs - 1)))],
        out_specs=[row(d), row(pool_width), row(four_width), row(mem_width),
                   pl.BlockSpec((d, slab), lambda i: (0, jnp.minimum(i, n_slabs - 1)))],
        out_shape=[jax.ShapeDtypeStruct((m, d), BF16), jax.ShapeDtypeStruct((m, pool_width), F32),
                   jax.ShapeDtypeStruct((m, four_width), BF16), jax.ShapeDtypeStruct((m, mem_width), BF16),
                   jax.ShapeDtypeStruct((d, gate_width), BF16)],
        scratch_shapes=[pltpu.VMEM((d, mix_width), BF16)],
        compiler_params=_cparams(1, 56),
        name="in_proj",
    )(x, g.reshape(1, d), w_in, kv3, w_in)


def _out_proj_kernel(m_ref, w_ref, r_ref, g_ref, wr_ref, x1_ref, h2_ref, afft_ref, wc_ref, *, chunk, n_experts):
    @pl.when(pl.program_id(0) == 0)
    def _():
        wc_ref[...] = w_ref[...].astype(BF16)

    wr = wr_ref[...].astype(BF16)
    for r0 in range(0, m_ref.shape[0], ROW_CHUNK):
        rs = pl.ds(r0, ROW_CHUNK)
        a = m_ref[rs, :]
        for c0 in range(0, wc_ref.shape[1], chunk):
            cols = slice(c0, c0 + chunk)
            x1_ref[rs, cols] = r_ref[rs, cols] + jnp.dot(a, wc_ref[:, cols], preferred_element_type=F32)

        x = x1_ref[rs, :]
        inv = lax.rsqrt(jnp.mean(x * x, axis=-1, keepdims=True) + EPS)
        hb = ((x * inv) * g_ref[...]).astype(BF16)
        h2_ref[rs, :] = hb
        logits = jnp.dot(hb, wr, preferred_element_type=F32)
        lane = lax.broadcasted_iota(jnp.int32, logits.shape, 1)
        logits = jnp.where(lane < n_experts, logits, -1e30)
        mx = jnp.max(logits, axis=-1, keepdims=True)
        p = jnp.exp(logits - mx)
        aff = p / jnp.sum(p, axis=-1, keepdims=True)
        afft_ref[0, :, rs] = aff.T[:n_experts, :]


def _out_proj(merged, w_out, resid, g, w_router, *, batch, tm):
    m, d = resid.shape
    e = w_router.shape[1]
    s = m // batch
    tps = s // tm
    wr = jnp.pad(w_router, ((0, 0), (0, LANES - e)))
    row = pl.BlockSpec((tm, d), lambda i: (i, 0))
    return pl.pallas_call(
        functools.partial(_out_proj_kernel, chunk=512, n_experts=e),
        grid=(m // tm,),
        in_specs=[row, pl.BlockSpec((d, d), lambda i: (0, 0), pipeline_mode=pl.Buffered(1)), row,
                  pl.BlockSpec((1, d), lambda i: (0, 0)), pl.BlockSpec((d, LANES), lambda i: (0, 0))],
        out_specs=[row, row, pl.BlockSpec((1, e, tm), lambda i: (i // tps, 0, i % tps))],
        out_shape=[jax.ShapeDtypeStruct((m, d), F32), jax.ShapeDtypeStruct((m, d), BF16),
                   jax.ShapeDtypeStruct((batch, e, s), F32)],
        scratch_shapes=[pltpu.VMEM((d, d), BF16)],
        compiler_params=_cparams(1, 56),
        name="out_proj",
    )(merged, w_out, resid, g.reshape(1, d), wr)


def _pool_kernel(u_ref, pw_ref, ps_ref, o_ref, a_ref, b_ref, *, chunk):
    s = u_ref.shape[1]
    c = pw_ref.shape[1]
    h = POOL_HALO
    rows = s + 2 * h
    zeros = jnp.zeros((h, c), F32)

    def level(src, dst, off_lo, off_hi, margin):
        for r0 in range(margin, rows - margin, chunk):
            n = min(chunk, rows - margin - r0)
            dst[pl.ds(r0, n), :] = src[pl.ds(r0 + off_lo, n), :] + src[pl.ds(r0 + off_hi, n), :]

    for g, w in enumerate(POOL_WINDOWS):
        cols = slice(g * c, (g + 1) * c)
        a_ref[pl.ds(0, h), :] = zeros
        a_ref[pl.ds(h + s, h), :] = zeros
        a_ref[pl.ds(h, s), :] = u_ref[0, :, cols]
        src, off_lo, off_hi = a_ref, -1, 0
        if w >= 4:
            level(a_ref, b_ref, -1, 0, 8)
            src, off_lo, off_hi = b_ref, -1, 1
        if w >= 8:
            level(b_ref, a_ref, -1, 1, 16)
            src, off_lo, off_hi = a_ref, -2, 2
        if w >= 16:
            level(a_ref, b_ref, -2, 2, 24)
            src, off_lo, off_hi = b_ref, -4, 4
        wg = pw_ref[g].astype(BF16)
        scale = ps_ref[:, cols]
        for r0 in range(0, s, chunk):
            pos = r0 + lax.broadcasted_iota(jnp.int32, (chunk, 1), 0)
            lo = jnp.maximum(pos - w // 2, 0)
            hi = jnp.minimum(pos + (w - w // 2), s)
            cnt = (hi - lo).astype(F32)
            tot = src[pl.ds(h + r0 + off_lo, chunk), :] + src[pl.ds(h + r0 + off_hi, chunk), :]
            pooled = tot / cnt - u_ref[0, pl.ds(r0, chunk), cols]
            y = jnp.dot(pooled.astype(BF16), wg, preferred_element_type=F32)
            o_ref[0, pl.ds(r0, chunk), cols] = (y * scale).astype(o_ref.dtype)


def _pool_mixer(z3, pool_w, pool_scale):
    b, s, _ = z3.shape
    g, c, _ = pool_w.shape
    width = g * c
    assert POOL_WINDOWS == (2, 4, 8, 16) and g == len(POOL_WINDOWS)
    return pl.pallas_call(
        functools.partial(_pool_kernel, chunk=256),
        grid=(b,),
        in_specs=[pl.BlockSpec((1, s, width), lambda i: (i, 0, 0)),
                  pl.BlockSpec((g, c, c), lambda i: (0, 0, 0)),
                  pl.BlockSpec((1, width), lambda i: (0, 0))],
        out_specs=pl.BlockSpec((1, s, width), lambda i: (i, 0, 0)),
        out_shape=jax.ShapeDtypeStruct((b, s, width), BF16),
        scratch_shapes=[pltpu.VMEM((s + 2 * POOL_HALO, c), F32), pltpu.VMEM((s + 2 * POOL_HALO, c), F32)],
        compiler_params=_cparams(1, 48),
        name="pool_mixer",
    )(z3, pool_w, pool_scale.reshape(1, width))


def _dft_tables(s, c):
    def tab(n):
        r = np.outer(np.arange(n), np.arange(n)) % n
        ang = r * (2.0 * np.pi / n)
        return np.cos(ang), np.sin(ang)
    cs, ss = tab(s)
    cc, sc = tab(c)
    alt = np.zeros((8, s))
    alt[0] = 1.0 - 2.0 * (np.arange(s) % 2)
    const = lambda a: jnp.asarray(a, dtype=F32).astype(BF16)
    return const(cs[:s // 2]), const(ss[:s // 2]), const(alt), const(np.concatenate([cc, sc], axis=1))


def _fourier_kernel(u_ref, cs_ref, ss_ref, alt_ref, ccsc_ref, fw_ref, o_ref, t1_ref, t2_ref, *, norm):
    s = u_ref.shape[1]
    half = s // 2
    ng, c, _ = fw_ref.shape
    for g in range(ng):
        ab = jnp.dot(u_ref[0, :, g * c:(g + 1) * c], ccsc_ref[...], preferred_element_type=F32)
        t1_ref[:, g * c:(g + 1) * c] = ab[:, :c].astype(BF16)
        t2_ref[:, g * c:(g + 1) * c] = ab[:, c:].astype(BF16)
    p = jnp.dot(cs_ref[...], t1_ref[...], preferred_element_type=F32)
    q = jnp.dot(ss_ref[...], t2_ref[...], preferred_element_type=F32)
    mid = jnp.dot(alt_ref[...], t1_ref[...], preferred_element_type=F32)
    low = ((p - q) * norm).astype(BF16)
    mirror = ((p + q) * norm).astype(BF16)
    mid = (mid * norm).astype(BF16)
    rr = lax.broadcasted_iota(jnp.int32, (half, half), 0)
    cc = lax.broadcasted_iota(jnp.int32, (half, half), 1)
    reverse = (rr + cc == half).astype(BF16)
    first = lax.broadcasted_iota(jnp.int32, (half, 1), 0) == 0
    y_mirror, y_mid = [], []
    for g in range(ng):
        cols = slice(g * c, (g + 1) * c)
        w = fw_ref[g].astype(BF16)
        o_ref[0, pl.ds(0, half), cols] = jnp.dot(low[:, cols], w, preferred_element_type=F32).astype(o_ref.dtype)
        y_mirror.append(jnp.dot(mirror[:, cols], w, preferred_element_type=F32).astype(BF16))
        y_mid.append(jnp.dot(mid[:, cols], w, preferred_element_type=F32)[0:1, :])
    y_rev = jnp.dot(reverse, jnp.concatenate(y_mirror, axis=1), preferred_element_type=F32)
    o_ref[0, pl.ds(half, half), :] = jnp.where(first, jnp.concatenate(y_mid, axis=1), y_rev).astype(o_ref.dtype)


def _fourier_mixer(z3, fourier_w):
    b, s, _ = z3.shape
    ng, c, _ = fourier_w.shape
    width = ng * c
    cs, ss, alt, ccsc = _dft_tables(s, c)
    const = lambda shape: pl.BlockSpec(shape, lambda i: (0,) * len(shape), pipeline_mode=pl.Buffered(1))
    return pl.pallas_call(
        functools.partial(_fourier_kernel, norm=float((s * c) ** -0.5)),
        grid=(b,),
        in_specs=[pl.BlockSpec((1, s, width), lambda i: (i, 0, 0)),
                  const((s // 2, s)), const((s // 2, s)), const((8, s)), const((c, 2 * c)),
                  pl.BlockSpec((ng, c, c), lambda i: (0, 0, 0))],
        out_specs=pl.BlockSpec((1, s, width), lambda i: (i, 0, 0)),
        out_shape=jax.ShapeDtypeStruct((b, s, width), BF16),
        scratch_shapes=[pltpu.VMEM((s, width), BF16), pltpu.VMEM((s, width), BF16)],
        compiler_params=_cparams(1, 48),
        name="fourier_mixer",
    )(z3, cs, ss, alt, ccsc, fourier_w)


def _merge_kernel(ya_ref, yb_ref, yc_ref, h_ref, pp_ref, pf_ref, pm_ref, cg0, cg1, cg2, o_ref, cpp, cpf, cpm):
    @pl.when(pl.program_id(1) == 0)
    def _():
        for src, dst in ((pp_ref, cpp), (pf_ref, cpf), (pm_ref, cpm)):
            dst[...] = src[...].astype(BF16)

    for r0 in range(0, h_ref.shape[0], MERGE_ROW_CHUNK):
        rs = pl.ds(r0, MERGE_ROW_CHUNK)
        h = h_ref[rs, :]

        def branch(y_ref, proj, gate_w):
            gate = jax.nn.sigmoid(jnp.dot(h, gate_w[...], preferred_element_type=F32))
            return gate * jnp.dot(y_ref[rs, :], proj[...], preferred_element_type=F32)

        acc = branch(ya_ref, cpp, cg0)
        acc = acc + branch(yb_ref, cpf, cg1)
        acc = acc + branch(yc_ref, cpm, cg2)
        o_ref[rs, :] = acc.astype(o_ref.dtype)


def _gated_merge(ya, yb, yc, h, proj_pool, proj_fourier, proj_mem, w_gates, *, tm, tn):
    m, d = h.shape
    nb = d // tn
    act = lambda width: pl.BlockSpec((tm, width), lambda n, i: (i, 0))
    wsp = lambda rows, off: pl.BlockSpec((rows, tn), lambda n, i: (0, n + off))
    return pl.pallas_call(
        _merge_kernel,
        grid=(d // tn, m // tm),
        in_specs=[act(ya.shape[1]), act(yb.shape[1]), act(yc.shape[1]), act(d),
                  wsp(proj_pool.shape[0], 0), wsp(proj_fourier.shape[0], 0), wsp(proj_mem.shape[0], 0),
                  wsp(d, 0), wsp(d, nb), wsp(d, 2 * nb)],
        out_specs=pl.BlockSpec((tm, tn), lambda n, i: (i, n)),
        out_shape=jax.ShapeDtypeStruct((m, d), BF16),
        scratch_shapes=[pltpu.VMEM((proj_pool.shape[0], tn), BF16), pltpu.VMEM((proj_fourier.shape[0], tn), BF16),
                        pltpu.VMEM((proj_mem.shape[0], tn), BF16)],
        compiler_params=_cparams(2, 56),
        name="gated_merge",
    )(ya, yb, yc, h, proj_pool, proj_fourier, proj_mem, w_gates, w_gates, w_gates)


def _select_kernel(aff_ref, tri_ref, slot_ref, slot_t_ref, starts_ref, bounds_ref, *, cap, tile, block):
    a = aff_ref[...]
    rows = a.shape[0]
    capf = float(cap)

    def count(pred):
        return jnp.sum(pred.astype(F32), axis=-1, keepdims=True)

    def body(i, t_bits):
        cand = t_bits | jnp.left_shift(jnp.int32(1), 30 - i)
        return jnp.where(count(a >= pltpu.bitcast(cand, F32)) >= capf, cand, t_bits)

    t = pltpu.bitcast(lax.fori_loop(0, 31, body, jnp.zeros((rows, 1), jnp.int32)), F32)
    gt = a > t
    eq = a == t
    need = capf - count(gt)
    tri = tri_ref[...]
    eq_rank = jnp.dot(eq.astype(BF16), tri, preferred_element_type=F32)
    sel = gt | (eq & (eq_rank < need))
    pos = jnp.dot(sel.astype(BF16), tri, preferred_element_type=F32)
    slot = jnp.where(sel, pos, -1.0)
    slot_ref[...] = slot
    slot_t_ref[...] = slot.T
    s = a.shape[1]
    lane = lax.broadcasted_iota(jnp.int32, (rows, LANES), 1)
    starts = jnp.where(lane == s // tile, capf, 0.0)
    for t in range(s // tile):
        starts = jnp.where(lane == t, pos[:, t * tile:t * tile + 1], starts)
    starts_ref[...] = starts.astype(jnp.int32)

    nblk = cap // block
    tok = lax.broadcasted_iota(jnp.int32, (1, s), 1).astype(F32)

    def token_of(j):
        return jnp.sum(jnp.where(slot == float(j), tok, 0.0), axis=-1, keepdims=True)

    bounds = jnp.zeros((rows, LANES), F32)
    for i in range(nblk):
        bounds = jnp.where(lane == i, token_of(i * block), bounds)
        bounds = jnp.where(lane == nblk + i, token_of((i + 1) * block - 1) + 1.0, bounds)
    bounds_ref[...] = bounds.astype(jnp.int32)


def _select(aff_rows, *, cap, tile, block):
    rows, s = aff_rows.shape
    assert rows == LANES
    idx = jnp.arange(s, dtype=jnp.int32)
    tri = (idx[:, None] < idx[None, :]).astype(BF16)
    full = lambda shape: pl.BlockSpec(shape, lambda i: (0, 0))
    assert 2 * cap // block <= LANES and s // tile + 1 <= LANES
    return pl.pallas_call(
        functools.partial(_select_kernel, cap=cap, tile=tile, block=block),
        grid=(1,),
        in_specs=[full((rows, s)), full((s, s))],
        out_specs=[full((rows, s)), full((s, rows)), full((rows, LANES)), full((rows, LANES))],
        out_shape=[jax.ShapeDtypeStruct((rows, s), F32), jax.ShapeDtypeStruct((s, rows), F32),
                   jax.ShapeDtypeStruct((rows, LANES), jnp.int32), jax.ShapeDtypeStruct((rows, LANES), jnp.int32)],
        compiler_params=_cparams(1, 48),
        name="expert_select",
    )(aff_rows, tri)


def _dispatch_kernel(bounds_ref, slot_ref, aff_ref, h_ref, o_ref, oa_ref, *, window, group):
    bi = pl.program_id(0)
    ji = pl.program_id(1)
    nblk = pl.num_programs(1)
    ne, block, d = o_ref.shape
    s = h_ref.shape[1]

    lo = hi = None
    for k in range(ne):
        l = bounds_ref[bi * ne + k, ji]
        h = bounds_ref[bi * ne + k, nblk + ji]
        lo = l if lo is None else jnp.minimum(lo, l)
        hi = h if hi is None else jnp.maximum(hi, h)
    w0 = jnp.minimum(jnp.bitwise_and(lo, -BF16_ROWS), s - window)
    fits = hi - w0 <= window

    tok_all = lax.broadcasted_iota(jnp.int32, (1, s), 1).astype(F32)
    slot_ids = (ji * block + lax.broadcasted_iota(jnp.int32, (block, 1), 0)).astype(F32)

    def tokens_of(g0):
        cols = []
        for k in range(g0, g0 + group):
            hit = slot_ref[0, k:k + 1, :] == slot_ids
            cols.append(jnp.sum(jnp.where(hit, tok_all, 0.0), axis=-1, keepdims=True))
            oa_ref[k] = jnp.sum(jnp.where(hit, aff_ref[0, k:k + 1, :], 0.0), axis=-1, keepdims=True)
        return jnp.concatenate(cols, axis=0)

    def gather(width, first_tok, src):
        t = lax.broadcasted_iota(jnp.int32, (group * block, width), 1).astype(F32) + first_tok
        toks = tokens_of(0)
        for g0 in range(0, ne, group):
            onehot = (toks == t).astype(BF16)
            if g0 + group < ne:
                toks = tokens_of(g0 + group)
            out = jnp.dot(onehot, src, preferred_element_type=F32)
            o_ref[g0:g0 + group] = out.reshape(group, block, d).astype(o_ref.dtype)

    @pl.when(fits)
    def _():
        gather(window, w0.astype(F32), h_ref[0, pl.ds(pl.multiple_of(w0, BF16_ROWS), window), :])

    @pl.when(jnp.logical_not(fits))
    def _():
        gather(s, 0.0, h_ref[0])


def _dispatch(bounds, slot3, aff_t, h3, *, n_experts, cap, block, window):
    b, s, d = h3.shape
    assert window % BF16_ROWS == 0 and window <= s and cap % block == 0 and block % BF16_ROWS == 0
    nblk = cap // block
    grid_spec = pltpu.PrefetchScalarGridSpec(
        num_scalar_prefetch=1,
        grid=(b, nblk),
        in_specs=[pl.BlockSpec((1, n_experts, s), lambda i, j, bd: (i, 0, 0)),
                  pl.BlockSpec((1, n_experts, s), lambda i, j, bd: (i, 0, 0)),
                  pl.BlockSpec((1, s, d), lambda i, j, bd: (i, 0, 0))],
        out_specs=[pl.BlockSpec((n_experts, block, d), lambda i, j, bd: (0, i * nblk + j, 0)),
                   pl.BlockSpec((n_experts, block, 1), lambda i, j, bd: (0, i * nblk + j, 0))],
    )
    return pl.pallas_call(
        functools.partial(_dispatch_kernel, window=window, group=DISPATCH_GROUP),
        grid_spec=grid_spec,
        out_shape=[jax.ShapeDtypeStruct((n_experts, b * cap, d), BF16),
                   jax.ShapeDtypeStruct((n_experts, b * cap, 1), F32)],
        compiler_params=_cparams(2, 48),
        name="dispatch",
    )(bounds, slot3, aff_t, h3)


def _combine_kernel(starts_ref, slot_ref, y_ref, x_ref, g_ref, o_ref, *, win, final_norm):
    bi = pl.program_id(0)
    ti = pl.program_id(1)
    tt, d = x_ref.shape
    ne, cap, _ = y_ref.shape

    wins = []
    fits = None
    for k in range(ne):
        lo = starts_ref[bi * ne + k, ti]
        hi = starts_ref[bi * ne + k, ti + 1]
        w0 = jnp.minimum(jnp.bitwise_and(lo, -BF16_ROWS), cap - win)
        ok = hi - w0 <= win
        fits = ok if fits is None else jnp.logical_and(fits, ok)
        wins.append(w0)

    slots = slot_ref[...].astype(BF16)

    def onehot(width, first_slot):
        shift = width.bit_length() - 1
        n = ne * width
        lane_of = bi * ne + lax.shift_right_logical(lax.broadcasted_iota(jnp.int32, (LANES, n), 1), shift)
        spread = (lax.broadcasted_iota(jnp.int32, (LANES, n), 0) == lane_of).astype(BF16)
        slot_b = jnp.dot(slots, spread, preferred_element_type=F32)
        c = lax.broadcasted_iota(jnp.int32, (1, n), 1)
        target = jnp.bitwise_and(c, width - 1)
        if first_slot is not None:
            kk = lax.shift_right_logical(c, shift)
            for k in range(ne):
                target = target + jnp.where(kk == k, first_slot[k], 0)
        return (slot_b == target.astype(F32)).astype(BF16)

    def finish(contrib):
        x = x_ref[...] + contrib
        if final_norm:
            inv = lax.rsqrt(jnp.mean(x * x, axis=-1, keepdims=True) + EPS)
            x = (x * inv) * g_ref[...]
        o_ref[...] = x

    @pl.when(fits)
    def _():
        ywin = jnp.concatenate([y_ref[k, pl.ds(pl.multiple_of(wins[k], BF16_ROWS), win), :] for k in range(ne)],
                               axis=0)
        finish(jnp.dot(onehot(win, wins), ywin, preferred_element_type=F32))

    @pl.when(jnp.logical_not(fits))
    def _():
        finish(jnp.dot(onehot(cap, None), y_ref[...].reshape(ne * cap, d), preferred_element_type=F32))


def _combine(starts, slot_cols, y, x1, g, *, batch, cap, tt, win, final_norm):
    m, d = x1.shape
    ne = y.shape[0]
    s = m // batch
    tps = s // tt
    assert slot_cols.shape == (s, LANES) and batch * ne == LANES
    assert win % BF16_ROWS == 0 and win <= cap and win & (win - 1) == 0 and cap & (cap - 1) == 0
    grid_spec = pltpu.PrefetchScalarGridSpec(
        num_scalar_prefetch=1,
        grid=(batch, tps),
        in_specs=[pl.BlockSpec((tt, LANES), lambda b, t, st: (t, 0)),
                  pl.BlockSpec((ne, cap, d), lambda b, t, st: (0, b, 0)),
                  pl.BlockSpec((tt, d), lambda b, t, st: (b * tps + t, 0)),
                  pl.BlockSpec((1, d), lambda b, t, st: (0, 0))],
        out_specs=pl.BlockSpec((tt, d), lambda b, t, st: (b * tps + t, 0)),
    )
    return pl.pallas_call(
        functools.partial(_combine_kernel, win=win, final_norm=final_norm),
        grid_spec=grid_spec,
        out_shape=jax.ShapeDtypeStruct((m, d), F32),
        compiler_params=_cparams(2, 56),
        name="combine",
    )(starts, slot_cols, y, x1, g.reshape(1, d))


def _expert_up_kernel(x_ref, wg_ref, wu_ref, o_ref, *, chunk):
    wg = wg_ref[0].astype(BF16)
    wu = wu_ref[0].astype(BF16)
    for r0 in range(0, x_ref.shape[1], chunk):
        x = x_ref[0, pl.ds(r0, chunk), :]
        gate = jnp.dot(x, wg, preferred_element_type=F32)
        up = jnp.dot(x, wu, preferred_element_type=F32)
        o_ref[0, pl.ds(r0, chunk), :] = (jax.nn.silu(gate) * up).astype(o_ref.dtype)


def _expert_up(xin, w_gate, w_up, *, tf):
    e, m, d = xin.shape
    f = w_gate.shape[2]
    return pl.pallas_call(
        functools.partial(_expert_up_kernel, chunk=512),
        grid=(e, f // tf),
        in_specs=[pl.BlockSpec((1, m, d), lambda i, j: (i, 0, 0)),
                  pl.BlockSpec((1, d, tf), lambda i, j: (i, 0, j)),
                  pl.BlockSpec((1, d, tf), lambda i, j: (i, 0, j))],
        out_specs=pl.BlockSpec((1, m, tf), lambda i, j: (i, 0, j)),
        out_shape=jax.ShapeDtypeStruct((e, m, f), BF16),
        compiler_params=_cparams(2, 56),
        name="expert_up",
    )(xin, w_gate, w_up)


def _expert_down_kernel(h_ref, wd_ref, a_ref, o_ref, *, chunk):
    wd = wd_ref[0].astype(BF16)
    for r0 in range(0, h_ref.shape[1], chunk):
        y = jnp.dot(h_ref[0, pl.ds(r0, chunk), :], wd, preferred_element_type=F32)
        o_ref[0, pl.ds(r0, chunk), :] = (y * a_ref[0, pl.ds(r0, chunk), :]).astype(o_ref.dtype)


def _expert_down(hidden, w_down, aff, *, tn):
    e, m, f = hidden.shape
    d = w_down.shape[2]
    return pl.pallas_call(
        functools.partial(_expert_down_kernel, chunk=512),
        grid=(e, d // tn),
        in_specs=[pl.BlockSpec((1, m, f), lambda i, j: (i, 0, 0)),
                  pl.BlockSpec((1, f, tn), lambda i, j: (i, 0, j)),
                  pl.BlockSpec((1, m, 1), lambda i, j: (i, 0, 0))],
        out_specs=pl.BlockSpec((1, m, tn), lambda i, j: (i, 0, j)),
        out_shape=jax.ShapeDtypeStruct((e, m, d), BF16),
        compiler_params=_cparams(2, 56),
        name="expert_down",
    )(hidden, w_down, aff)


def kernel(x, mem, norm_mix_g, norm_mem_g, w_in, pool_w, pool_scale, fourier_w, w_kv_mem, proj_pool, proj_fourier,
           proj_mem, w_out, norm_ffn_g, w_router, w_expert_gate, w_expert_up, w_expert_down, norm_final_g):
    b, s, d = x.shape
    depth = w_in.shape[0]
    n_tok = b * s
    pool_width = pool_w.shape[1] * pool_w.shape[2]
    four_width = fourier_w.shape[1] * fourier_w.shape[2]
    mem_width = proj_mem.shape[1]
    mix_width = pool_width + four_width + mem_width
    e = w_router.shape[2]
    cap = CAPACITY_FACTOR * s // e

    xf = x.reshape(n_tok, d)
    for l in range(depth):
        kv = _kv_proj(mem.reshape(-1, d), norm_mem_g[l], w_kv_mem[l], tm=KV_TILE[0], tn=KV_TILE[1])
        h, zp, zf, yc, w_gates = _in_proj(xf, norm_mix_g[l], w_in[l], kv.reshape(b, -1, 2 * mem_width),
                                          pool_width=pool_width, four_width=four_width, n_heads=MEM_HEADS,
                                          tm=ROW_TILE)
        ya = _pool_mixer(zp.reshape(b, s, pool_width), pool_w[l], pool_scale[l])
        yb = _fourier_mixer(zf.reshape(b, s, four_width), fourier_w[l])
        merged = _gated_merge(ya.reshape(n_tok, -1), yb.reshape(n_tok, -1), yc, h,
                              proj_pool[l], proj_fourier[l], proj_mem[l], w_gates, tm=MERGE_TILE[0], tn=MERGE_TILE[1])

        x1, h2, aff_t = _out_proj(merged, w_out[l], xf, norm_ffn_g[l], w_router[l], batch=b, tm=ROW_TILE)
        slot_rows, slot_cols, starts, bounds = _select(aff_t.reshape(b * e, s), cap=cap, tile=COMBINE_TILE,
                                                       block=DISPATCH_BLOCK)
        xin, aff_slot = _dispatch(bounds[:, :2 * cap // DISPATCH_BLOCK], slot_rows.reshape(b, e, s), aff_t,
                                  h2.reshape(b, s, d), n_experts=e, cap=cap, block=DISPATCH_BLOCK,
                                  window=DISPATCH_WINDOW)
        hidden = _expert_up(xin, w_expert_gate[l], w_expert_up[l], tf=EXPERT_UP_COLS)
        y = _expert_down(hidden, w_expert_down[l], aff_slot, tn=EXPERT_DOWN_COLS)
        last = l + 1 == depth
        xf = _combine(starts[:, :s // COMBINE_TILE + 1], slot_cols, y, x1, norm_final_g, batch=b, cap=cap,
                      tt=COMBINE_TILE, win=COMBINE_WINDOW, final_norm=last)
    return xf.reshape(b, s, d)
```

```python
import functools

import numpy as np
import jax
import jax.numpy as jnp
from jax import lax
from jax.experimental import pallas as pl
from jax.experimental.pallas import tpu as pltpu

F32 = jnp.float32
BF16 = jnp.bfloat16

EPS = 1e-6
POOL_WINDOWS = (2, 4, 8, 16)
CAPACITY_FACTOR = 2
MEM_HEADS = 4
LANES = 128
BF16_ROWS = 16
POOL_HALO = 32
ROW_CHUNK = 256
MERGE_ROW_CHUNK = 512
GATE_CAST_SLAB = 256
DISPATCH_BLOCK = 64
DISPATCH_WINDOW = 768
DISPATCH_GROUP = 2
COMBINE_TILE = 256
COMBINE_WINDOW = 64
MIB = 1024 * 1024
ROW_TILE = 512
KV_TILE = (1024, 512)
MERGE_TILE = (1024, 512)
EXPERT_UP_COLS = 512
EXPERT_DOWN_COLS = 1024


def _cparams(n_axes, vmem_mib):
    return pltpu.CompilerParams(
        dimension_semantics=("arbitrary",) * n_axes,
        vmem_limit_bytes=vmem_mib * MIB,
    )


def _kv_proj_kernel(x_ref, g_ref, w_ref, o_ref, wc_ref):
    @pl.when(pl.program_id(1) == 0)
    def _():
        wc_ref[...] = w_ref[...].astype(BF16)

    x = x_ref[...]
    inv = lax.rsqrt(jnp.mean(x * x, axis=-1, keepdims=True) + EPS)
    hb = ((x * inv) * g_ref[...]).astype(BF16)
    o_ref[...] = jnp.dot(hb, wc_ref[...], preferred_element_type=F32).astype(o_ref.dtype)


def _kv_proj(x, g, w, *, tm, tn):
    m, k = x.shape
    n = w.shape[1]
    return pl.pallas_call(
        _kv_proj_kernel,
        grid=(n // tn, m // tm),
        in_specs=[pl.BlockSpec((tm, k), lambda c, i: (i, 0)), pl.BlockSpec((1, k), lambda c, i: (0, 0)),
                  pl.BlockSpec((k, tn), lambda c, i: (0, c))],
        out_specs=pl.BlockSpec((tm, tn), lambda c, i: (i, c)),
        out_shape=jax.ShapeDtypeStruct((m, n), BF16),
        scratch_shapes=[pltpu.VMEM((k, tn), BF16)],
        compiler_params=_cparams(2, 48),
        name="kv_proj",
    )(x, g.reshape(1, k), w)


def _in_proj_kernel(x_ref, g_ref, w_ref, kv_ref, wg_ref, h_ref, zp_ref, zf_ref, yc_ref, wgb_ref, wc_ref, *, n_heads):
    @pl.when(pl.program_id(0) == 0)
    def _():
        wc_ref[...] = w_ref[...].astype(BF16)

    wgb_ref[...] = wg_ref[...].astype(BF16)

    wp = zp_ref.shape[1]
    wf = zf_ref.shape[1]
    wm = yc_ref.shape[1]
    dh = wm // n_heads
    scale = float(dh ** -0.5)
    chunks = [pl.ds(r0, ROW_CHUNK) for r0 in range(0, x_ref.shape[0], ROW_CHUNK)]
    queries = []
    for rs in chunks:
        x = x_ref[rs, :]
        inv = lax.rsqrt(jnp.mean(x * x, axis=-1, keepdims=True) + EPS)
        hb = ((x * inv) * g_ref[...]).astype(BF16)
        h_ref[rs, :] = hb
        for c0 in range(0, wp, wf):
            zp_ref[rs, c0:c0 + wf] = jnp.dot(hb, wc_ref[:, c0:c0 + wf], preferred_element_type=F32)
        zf_ref[rs, :] = jnp.dot(hb, wc_ref[:, wp:wp + wf], preferred_element_type=F32).astype(BF16)
        queries.append(jnp.dot(hb, wc_ref[:, wp + wf:wp + wf + wm], preferred_element_type=F32).astype(BF16))
    pairs = [(ci, hd) for ci in range(len(chunks)) for hd in range(n_heads)]
    scores = [lax.dot_general(queries[ci][:, hd * dh:(hd + 1) * dh], kv_ref[0, :, hd * dh:(hd + 1) * dh],
                              (((1,), (1,)), ((), ())), preferred_element_type=F32) * scale for ci, hd in pairs]
    probs = []
    for sc in scores:
        p = jnp.exp(sc - jnp.max(sc, axis=-1, keepdims=True))
        probs.append((p / jnp.sum(p, axis=-1, keepdims=True)).astype(BF16))
    for (ci, hd), p in zip(pairs, probs):
        o = jnp.dot(p, kv_ref[0, :, wm + hd * dh:wm + (hd + 1) * dh], preferred_element_type=F32)
        yc_ref[chunks[ci], hd * dh:(hd + 1) * dh] = o.astype(BF16)


def _in_proj(x, g, w_in, kv3, *, pool_width, four_width, n_heads, tm):
    m, d = x.shape
    bsz, mlen, kvw = kv3.shape
    mem_width = kvw // 2
    mix_width = pool_width + four_width + mem_width
    tps = m // bsz // tm
    gate_width = w_in.shape[1] - mix_width
    slab = GATE_CAST_SLAB
    n_slabs = gate_width // slab
    assert gate_width % slab == 0 and mix_width % slab == 0 and n_slabs <= m // tm
    row = lambda width: pl.BlockSpec((tm, width), lambda i: (i, 0))
    return pl.pallas_call(
        functools.partial(_in_proj_kernel, n_heads=n_heads),
        grid=(m // tm,),
        in_specs=[row(d), pl.BlockSpec((1, d), lambda i: (0, 0)),
                  pl.BlockSpec((d, mix_width), lambda i: (0, 0), pipeline_mode=pl.Buffered(1)),
                  pl.BlockSpec((1, mlen, kvw), lambda i: (i // tps, 0, 0)),
                  pl.BlockSpec((d, slab), lambda i: (0, mix_width // slab + jnp.minimum(i, n_slabs - 1)))],
        out_specs=[row(d), row(pool_width), row(four_width), row(mem_width),
                   pl.BlockSpec((d, slab), lambda i: (0, jnp.minimum(i, n_slabs - 1)))],
        out_shape=[jax.ShapeDtypeStruct((m, d), BF16), jax.ShapeDtypeStruct((m, pool_width), F32),
                   jax.ShapeDtypeStruct((m, four_width), BF16), jax.ShapeDtypeStruct((m, mem_width), BF16),
                   jax.ShapeDtypeStruct((d, gate_width), BF16)],
        scratch_shapes=[pltpu.VMEM((d, mix_width), BF16)],
        compiler_params=_cparams(1, 56),
        name="in_proj",
    )(x, g.reshape(1, d), w_in, kv3, w_in)


def _out_proj_kernel(m_ref, w_ref, r_ref, g_ref, wr_ref, x1_ref, h2_ref, afft_ref, wc_ref, *, chunk, n_experts):
    @pl.when(pl.program_id(0) == 0)
    def _():
        wc_ref[...] = w_ref[...].astype(BF16)

    wr = wr_ref[...].astype(BF16)
    for r0 in range(0, m_ref.shape[0], ROW_CHUNK):
        rs = pl.ds(r0, ROW_CHUNK)
        a = m_ref[rs, :]
        for c0 in range(0, wc_ref.shape[1], chunk):
            cols = slice(c0, c0 + chunk)
            x1_ref[rs, cols] = r_ref[rs, cols] + jnp.dot(a, wc_ref[:, cols], preferred_element_type=F32)

        x = x1_ref[rs, :]
        inv = lax.rsqrt(jnp.mean(x * x, axis=-1, keepdims=True) + EPS)
        hb = ((x * inv) * g_ref[...]).astype(BF16)
        h2_ref[rs, :] = hb
        logits = jnp.dot(hb, wr, preferred_element_type=F32)
        lane = lax.broadcasted_iota(jnp.int32, logits.shape, 1)
        logits = jnp.where(lane < n_experts, logits, -1e30)
        mx = jnp.max(logits, axis=-1, keepdims=True)
        p = jnp.exp(logits - mx)
        aff = p / jnp.sum(p, axis=-1, keepdims=True)
        afft_ref[0, :, rs] = aff.T[:n_experts, :]


def _out_proj(merged, w_out, resid, g, w_router, *, batch, tm):
    m, d = resid.shape
    e = w_router.shape[1]
    s = m // batch
    tps = s // tm
    wr = jnp.pad(w_router, ((0, 0), (0, LANES - e)))
    row = pl.BlockSpec((tm, d), lambda i: (i, 0))
    return pl.pallas_call(
        functools.partial(_out_proj_kernel, chunk=512, n_experts=e),
        grid=(m // tm,),
        in_specs=[row, pl.BlockSpec((d, d), lambda i: (0, 0), pipeline_mode=pl.Buffered(1)), row,
                  pl.BlockSpec((1, d), lambda i: (0, 0)), pl.BlockSpec((d, LANES), lambda i: (0, 0))],
        out_specs=[row, row, pl.BlockSpec((1, e, tm), lambda i: (i // tps, 0, i % tps))],
        out_shape=[jax.ShapeDtypeStruct((m, d), F32), jax.ShapeDtypeStruct((m, d), BF16),
                   jax.ShapeDtypeStruct((batch, e, s), F32)],
        scratch_shapes=[pltpu.VMEM((d, d), BF16)],
        compiler_params=_cparams(1, 56),
        name="out_proj",
    )(merged, w_out, resid, g.reshape(1, d), wr)


def _pool_kernel(u_ref, pw_ref, ps_ref, o_ref, a_ref, b_ref, *, chunk):
    s = u_ref.shape[1]
    c = pw_ref.shape[1]
    h = POOL_HALO
    rows = s + 2 * h
    zeros = jnp.zeros((h, c), F32)

    def level(src, dst, off_lo, off_hi, margin):
        for r0 in range(margin, rows - margin, chunk):
            n = min(chunk, rows - margin - r0)
            dst[pl.ds(r0, n), :] = src[pl.ds(r0 + off_lo, n), :] + src[pl.ds(r0 + off_hi, n), :]

    for g, w in enumerate(POOL_WINDOWS):
        cols = slice(g * c, (g + 1) * c)
        a_ref[pl.ds(0, h), :] = zeros
        a_ref[pl.ds(h + s, h), :] = zeros
        a_ref[pl.ds(h, s), :] = u_ref[0, :, cols]
        src, off_lo, off_hi = a_ref, -1, 0
        if w >= 4:
            level(a_ref, b_ref, -1, 0, 8)
            src, off_lo, off_hi = b_ref, -1, 1
        if w >= 8:
            level(b_ref, a_ref, -1, 1, 16)
            src, off_lo, off_hi = a_ref, -2, 2
        if w >= 16:
            level(a_ref, b_ref, -2, 2, 24)
            src, off_lo, off_hi = b_ref, -4, 4
        wg = pw_ref[g].astype(BF16)
        scale = ps_ref[:, cols]
        for r0 in range(0, s, chunk):
            pos = r0 + lax.broadcasted_iota(jnp.int32, (chunk, 1), 0)
            lo = jnp.maximum(pos - w // 2, 0)
            hi = jnp.minimum(pos + (w - w // 2), s)
            cnt = (hi - lo).astype(F32)
            tot = src[pl.ds(h + r0 + off_lo, chunk), :] + src[pl.ds(h + r0 + off_hi, chunk), :]
            pooled = tot / cnt - u_ref[0, pl.ds(r0, chunk), cols]
            y = jnp.dot(pooled.astype(BF16), wg, preferred_element_type=F32)
            o_ref[0, pl.ds(r0, chunk), cols] = (y * scale).astype(o_ref.dtype)


def _pool_mixer(z3, pool_w, pool_scale):
    b, s, _ = z3.shape
    g, c, _ = pool_w.shape
    width = g * c
    assert POOL_WINDOWS == (2, 4, 8, 16) and g == len(POOL_WINDOWS)
    return pl.pallas_call(
        functools.partial(_pool_kernel, chunk=256),
        grid=(b,),
        in_specs=[pl.BlockSpec((1, s, width), lambda i: (i, 0, 0)),
                  pl.BlockSpec((g, c, c), lambda i: (0, 0, 0)),
                  pl.BlockSpec((1, width), lambda i: (0, 0))],
        out_specs=pl.BlockSpec((1, s, width), lambda i: (i, 0, 0)),
        out_shape=jax.ShapeDtypeStruct((b, s, width), BF16),
        scratch_shapes=[pltpu.VMEM((s + 2 * POOL_HALO, c), F32), pltpu.VMEM((s + 2 * POOL_HALO, c), F32)],
        compiler_params=_cparams(1, 48),
        name="pool_mixer",
    )(z3, pool_w, pool_scale.reshape(1, width))


def _dft_tables(s, c):
    def tab(n):
        r = np.outer(np.arange(n), np.arange(n)) % n
        ang = r * (2.0 * np.pi / n)
        return np.cos(ang), np.sin(ang)
    cs, ss = tab(s)
    cc, sc = tab(c)
    alt = np.zeros((8, s))
    alt[0] = 1.0 - 2.0 * (np.arange(s) % 2)
    const = lambda a: jnp.asarray(a, dtype=F32).astype(BF16)
    return const(cs[:s // 2]), const(ss[:s // 2]), const(alt), const(np.concatenate([cc, sc], axis=1))


def _fourier_kernel(u_ref, cs_ref, ss_ref, alt_ref, ccsc_ref, fw_ref, o_ref, t1_ref, t2_ref, *, norm):
    s = u_ref.shape[1]
    half = s // 2
    ng, c, _ = fw_ref.shape
    for g in range(ng):
        ab = jnp.dot(u_ref[0, :, g * c:(g + 1) * c], ccsc_ref[...], preferred_element_type=F32)
        t1_ref[:, g * c:(g + 1) * c] = ab[:, :c].astype(BF16)
        t2_ref[:, g * c:(g + 1) * c] = ab[:, c:].astype(BF16)
    p = jnp.dot(cs_ref[...], t1_ref[...], preferred_element_type=F32)
    q = jnp.dot(ss_ref[...], t2_ref[...], preferred_element_type=F32)
    mid = jnp.dot(alt_ref[...], t1_ref[...], preferred_element_type=F32)
    low = ((p - q) * norm).astype(BF16)
    mirror = ((p + q) * norm).astype(BF16)
    mid = (mid * norm).astype(BF16)
    rr = lax.broadcasted_iota(jnp.int32, (half, half), 0)
    cc = lax.broadcasted_iota(jnp.int32, (half, half), 1)
    reverse = (rr + cc == half).astype(BF16)
    first = lax.broadcasted_iota(jnp.int32, (half, 1), 0) == 0
    y_mirror, y_mid = [], []
    for g in range(ng):
        cols = slice(g * c, (g + 1) * c)
        w = fw_ref[g].astype(BF16)
        o_ref[0, pl.ds(0, half), cols] = jnp.dot(low[:, cols], w, preferred_element_type=F32).astype(o_ref.dtype)
        y_mirror.append(jnp.dot(mirror[:, cols], w, preferred_element_type=F32).astype(BF16))
        y_mid.append(jnp.dot(mid[:, cols], w, preferred_element_type=F32)[0:1, :])
    y_rev = jnp.dot(reverse, jnp.concatenate(y_mirror, axis=1), preferred_element_type=F32)
    o_ref[0, pl.ds(half, half), :] = jnp.where(first, jnp.concatenate(y_mid, axis=1), y_rev).astype(o_ref.dtype)


def _fourier_mixer(z3, fourier_w):
    b, s, _ = z3.shape
    ng, c, _ = fourier_w.shape
    width = ng * c
    cs, ss, alt, ccsc = _dft_tables(s, c)
    const = lambda shape: pl.BlockSpec(shape, lambda i: (0,) * len(shape), pipeline_mode=pl.Buffered(1))
    return pl.pallas_call(
        functools.partial(_fourier_kernel, norm=float((s * c) ** -0.5)),
        grid=(b,),
        in_specs=[pl.BlockSpec((1, s, width), lambda i: (i, 0, 0)),
                  const((s // 2, s)), const((s // 2, s)), const((8, s)), const((c, 2 * c)),
                  pl.BlockSpec((ng, c, c), lambda i: (0, 0, 0))],
        out_specs=pl.BlockSpec((1, s, width), lambda i: (i, 0, 0)),
        out_shape=jax.ShapeDtypeStruct((b, s, width), BF16),
        scratch_shapes=[pltpu.VMEM((s, width), BF16), pltpu.VMEM((s, width), BF16)],
        compiler_params=_cparams(1, 48),
        name="fourier_mixer",
    )(z3, cs, ss, alt, ccsc, fourier_w)


def _merge_kernel(ya_ref, yb_ref, yc_ref, h_ref, pp_ref, pf_ref, pm_ref, cg0, cg1, cg2, o_ref, cpp, cpf, cpm):
    @pl.when(pl.program_id(1) == 0)
    def _():
        for src, dst in ((pp_ref, cpp), (pf_ref, cpf), (pm_ref, cpm)):
            dst[...] = src[...].astype(BF16)

    for r0 in range(0, h_ref.shape[0], MERGE_ROW_CHUNK):
        rs = pl.ds(r0, MERGE_ROW_CHUNK)
        h = h_ref[rs, :]

        def branch(y_ref, proj, gate_w):
            gate = jax.nn.sigmoid(jnp.dot(h, gate_w[...], preferred_element_type=F32))
            return gate * jnp.dot(y_ref[rs, :], proj[...], preferred_element_type=F32)

        acc = branch(ya_ref, cpp, cg0)
        acc = acc + branch(yb_ref, cpf, cg1)
        acc = acc + branch(yc_ref, cpm, cg2)
        o_ref[rs, :] = acc.astype(o_ref.dtype)


def _gated_merge(ya, yb, yc, h, proj_pool, proj_fourier, proj_mem, w_gates, *, tm, tn):
    m, d = h.shape
    nb = d // tn
    act = lambda width: pl.BlockSpec((tm, width), lambda n, i: (i, 0))
    wsp = lambda rows, off: pl.BlockSpec((rows, tn), lambda n, i: (0, n + off))
    return pl.pallas_call(
        _merge_kernel,
        grid=(d // tn, m // tm),
        in_specs=[act(ya.shape[1]), act(yb.shape[1]), act(yc.shape[1]), act(d),
                  wsp(proj_pool.shape[0], 0), wsp(proj_fourier.shape[0], 0), wsp(proj_mem.shape[0], 0),
                  wsp(d, 0), wsp(d, nb), wsp(d, 2 * nb)],
        out_specs=pl.BlockSpec((tm, tn), lambda n, i: (i, n)),
        out_shape=jax.ShapeDtypeStruct((m, d), BF16),
        scratch_shapes=[pltpu.VMEM((proj_pool.shape[0], tn), BF16), pltpu.VMEM((proj_fourier.shape[0], tn), BF16),
                        pltpu.VMEM((proj_mem.shape[0], tn), BF16)],
        compiler_params=_cparams(2, 56),
        name="gated_merge",
    )(ya, yb, yc, h, proj_pool, proj_fourier, proj_mem, w_gates, w_gates, w_gates)


def _select_kernel(aff_ref, tri_ref, slot_ref, slot_t_ref, starts_ref, bounds_ref, *, cap, tile, block):
    a = aff_ref[...]
    rows = a.shape[0]
    capf = float(cap)

    def count(pred):
        return jnp.sum(pred.astype(F32), axis=-1, keepdims=True)

    def body(i, t_bits):
        cand = t_bits | jnp.left_shift(jnp.int32(1), 30 - i)
        return jnp.where(count(a >= pltpu.bitcast(cand, F32)) >= capf, cand, t_bits)

    t = pltpu.bitcast(lax.fori_loop(0, 31, body, jnp.zeros((rows, 1), jnp.int32)), F32)
    gt = a > t
    eq = a == t
    need = capf - count(gt)
    tri = tri_ref[...]
    eq_rank = jnp.dot(eq.astype(BF16), tri, preferred_element_type=F32)
    sel = gt | (eq & (eq_rank < need))
    pos = jnp.dot(sel.astype(BF16), tri, preferred_element_type=F32)
    slot = jnp.where(sel, pos, -1.0)
    slot_ref[...] = slot
    slot_t_ref[...] = slot.T
    s = a.shape[1]
    lane = lax.broadcasted_iota(jnp.int32, (rows, LANES), 1)
    starts = jnp.where(lane == s // tile, capf, 0.0)
    for t in range(s // tile):
        starts = jnp.where(lane == t, pos[:, t * tile:t * tile + 1], starts)
    starts_ref[...] = starts.astype(jnp.int32)

    nblk = cap // block
    tok = lax.broadcasted_iota(jnp.int32, (1, s), 1).astype(F32)

    def token_of(j):
        return jnp.sum(jnp.where(slot == float(j), tok, 0.0), axis=-1, keepdims=True)

    bounds = jnp.zeros((rows, LANES), F32)
    for i in range(nblk):
        bounds = jnp.where(lane == i, token_of(i * block), bounds)
        bounds = jnp.where(lane == nblk + i, token_of((i + 1) * block - 1) + 1.0, bounds)
    bounds_ref[...] = bounds.astype(jnp.int32)


def _select(aff_rows, *, cap, tile, block):
    rows, s = aff_rows.shape
    assert rows == LANES
    idx = jnp.arange(s, dtype=jnp.int32)
    tri = (idx[:, None] < idx[None, :]).astype(BF16)
    full = lambda shape: pl.BlockSpec(shape, lambda i: (0, 0))
    assert 2 * cap // block <= LANES and s // tile + 1 <= LANES
    return pl.pallas_call(
        functools.partial(_select_kernel, cap=cap, tile=tile, block=block),
        grid=(1,),
        in_specs=[full((rows, s)), full((s, s))],
        out_specs=[full((rows, s)), full((s, rows)), full((rows, LANES)), full((rows, LANES))],
        out_shape=[jax.ShapeDtypeStruct((rows, s), F32), jax.ShapeDtypeStruct((s, rows), F32),
                   jax.ShapeDtypeStruct((rows, LANES), jnp.int32), jax.ShapeDtypeStruct((rows, LANES), jnp.int32)],
        compiler_params=_cparams(1, 48),
        name="expert_select",
    )(aff_rows, tri)


def _dispatch_kernel(bounds_ref, slot_ref, aff_ref, h_ref, o_ref, oa_ref, *, window, group):
    bi = pl.program_id(0)
    ji = pl.program_id(1)
    nblk = pl.num_programs(1)
    ne, block, d = o_ref.shape
    s = h_ref.shape[1]

    lo = hi = None
    for k in range(ne):
        l = bounds_ref[bi * ne + k, ji]
        h = bounds_ref[bi * ne + k, nblk + ji]
        lo = l if lo is None else jnp.minimum(lo, l)
        hi = h if hi is None else jnp.maximum(hi, h)
    w0 = jnp.minimum(jnp.bitwise_and(lo, -BF16_ROWS), s - window)
    fits = hi - w0 <= window

    tok_all = lax.broadcasted_iota(jnp.int32, (1, s), 1).astype(F32)
    slot_ids = (ji * block + lax.broadcasted_iota(jnp.int32, (block, 1), 0)).astype(F32)

    def tokens_of(g0):
        cols = []
        for k in range(g0, g0 + group):
            hit = slot_ref[0, k:k + 1, :] == slot_ids
            cols.append(jnp.sum(jnp.where(hit, tok_all, 0.0), axis=-1, keepdims=True))
            oa_ref[k] = jnp.sum(jnp.where(hit, aff_ref[0, k:k + 1, :], 0.0), axis=-1, keepdims=True)
        return jnp.concatenate(cols, axis=0)

    def gather(width, first_tok, src):
        t = lax.broadcasted_iota(jnp.int32, (group * block, width), 1).astype(F32) + first_tok
        toks = tokens_of(0)
        for g0 in range(0, ne, group):
            onehot = (toks == t).astype(BF16)
            if g0 + group < ne:
                toks = tokens_of(g0 + group)
            out = jnp.dot(onehot, src, preferred_element_type=F32)
            o_ref[g0:g0 + group] = out.reshape(group, block, d).astype(o_ref.dtype)

    @pl.when(fits)
    def _():
        gather(window, w0.astype(F32), h_ref[0, pl.ds(pl.multiple_of(w0, BF16_ROWS), window), :])

    @pl.when(jnp.logical_not(fits))
    def _():
        gather(s, 0.0, h_ref[0])


def _dispatch(bounds, slot3, aff_t, h3, *, n_experts, cap, block, window):
    b, s, d = h3.shape
    assert window % BF16_ROWS == 0 and window <= s and cap % block == 0 and block % BF16_ROWS == 0
    nblk = cap // block
    grid_spec = pltpu.PrefetchScalarGridSpec(
        num_scalar_prefetch=1,
        grid=(b, nblk),
        in_specs=[pl.BlockSpec((1, n_experts, s), lambda i, j, bd: (i, 0, 0)),
                  pl.BlockSpec((1, n_experts, s), lambda i, j, bd: (i, 0, 0)),
                  pl.BlockSpec((1, s, d), lambda i, j, bd: (i, 0, 0))],
        out_specs=[pl.BlockSpec((n_experts, block, d), lambda i, j, bd: (0, i * nblk + j, 0)),
                   pl.BlockSpec((n_experts, block, 1), lambda i, j, bd: (0, i * nblk + j, 0))],
    )
    return pl.pallas_call(
        functools.partial(_dispatch_kernel, window=window, group=DISPATCH_GROUP),
        grid_spec=grid_spec,
        out_shape=[jax.ShapeDtypeStruct((n_experts, b * cap, d), BF16),
                   jax.ShapeDtypeStruct((n_experts, b * cap, 1), F32)],
        compiler_params=_cparams(2, 48),
        name="dispatch",
    )(bounds, slot3, aff_t, h3)


def _combine_kernel(starts_ref, slot_ref, y_hbm, x_ref, g_ref, o_ref, ybuf, sem, *, win, final_norm):
    bi = pl.program_id(0)
    ti = pl.program_id(1)
    tt, d = x_ref.shape
    _, ne, cap, _ = ybuf.shape

    def y_copy(batch, buf):
        return pltpu.make_async_copy(y_hbm.at[:, pl.ds(batch * cap, cap), :], ybuf.at[buf], sem.at[buf])

    @pl.when(ti == 0)
    def _():
        @pl.when(bi == 0)
        def _():
            y_copy(0, 0).start()

        y_copy(bi, jnp.bitwise_and(bi, 1)).wait()

        @pl.when(bi + 1 < pl.num_programs(0))
        def _():
            y_copy(bi + 1, jnp.bitwise_and(bi + 1, 1)).start()

    cur = jnp.bitwise_and(bi, 1)

    wins = []
    fits = None
    for k in range(ne):
        lo = starts_ref[bi * ne + k, ti]
        hi = starts_ref[bi * ne + k, ti + 1]
        w0 = jnp.minimum(jnp.bitwise_and(lo, -BF16_ROWS), cap - win)
        ok = hi - w0 <= win
        fits = ok if fits is None else jnp.logical_and(fits, ok)
        wins.append(w0)

    slots = slot_ref[...].astype(BF16)

    def onehot(width, first_slot):
        shift = width.bit_length() - 1
        n = ne * width
        lane_of = bi * ne + lax.shift_right_logical(lax.broadcasted_iota(jnp.int32, (LANES, n), 1), shift)
        spread = (lax.broadcasted_iota(jnp.int32, (LANES, n), 0) == lane_of).astype(BF16)
        slot_b = jnp.dot(slots, spread, preferred_element_type=F32)
        c = lax.broadcasted_iota(jnp.int32, (1, n), 1)
        target = jnp.bitwise_and(c, width - 1)
        if first_slot is not None:
            kk = lax.shift_right_logical(c, shift)
            for k in range(ne):
                target = target + jnp.where(kk == k, first_slot[k], 0)
        return (slot_b == target.astype(F32)).astype(BF16)

    def finish(contrib):
        x = x_ref[...] + contrib
        if final_norm:
            inv = lax.rsqrt(jnp.mean(x * x, axis=-1, keepdims=True) + EPS)
            x = (x * inv) * g_ref[...]
        o_ref[...] = x

    @pl.when(fits)
    def _():
        ywin = jnp.concatenate([ybuf[cur, k, pl.ds(pl.multiple_of(wins[k], BF16_ROWS), win), :] for k in range(ne)],
                               axis=0)
        finish(jnp.dot(onehot(win, wins), ywin, preferred_element_type=F32))

    @pl.when(jnp.logical_not(fits))
    def _():
        finish(jnp.dot(onehot(cap, None), ybuf[cur].reshape(ne * cap, d), preferred_element_type=F32))


def _combine(starts, slot_cols, y, x1, g, *, batch, cap, tt, win, final_norm):
    m, d = x1.shape
    ne = y.shape[0]
    s = m // batch
    tps = s // tt
    assert slot_cols.shape == (s, LANES) and batch * ne == LANES
    assert win % BF16_ROWS == 0 and win <= cap and win & (win - 1) == 0 and cap & (cap - 1) == 0
    grid_spec = pltpu.PrefetchScalarGridSpec(
        num_scalar_prefetch=1,
        grid=(batch, tps),
        in_specs=[pl.BlockSpec((tt, LANES), lambda b, t, st: (t, 0)),
                  pl.BlockSpec(memory_space=pl.ANY),
                  pl.BlockSpec((tt, d), lambda b, t, st: (b * tps + t, 0)),
                  pl.BlockSpec((1, d), lambda b, t, st: (0, 0))],
        out_specs=pl.BlockSpec((tt, d), lambda b, t, st: (b * tps + t, 0)),
        scratch_shapes=[pltpu.VMEM((2, ne, cap, d), BF16), pltpu.SemaphoreType.DMA((2,))],
    )
    return pl.pallas_call(
        functools.partial(_combine_kernel, win=win, final_norm=final_norm),
        grid_spec=grid_spec,
        out_shape=jax.ShapeDtypeStruct((m, d), F32),
        compiler_params=_cparams(2, 56),
        name="combine",
    )(starts, slot_cols, y, x1, g.reshape(1, d))


def _expert_up_kernel(x_ref, wg_ref, wu_ref, o_ref, *, chunk):
    wg = wg_ref[0].astype(BF16)
    wu = wu_ref[0].astype(BF16)
    for r0 in range(0, x_ref.shape[1], chunk):
        x = x_ref[0, pl.ds(r0, chunk), :]
        gate = jnp.dot(x, wg, preferred_element_type=F32)
        up = jnp.dot(x, wu, preferred_element_type=F32)
        o_ref[0, pl.ds(r0, chunk), :] = (jax.nn.silu(gate) * up).astype(o_ref.dtype)


def _expert_up(xin, w_gate, w_up, *, tf):
    e, m, d = xin.shape
    f = w_gate.shape[2]
    return pl.pallas_call(
        functools.partial(_expert_up_kernel, chunk=512),
        grid=(e, f // tf),
        in_specs=[pl.BlockSpec((1, m, d), lambda i, j: (i, 0, 0)),
                  pl.BlockSpec((1, d, tf), lambda i, j: (i, 0, j)),
                  pl.BlockSpec((1, d, tf), lambda i, j: (i, 0, j))],
        out_specs=pl.BlockSpec((1, m, tf), lambda i, j: (i, 0, j)),
        out_shape=jax.ShapeDtypeStruct((e, m, f), BF16),
        compiler_params=_cparams(2, 56),
        name="expert_up",
    )(xin, w_gate, w_up)


def _expert_down_kernel(h_ref, wd_ref, a_ref, o_ref, *, chunk):
    wd = wd_ref[0].astype(BF16)
    for r0 in range(0, h_ref.shape[1], chunk):
        y = jnp.dot(h_ref[0, pl.ds(r0, chunk), :], wd, preferred_element_type=F32)
        o_ref[0, pl.ds(r0, chunk), :] = (y * a_ref[0, pl.ds(r0, chunk), :]).astype(o_ref.dtype)


def _expert_down(hidden, w_down, aff, *, tn):
    e, m, f = hidden.shape
    d = w_down.shape[2]
    return pl.pallas_call(
        functools.partial(_expert_down_kernel, chunk=512),
        grid=(e, d // tn),
        in_specs=[pl.BlockSpec((1, m, f), lambda i, j: (i, 0, 0)),
                  pl.BlockSpec((1, f, tn), lambda i, j: (i, 0, j)),
                  pl.BlockSpec((1, m, 1), lambda i, j: (i, 0, 0))],
        out_specs=pl.BlockSpec((1, m, tn), lambda i, j: (i, 0, j)),
        out_shape=jax.ShapeDtypeStruct((e, m, d), BF16),
        compiler_params=_cparams(2, 56),
        name="expert_down",
    )(hidden, w_down, aff)


def kernel(x, mem, norm_mix_g, norm_mem_g, w_in, pool_w, pool_scale, fourier_w, w_kv_mem, proj_pool, proj_fourier,
           proj_mem, w_out, norm_ffn_g, w_router, w_expert_gate, w_expert_up, w_expert_down, norm_final_g):
    b, s, d = x.shape
    depth = w_in.shape[0]
    n_tok = b * s
    pool_width = pool_w.shape[1] * pool_w.shape[2]
    four_width = fourier_w.shape[1] * fourier_w.shape[2]
    mem_width = proj_mem.shape[1]
    mix_width = pool_width + four_width + mem_width
    e = w_router.shape[2]
    cap = CAPACITY_FACTOR * s // e

    xf = x.reshape(n_tok, d)
    for l in range(depth):
        kv = _kv_proj(mem.reshape(-1, d), norm_mem_g[l], w_kv_mem[l], tm=KV_TILE[0], tn=KV_TILE[1])
        h, zp, zf, yc, w_gates = _in_proj(xf, norm_mix_g[l], w_in[l], kv.reshape(b, -1, 2 * mem_width),
                                          pool_width=pool_width, four_width=four_width, n_heads=MEM_HEADS,
                                          tm=ROW_TILE)
        ya = _pool_mixer(zp.reshape(b, s, pool_width), pool_w[l], pool_scale[l])
        yb = _fourier_mixer(zf.reshape(b, s, four_width), fourier_w[l])
        merged = _gated_merge(ya.reshape(n_tok, -1), yb.reshape(n_tok, -1), yc, h,
                              proj_pool[l], proj_fourier[l], proj_mem[l], w_gates, tm=MERGE_TILE[0], tn=MERGE_TILE[1])

        x1, h2, aff_t = _out_proj(merged, w_out[l], xf, norm_ffn_g[l], w_router[l], batch=b, tm=ROW_TILE)
        slot_rows, slot_cols, starts, bounds = _select(aff_t.reshape(b * e, s), cap=cap, tile=COMBINE_TILE,
                                                       block=DISPATCH_BLOCK)
        xin, aff_slot = _dispatch(bounds[:, :2 * cap // DISPATCH_BLOCK], slot_rows.reshape(b, e, s), aff_t,
                                  h2.reshape(b, s, d), n_experts=e, cap=cap, block=DISPATCH_BLOCK,
                                  window=DISPATCH_WINDOW)
        hidden = _expert_up(xin, w_expert_gate[l], w_expert_up[l], tf=EXPERT_UP_COLS)
        y = _expert_down(hidden, w_expert_down[l], aff_slot, tn=EXPERT_DOWN_COLS)
        last = l + 1 == depth
        xf = _combine(starts[:, :s // COMBINE_TILE + 1], slot_cols, y, x1, norm_final_g, batch=b, cap=cap,
                      tt=COMBINE_TILE, win=COMBINE_WINDOW, final_norm=last)
    return xf.reshape(b, s, d)
```

```python
import functools

import numpy as np
import jax
import jax.numpy as jnp
from jax import lax
from jax.experimental import pallas as pl
from jax.experimental.pallas import tpu as pltpu

F32 = jnp.float32
BF16 = jnp.bfloat16

EPS = 1e-6
POOL_WINDOWS = (2, 4, 8, 16)
CAPACITY_FACTOR = 2
MEM_HEADS = 4
LANES = 128
BF16_ROWS = 16
POOL_HALO = 32
ROW_CHUNK = 256
MERGE_ROW_CHUNK = 512
GATE_CAST_SLAB = 256
DISPATCH_BLOCK = 64
DISPATCH_WINDOW = 768
DISPATCH_GROUP = 2
COMBINE_TILE = 256
COMBINE_WINDOW = 64
MIB = 1024 * 1024
ROW_TILE = 512
KV_TILE = (1024, 512)
MERGE_TILE = (1024, 512)
EXPERT_UP_COLS = 512
EXPERT_DOWN_COLS = 1024


def _cparams(n_axes, vmem_mib):
    return pltpu.CompilerParams(
        dimension_semantics=("arbitrary",) * n_axes,
        vmem_limit_bytes=vmem_mib * MIB,
    )


def _kv_proj_kernel(x_ref, g_ref, w_ref, o_ref, wc_ref):
    @pl.when(pl.program_id(1) == 0)
    def _():
        wc_ref[...] = w_ref[...].astype(BF16)

    x = x_ref[...]
    inv = lax.rsqrt(jnp.mean(x * x, axis=-1, keepdims=True) + EPS)
    hb = ((x * inv) * g_ref[...]).astype(BF16)
    o_ref[...] = jnp.dot(hb, wc_ref[...], preferred_element_type=F32).astype(o_ref.dtype)


def _kv_proj(x, g, w, *, tm, tn):
    m, k = x.shape
    n = w.shape[1]
    return pl.pallas_call(
        _kv_proj_kernel,
        grid=(n // tn, m // tm),
        in_specs=[pl.BlockSpec((tm, k), lambda c, i: (i, 0)), pl.BlockSpec((1, k), lambda c, i: (0, 0)),
                  pl.BlockSpec((k, tn), lambda c, i: (0, c))],
        out_specs=pl.BlockSpec((tm, tn), lambda c, i: (i, c)),
        out_shape=jax.ShapeDtypeStruct((m, n), BF16),
        scratch_shapes=[pltpu.VMEM((k, tn), BF16)],
        compiler_params=_cparams(2, 48),
        name="kv_proj",
    )(x, g.reshape(1, k), w)


def _in_proj_kernel(x_ref, g_ref, w_ref, kv_ref, wg_ref, h_ref, zp_ref, zf_ref, yc_ref, wgb_ref, wc_ref, *, n_heads):
    @pl.when(pl.program_id(0) == 0)
    def _():
        wc_ref[...] = w_ref[...].astype(BF16)

    wgb_ref[...] = wg_ref[...].astype(BF16)

    wp = zp_ref.shape[1]
    wf = zf_ref.shape[1]
    wm = yc_ref.shape[1]
    dh = wm // n_heads
    scale = float(dh ** -0.5)
    chunks = [pl.ds(r0, ROW_CHUNK) for r0 in range(0, x_ref.shape[0], ROW_CHUNK)]
    queries = []
    for rs in chunks:
        x = x_ref[rs, :]
        inv = lax.rsqrt(jnp.mean(x * x, axis=-1, keepdims=True) + EPS)
        hb = ((x * inv) * g_ref[...]).astype(BF16)
        h_ref[rs, :] = hb
        for c0 in range(0, wp, wf):
            zp_ref[rs, c0:c0 + wf] = jnp.dot(hb, wc_ref[:, c0:c0 + wf], preferred_element_type=F32)
        zf_ref[rs, :] = jnp.dot(hb, wc_ref[:, wp:wp + wf], preferred_element_type=F32).astype(BF16)
        queries.append(jnp.dot(hb, wc_ref[:, wp + wf:wp + wf + wm], preferred_element_type=F32).astype(BF16))
    pairs = [(ci, hd) for ci in range(len(chunks)) for hd in range(n_heads)]
    scores = [lax.dot_general(queries[ci][:, hd * dh:(hd + 1) * dh], kv_ref[0, :, hd * dh:(hd + 1) * dh],
                              (((1,), (1,)), ((), ())), preferred_element_type=F32) * scale for ci, hd in pairs]
    probs = []
    for sc in scores:
        p = jnp.exp(sc - jnp.max(sc, axis=-1, keepdims=True))
        probs.append((p / jnp.sum(p, axis=-1, keepdims=True)).astype(BF16))
    for (ci, hd), p in zip(pairs, probs):
        o = jnp.dot(p, kv_ref[0, :, wm + hd * dh:wm + (hd + 1) * dh], preferred_element_type=F32)
        yc_ref[chunks[ci], hd * dh:(hd + 1) * dh] = o.astype(BF16)


def _in_proj(x, g, w_in, kv3, *, pool_width, four_width, n_heads, tm):
    m, d = x.shape
    bsz, mlen, kvw = kv3.shape
    mem_width = kvw // 2
    mix_width = pool_width + four_width + mem_width
    tps = m // bsz // tm
    gate_width = w_in.shape[1] - mix_width
    slab = GATE_CAST_SLAB
    n_slabs = gate_width // slab
    assert gate_width % slab == 0 and mix_width % slab == 0 and n_slabs <= m // tm
    row = lambda width: pl.BlockSpec((tm, width), lambda i: (i, 0))
    return pl.pallas_call(
        functools.partial(_in_proj_kernel, n_heads=n_heads),
        grid=(m // tm,),
        in_specs=[row(d), pl.BlockSpec((1, d), lambda i: (0, 0)),
                  pl.BlockSpec((d, mix_width), lambda i: (0, 0), pipeline_mode=pl.Buffered(1)),
                  pl.BlockSpec((1, mlen, kvw), lambda i: (i // tps, 0, 0)),
                  pl.BlockSpec((d, slab), lambda i: (0, mix_width // slab + jnp.minimum(i, n_slabs - 1)))],
        out_specs=[row(d), row(pool_width), row(four_width), row(mem_width),
                   pl.BlockSpec((d, slab), lambda i: (0, jnp.minimum(i, n_slabs - 1)))],
        out_shape=[jax.ShapeDtypeStruct((m, d), BF16), jax.ShapeDtypeStruct((m, pool_width), F32),
                   jax.ShapeDtypeStruct((m, four_width), BF16), jax.ShapeDtypeStruct((m, mem_width), BF16),
                   jax.ShapeDtypeStruct((d, gate_width), BF16)],
        scratch_shapes=[pltpu.VMEM((d, mix_width), BF16)],
        compiler_params=_cparams(1, 56),
        name="in_proj",
    )(x, g.reshape(1, d), w_in, kv3, w_in)


def _out_proj_kernel(m_ref, w_ref, r_ref, g_ref, wr_ref, x1_ref, h2_ref, afft_ref, wc_ref, *, chunk, n_experts):
    @pl.when(pl.program_id(0) == 0)
    def _():
        wc_ref[...] = w_ref[...].astype(BF16)

    wr = wr_ref[...].astype(BF16)
    for r0 in range(0, m_ref.shape[0], ROW_CHUNK):
        rs = pl.ds(r0, ROW_CHUNK)
        a = m_ref[rs, :]
        for c0 in range(0, wc_ref.shape[1], chunk):
            cols = slice(c0, c0 + chunk)
            x1_ref[rs, cols] = r_ref[rs, cols] + jnp.dot(a, wc_ref[:, cols], preferred_element_type=F32)

        x = x1_ref[rs, :]
        inv = lax.rsqrt(jnp.mean(x * x, axis=-1, keepdims=True) + EPS)
        hb = ((x * inv) * g_ref[...]).astype(BF16)
        h2_ref[rs, :] = hb
        logits = jnp.dot(hb, wr, preferred_element_type=F32)
        lane = lax.broadcasted_iota(jnp.int32, logits.shape, 1)
        logits = jnp.where(lane < n_experts, logits, -1e30)
        mx = jnp.max(logits, axis=-1, keepdims=True)
        p = jnp.exp(logits - mx)
        aff = p / jnp.sum(p, axis=-1, keepdims=True)
        afft_ref[0, :, rs] = aff.T[:n_experts, :]


def _out_proj(merged, w_out, resid, g, w_router, *, batch, tm):
    m, d = resid.shape
    e = w_router.shape[1]
    s = m // batch
    tps = s // tm
    wr = jnp.pad(w_router, ((0, 0), (0, LANES - e)))
    row = pl.BlockSpec((tm, d), lambda i: (i, 0))
    return pl.pallas_call(
        functools.partial(_out_proj_kernel, chunk=512, n_experts=e),
        grid=(m // tm,),
        in_specs=[row, pl.BlockSpec((d, d), lambda i: (0, 0), pipeline_mode=pl.Buffered(1)), row,
                  pl.BlockSpec((1, d), lambda i: (0, 0)), pl.BlockSpec((d, LANES), lambda i: (0, 0))],
        out_specs=[row, row, pl.BlockSpec((1, e, tm), lambda i: (i // tps, 0, i % tps))],
        out_shape=[jax.ShapeDtypeStruct((m, d), F32), jax.ShapeDtypeStruct((m, d), BF16),
                   jax.ShapeDtypeStruct((batch, e, s), F32)],
        scratch_shapes=[pltpu.VMEM((d, d), BF16)],
        compiler_params=_cparams(1, 56),
        name="out_proj",
    )(merged, w_out, resid, g.reshape(1, d), wr)


def _pool_kernel(u_ref, pw_ref, ps_ref, o_ref, a_ref, b_ref, *, chunk):
    s = u_ref.shape[1]
    c = pw_ref.shape[1]
    h = POOL_HALO
    rows = s + 2 * h
    zeros = jnp.zeros((h, c), F32)

    def level(src, dst, off_lo, off_hi, margin):
        for r0 in range(margin, rows - margin, chunk):
            n = min(chunk, rows - margin - r0)
            dst[pl.ds(r0, n), :] = src[pl.ds(r0 + off_lo, n), :] + src[pl.ds(r0 + off_hi, n), :]

    for g, w in enumerate(POOL_WINDOWS):
        cols = slice(g * c, (g + 1) * c)
        a_ref[pl.ds(0, h), :] = zeros
        a_ref[pl.ds(h + s, h), :] = zeros
        a_ref[pl.ds(h, s), :] = u_ref[0, :, cols]
        src, off_lo, off_hi = a_ref, -1, 0
        if w >= 4:
            level(a_ref, b_ref, -1, 0, 8)
            src, off_lo, off_hi = b_ref, -1, 1
        if w >= 8:
            level(b_ref, a_ref, -1, 1, 16)
            src, off_lo, off_hi = a_ref, -2, 2
        if w >= 16:
            level(a_ref, b_ref, -2, 2, 24)
            src, off_lo, off_hi = b_ref, -4, 4
        wg = pw_ref[g].astype(BF16)
        scale = ps_ref[:, cols]
        for r0 in range(0, s, chunk):
            pos = r0 + lax.broadcasted_iota(jnp.int32, (chunk, 1), 0)
            lo = jnp.maximum(pos - w // 2, 0)
            hi = jnp.minimum(pos + (w - w // 2), s)
            cnt = (hi - lo).astype(F32)
            tot = src[pl.ds(h + r0 + off_lo, chunk), :] + src[pl.ds(h + r0 + off_hi, chunk), :]
            pooled = tot / cnt - u_ref[0, pl.ds(r0, chunk), cols]
            y = jnp.dot(pooled.astype(BF16), wg, preferred_element_type=F32)
            o_ref[0, pl.ds(r0, chunk), cols] = (y * scale).astype(o_ref.dtype)


def _pool_mixer(z3, pool_w, pool_scale):
    b, s, _ = z3.shape
    g, c, _ = pool_w.shape
    width = g * c
    assert POOL_WINDOWS == (2, 4, 8, 16) and g == len(POOL_WINDOWS)
    return pl.pallas_call(
        functools.partial(_pool_kernel, chunk=256),
        grid=(b,),
        in_specs=[pl.BlockSpec((1, s, width), lambda i: (i, 0, 0)),
                  pl.BlockSpec((g, c, c), lambda i: (0, 0, 0)),
                  pl.BlockSpec((1, width), lambda i: (0, 0))],
        out_specs=pl.BlockSpec((1, s, width), lambda i: (i, 0, 0)),
        out_shape=jax.ShapeDtypeStruct((b, s, width), BF16),
        scratch_shapes=[pltpu.VMEM((s + 2 * POOL_HALO, c), F32), pltpu.VMEM((s + 2 * POOL_HALO, c), F32)],
        compiler_params=_cparams(1, 48),
        name="pool_mixer",
    )(z3, pool_w, pool_scale.reshape(1, width))


def _dft_tables(s, c):
    def tab(n):
        r = np.outer(np.arange(n), np.arange(n)) % n
        ang = r * (2.0 * np.pi / n)
        return np.cos(ang), np.sin(ang)
    cs, ss = tab(s)
    cc, sc = tab(c)
    alt = np.zeros((8, s))
    alt[0] = 1.0 - 2.0 * (np.arange(s) % 2)
    const = lambda a: jnp.asarray(a, dtype=F32).astype(BF16)
    return const(cs[:s // 2]), const(ss[:s // 2]), const(alt), const(np.concatenate([cc, sc], axis=1))


def _fourier_kernel(u_ref, cs_ref, ss_ref, alt_ref, ccsc_ref, fw_ref, o_ref, t1_ref, t2_ref, *, norm):
    s = u_ref.shape[1]
    half = s // 2
    ng, c, _ = fw_ref.shape
    for g in range(ng):
        ab = jnp.dot(u_ref[0, :, g * c:(g + 1) * c], ccsc_ref[...], preferred_element_type=F32)
        t1_ref[:, g * c:(g + 1) * c] = ab[:, :c].astype(BF16)
        t2_ref[:, g * c:(g + 1) * c] = ab[:, c:].astype(BF16)
    p = jnp.dot(cs_ref[...], t1_ref[...], preferred_element_type=F32)
    q = jnp.dot(ss_ref[...], t2_ref[...], preferred_element_type=F32)
    mid = jnp.dot(alt_ref[...], t1_ref[...], preferred_element_type=F32)
    low = ((p - q) * norm).astype(BF16)
    mirror = ((p + q) * norm).astype(BF16)
    mid = (mid * norm).astype(BF16)
    rr = lax.broadcasted_iota(jnp.int32, (half, half), 0)
    cc = lax.broadcasted_iota(jnp.int32, (half, half), 1)
    reverse = (rr + cc == half).astype(BF16)
    first = lax.broadcasted_iota(jnp.int32, (half, 1), 0) == 0
    y_mirror, y_mid = [], []
    for g in range(ng):
        cols = slice(g * c, (g + 1) * c)
        w = fw_ref[g].astype(BF16)
        o_ref[0, pl.ds(0, half), cols] = jnp.dot(low[:, cols], w, preferred_element_type=F32).astype(o_ref.dtype)
        y_mirror.append(jnp.dot(mirror[:, cols], w, preferred_element_type=F32).astype(BF16))
        y_mid.append(jnp.dot(mid[:, cols], w, preferred_element_type=F32)[0:1, :])
    y_rev = jnp.dot(reverse, jnp.concatenate(y_mirror, axis=1), preferred_element_type=F32)
    o_ref[0, pl.ds(half, half), :] = jnp.where(first, jnp.concatenate(y_mid, axis=1), y_rev).astype(o_ref.dtype)


def _fourier_mixer(z3, fourier_w):
    b, s, _ = z3.shape
    ng, c, _ = fourier_w.shape
    width = ng * c
    cs, ss, alt, ccsc = _dft_tables(s, c)
    const = lambda shape: pl.BlockSpec(shape, lambda i: (0,) * len(shape), pipeline_mode=pl.Buffered(1))
    return pl.pallas_call(
        functools.partial(_fourier_kernel, norm=float((s * c) ** -0.5)),
        grid=(b,),
        in_specs=[pl.BlockSpec((1, s, width), lambda i: (i, 0, 0)),
                  const((s // 2, s)), const((s // 2, s)), const((8, s)), const((c, 2 * c)),
                  pl.BlockSpec((ng, c, c), lambda i: (0, 0, 0))],
        out_specs=pl.BlockSpec((1, s, width), lambda i: (i, 0, 0)),
        out_shape=jax.ShapeDtypeStruct((b, s, width), BF16),
        scratch_shapes=[pltpu.VMEM((s, width), BF16), pltpu.VMEM((s, width), BF16)],
        compiler_params=_cparams(1, 48),
        name="fourier_mixer",
    )(z3, cs, ss, alt, ccsc, fourier_w)


def _merge_kernel(ya_ref, yb_ref, yc_ref, h_ref, pp_ref, pf_ref, pm_ref, cg0, cg1, cg2, o_ref, cpp, cpf, cpm):
    @pl.when(pl.program_id(1) == 0)
    def _():
        for src, dst in ((pp_ref, cpp), (pf_ref, cpf), (pm_ref, cpm)):
            dst[...] = src[...].astype(BF16)

    for r0 in range(0, h_ref.shape[0], MERGE_ROW_CHUNK):
        rs = pl.ds(r0, MERGE_ROW_CHUNK)
        h = h_ref[rs, :]

        def branch(y_ref, proj, gate_w):
            gate = jax.nn.sigmoid(jnp.dot(h, gate_w[...], preferred_element_type=F32))
            return gate * jnp.dot(y_ref[rs, :], proj[...], preferred_element_type=F32)

        acc = branch(ya_ref, cpp, cg0)
        acc = acc + branch(yb_ref, cpf, cg1)
        acc = acc + branch(yc_ref, cpm, cg2)
        o_ref[rs, :] = acc.astype(o_ref.dtype)


def _gated_merge(ya, yb, yc, h, proj_pool, proj_fourier, proj_mem, w_gates, *, tm, tn):
    m, d = h.shape
    nb = d // tn
    act = lambda width: pl.BlockSpec((tm, width), lambda n, i: (i, 0))
    wsp = lambda rows, off: pl.BlockSpec((rows, tn), lambda n, i: (0, n + off))
    return pl.pallas_call(
        _merge_kernel,
        grid=(d // tn, m // tm),
        in_specs=[act(ya.shape[1]), act(yb.shape[1]), act(yc.shape[1]), act(d),
                  wsp(proj_pool.shape[0], 0), wsp(proj_fourier.shape[0], 0), wsp(proj_mem.shape[0], 0),
                  wsp(d, 0), wsp(d, nb), wsp(d, 2 * nb)],
        out_specs=pl.BlockSpec((tm, tn), lambda n, i: (i, n)),
        out_shape=jax.ShapeDtypeStruct((m, d), BF16),
        scratch_shapes=[pltpu.VMEM((proj_pool.shape[0], tn), BF16), pltpu.VMEM((proj_fourier.shape[0], tn), BF16),
                        pltpu.VMEM((proj_mem.shape[0], tn), BF16)],
        compiler_params=_cparams(2, 56),
        name="gated_merge",
    )(ya, yb, yc, h, proj_pool, proj_fourier, proj_mem, w_gates, w_gates, w_gates)


def _select_kernel(aff_ref, tri_ref, slot_ref, slot_t_ref, starts_ref, bounds_ref, *, cap, tile, block):
    a = aff_ref[...]
    rows = a.shape[0]
    capf = float(cap)

    def count(pred):
        return jnp.sum(pred.astype(F32), axis=-1, keepdims=True)

    def body(i, t_bits):
        cand = t_bits | jnp.left_shift(jnp.int32(1), 30 - i)
        return jnp.where(count(a >= pltpu.bitcast(cand, F32)) >= capf, cand, t_bits)

    t = pltpu.bitcast(lax.fori_loop(0, 31, body, jnp.zeros((rows, 1), jnp.int32)), F32)
    gt = a > t
    eq = a == t
    need = capf - count(gt)
    tri = tri_ref[...]
    eq_rank = jnp.dot(eq.astype(BF16), tri, preferred_element_type=F32)
    sel = gt | (eq & (eq_rank < need))
    pos = jnp.dot(sel.astype(BF16), tri, preferred_element_type=F32)
    slot = jnp.where(sel, pos, -1.0)
    slot_ref[...] = slot
    slot_t_ref[...] = slot.T
    s = a.shape[1]
    lane = lax.broadcasted_iota(jnp.int32, (rows, LANES), 1)
    starts = jnp.where(lane == s // tile, capf, 0.0)
    for t in range(s // tile):
        starts = jnp.where(lane == t, pos[:, t * tile:t * tile + 1], starts)
    starts_ref[...] = starts.astype(jnp.int32)

    nblk = cap // block
    tok = lax.broadcasted_iota(jnp.int32, (1, s), 1).astype(F32)

    def token_of(j):
        return jnp.sum(jnp.where(slot == float(j), tok, 0.0), axis=-1, keepdims=True)

    bounds = jnp.zeros((rows, LANES), F32)
    for i in range(nblk):
        bounds = jnp.where(lane == i, token_of(i * block), bounds)
        bounds = jnp.where(lane == nblk + i, token_of((i + 1) * block - 1) + 1.0, bounds)
    bounds_ref[...] = bounds.astype(jnp.int32)


def _select(aff_rows, *, cap, tile, block):
    rows, s = aff_rows.shape
    assert rows == LANES
    idx = jnp.arange(s, dtype=jnp.int32)
    tri = (idx[:, None] < idx[None, :]).astype(BF16)
    full = lambda shape: pl.BlockSpec(shape, lambda i: (0, 0))
    assert 2 * cap // block <= LANES and s // tile + 1 <= LANES
    return pl.pallas_call(
        functools.partial(_select_kernel, cap=cap, tile=tile, block=block),
        grid=(1,),
        in_specs=[full((rows, s)), full((s, s))],
        out_specs=[full((rows, s)), full((s, rows)), full((rows, LANES)), full((rows, LANES))],
        out_shape=[jax.ShapeDtypeStruct((rows, s), F32), jax.ShapeDtypeStruct((s, rows), F32),
                   jax.ShapeDtypeStruct((rows, LANES), jnp.int32), jax.ShapeDtypeStruct((rows, LANES), jnp.int32)],
        compiler_params=_cparams(1, 48),
        name="expert_select",
    )(aff_rows, tri)


def _dispatch_kernel(bounds_ref, slot_ref, aff_ref, h_ref, o_ref, oa_ref, *, window, group):
    bi = pl.program_id(0)
    ji = pl.program_id(1)
    nblk = pl.num_programs(1)
    ne, block, d = o_ref.shape
    s = h_ref.shape[1]

    lo = hi = None
    for k in range(ne):
        l = bounds_ref[bi * ne + k, ji]
        h = bounds_ref[bi * ne + k, nblk + ji]
        lo = l if lo is None else jnp.minimum(lo, l)
        hi = h if hi is None else jnp.maximum(hi, h)
    w0 = jnp.minimum(jnp.bitwise_and(lo, -BF16_ROWS), s - window)
    fits = hi - w0 <= window

    tok_all = lax.broadcasted_iota(jnp.int32, (1, s), 1).astype(F32)
    slot_ids = (ji * block + lax.broadcasted_iota(jnp.int32, (block, 1), 0)).astype(F32)

    def tokens_of(g0):
        cols = []
        for k in range(g0, g0 + group):
            hit = slot_ref[0, k:k + 1, :] == slot_ids
            cols.append(jnp.sum(jnp.where(hit, tok_all, 0.0), axis=-1, keepdims=True))
            oa_ref[k] = jnp.sum(jnp.where(hit, aff_ref[0, k:k + 1, :], 0.0), axis=-1, keepdims=True)
        return jnp.concatenate(cols, axis=0)

    def gather(width, first_tok, src):
        t = lax.broadcasted_iota(jnp.int32, (group * block, width), 1).astype(F32) + first_tok
        toks = tokens_of(0)
        for g0 in range(0, ne, group):
            onehot = (toks == t).astype(BF16)
            if g0 + group < ne:
                toks = tokens_of(g0 + group)
            out = jnp.dot(onehot, src, preferred_element_type=F32)
            o_ref[g0:g0 + group] = out.reshape(group, block, d).astype(o_ref.dtype)

    @pl.when(fits)
    def _():
        gather(window, w0.astype(F32), h_ref[0, pl.ds(pl.multiple_of(w0, BF16_ROWS), window), :])

    @pl.when(jnp.logical_not(fits))
    def _():
        gather(s, 0.0, h_ref[0])


def _dispatch(bounds, slot3, aff_t, h3, *, n_experts, cap, block, window):
    b, s, d = h3.shape
    assert window % BF16_ROWS == 0 and window <= s and cap % block == 0 and block % BF16_ROWS == 0
    nblk = cap // block
    grid_spec = pltpu.PrefetchScalarGridSpec(
        num_scalar_prefetch=1,
        grid=(b, nblk),
        in_specs=[pl.BlockSpec((1, n_experts, s), lambda i, j, bd: (i, 0, 0)),
                  pl.BlockSpec((1, n_experts, s), lambda i, j, bd: (i, 0, 0)),
                  pl.BlockSpec((1, s, d), lambda i, j, bd: (i, 0, 0))],
        out_specs=[pl.BlockSpec((n_experts, block, d), lambda i, j, bd: (0, i * nblk + j, 0)),
                   pl.BlockSpec((n_experts, block, 1), lambda i, j, bd: (0, i * nblk + j, 0))],
    )
    return pl.pallas_call(
        functools.partial(_dispatch_kernel, window=window, group=DISPATCH_GROUP),
        grid_spec=grid_spec,
        out_shape=[jax.ShapeDtypeStruct((n_experts, b * cap, d), BF16),
                   jax.ShapeDtypeStruct((n_experts, b * cap, 1), F32)],
        compiler_params=_cparams(2, 48),
        name="dispatch",
    )(bounds, slot3, aff_t, h3)


def _combine_kernel(starts_ref, slot_ref, y_hbm, x_ref, g_ref, o_ref, ybuf, sem, *, win, n_tiles, final_norm):
    bi = pl.program_id(0)
    ti = pl.program_id(1)
    tt, d = x_ref.shape
    _, ne, cap, _ = ybuf.shape

    per = ne // n_tiles

    def y_copy(batch, group):
        buf = jnp.bitwise_and(batch, 1)
        return pltpu.make_async_copy(y_hbm.at[pl.ds(group * per, per), pl.ds(batch * cap, cap), :],
                                     ybuf.at[buf, pl.ds(group * per, per)], sem.at[buf])

    @pl.when(jnp.logical_and(bi == 0, ti == 0))
    def _():
        for grp in range(n_tiles):
            y_copy(0, grp).start()

    @pl.when(ti == 0)
    def _():
        for grp in range(n_tiles):
            y_copy(bi, grp).wait()

    @pl.when(bi + 1 < pl.num_programs(0))
    def _():
        y_copy(bi + 1, ti).start()

    cur = jnp.bitwise_and(bi, 1)

    wins = []
    fits = None
    for k in range(ne):
        lo = starts_ref[bi * ne + k, ti]
        hi = starts_ref[bi * ne + k, ti + 1]
        w0 = jnp.minimum(jnp.bitwise_and(lo, -BF16_ROWS), cap - win)
        ok = hi - w0 <= win
        fits = ok if fits is None else jnp.logical_and(fits, ok)
        wins.append(w0)

    slots = slot_ref[...].astype(BF16)

    def onehot(width, first_slot):
        shift = width.bit_length() - 1
        n = ne * width
        lane_of = bi * ne + lax.shift_right_logical(lax.broadcasted_iota(jnp.int32, (LANES, n), 1), shift)
        spread = (lax.broadcasted_iota(jnp.int32, (LANES, n), 0) == lane_of).astype(BF16)
        slot_b = jnp.dot(slots, spread, preferred_element_type=F32)
        c = lax.broadcasted_iota(jnp.int32, (1, n), 1)
        target = jnp.bitwise_and(c, width - 1)
        if first_slot is not None:
            kk = lax.shift_right_logical(c, shift)
            for k in range(ne):
                target = target + jnp.where(kk == k, first_slot[k], 0)
        return (slot_b == target.astype(F32)).astype(BF16)

    def finish(contrib):
        x = x_ref[...] + contrib
        if final_norm:
            inv = lax.rsqrt(jnp.mean(x * x, axis=-1, keepdims=True) + EPS)
            x = (x * inv) * g_ref[...]
        o_ref[...] = x

    @pl.when(fits)
    def _():
        ywin = jnp.concatenate([ybuf[cur, k, pl.ds(pl.multiple_of(wins[k], BF16_ROWS), win), :] for k in range(ne)],
                               axis=0)
        finish(jnp.dot(onehot(win, wins), ywin, preferred_element_type=F32))

    @pl.when(jnp.logical_not(fits))
    def _():
        finish(jnp.dot(onehot(cap, None), ybuf[cur].reshape(ne * cap, d), preferred_element_type=F32))


def _combine(starts, slot_cols, y, x1, g, *, batch, cap, tt, win, final_norm):
    m, d = x1.shape
    ne = y.shape[0]
    s = m // batch
    tps = s // tt
    assert slot_cols.shape == (s, LANES) and batch * ne == LANES
    assert win % BF16_ROWS == 0 and win <= cap and win & (win - 1) == 0 and cap & (cap - 1) == 0 and ne % tps == 0
    grid_spec = pltpu.PrefetchScalarGridSpec(
        num_scalar_prefetch=1,
        grid=(batch, tps),
        in_specs=[pl.BlockSpec((tt, LANES), lambda b, t, st: (t, 0)),
                  pl.BlockSpec(memory_space=pl.ANY),
                  pl.BlockSpec((tt, d), lambda b, t, st: (b * tps + t, 0)),
                  pl.BlockSpec((1, d), lambda b, t, st: (0, 0))],
        out_specs=pl.BlockSpec((tt, d), lambda b, t, st: (b * tps + t, 0)),
        scratch_shapes=[pltpu.VMEM((2, ne, cap, d), BF16), pltpu.SemaphoreType.DMA((2,))],
    )
    return pl.pallas_call(
        functools.partial(_combine_kernel, win=win, n_tiles=tps, final_norm=final_norm),
        grid_spec=grid_spec,
        out_shape=jax.ShapeDtypeStruct((m, d), F32),
        compiler_params=_cparams(2, 56),
        name="combine",
    )(starts, slot_cols, y, x1, g.reshape(1, d))


def _expert_up_kernel(x_ref, wg_ref, wu_ref, o_ref, *, chunk):
    wg = wg_ref[0].astype(BF16)
    wu = wu_ref[0].astype(BF16)
    for r0 in range(0, x_ref.shape[1], chunk):
        x = x_ref[0, pl.ds(r0, chunk), :]
        gate = jnp.dot(x, wg, preferred_element_type=F32)
        up = jnp.dot(x, wu, preferred_element_type=F32)
        o_ref[0, pl.ds(r0, chunk), :] = (jax.nn.silu(gate) * up).astype(o_ref.dtype)


def _expert_up(xin, w_gate, w_up, *, tf):
    e, m, d = xin.shape
    f = w_gate.shape[2]
    return pl.pallas_call(
        functools.partial(_expert_up_kernel, chunk=512),
        grid=(e, f // tf),
        in_specs=[pl.BlockSpec((1, m, d), lambda i, j: (i, 0, 0)),
                  pl.BlockSpec((1, d, tf), lambda i, j: (i, 0, j)),
                  pl.BlockSpec((1, d, tf), lambda i, j: (i, 0, j))],
        out_specs=pl.BlockSpec((1, m, tf), lambda i, j: (i, 0, j)),
        out_shape=jax.ShapeDtypeStruct((e, m, f), BF16),
        compiler_params=_cparams(2, 56),
        name="expert_up",
    )(xin, w_gate, w_up)


def _expert_down_kernel(h_ref, wd_ref, a_ref, o_ref, *, chunk):
    wd = wd_ref[0].astype(BF16)
    for r0 in range(0, h_ref.shape[1], chunk):
        y = jnp.dot(h_ref[0, pl.ds(r0, chunk), :], wd, preferred_element_type=F32)
        o_ref[0, pl.ds(r0, chunk), :] = (y * a_ref[0, pl.ds(r0, chunk), :]).astype(o_ref.dtype)


def _expert_down(hidden, w_down, aff, *, tn):
    e, m, f = hidden.shape
    d = w_down.shape[2]
    return pl.pallas_call(
        functools.partial(_expert_down_kernel, chunk=512),
        grid=(e, d // tn),
        in_specs=[pl.BlockSpec((1, m, f), lambda i, j: (i, 0, 0)),
                  pl.BlockSpec((1, f, tn), lambda i, j: (i, 0, j)),
                  pl.BlockSpec((1, m, 1), lambda i, j: (i, 0, 0))],
        out_specs=pl.BlockSpec((1, m, tn), lambda i, j: (i, 0, j)),
        out_shape=jax.ShapeDtypeStruct((e, m, d), BF16),
        compiler_params=_cparams(2, 56),
        name="expert_down",
    )(hidden, w_down, aff)


def kernel(x, mem, norm_mix_g, norm_mem_g, w_in, pool_w, pool_scale, fourier_w, w_kv_mem, proj_pool, proj_fourier,
           proj_mem, w_out, norm_ffn_g, w_router, w_expert_gate, w_expert_up, w_expert_down, norm_final_g):
    b, s, d = x.shape
    depth = w_in.shape[0]
    n_tok = b * s
    pool_width = pool_w.shape[1] * pool_w.shape[2]
    four_width = fourier_w.shape[1] * fourier_w.shape[2]
    mem_width = proj_mem.shape[1]
    mix_width = pool_width + four_width + mem_width
    e = w_router.shape[2]
    cap = CAPACITY_FACTOR * s // e

    xf = x.reshape(n_tok, d)
    for l in range(depth):
        kv = _kv_proj(mem.reshape(-1, d), norm_mem_g[l], w_kv_mem[l], tm=KV_TILE[0], tn=KV_TILE[1])
        h, zp, zf, yc, w_gates = _in_proj(xf, norm_mix_g[l], w_in[l], kv.reshape(b, -1, 2 * mem_width),
                                          pool_width=pool_width, four_width=four_width, n_heads=MEM_HEADS,
                                          tm=ROW_TILE)
        ya = _pool_mixer(zp.reshape(b, s, pool_width), pool_w[l], pool_scale[l])
        yb = _fourier_mixer(zf.reshape(b, s, four_width), fourier_w[l])
        merged = _gated_merge(ya.reshape(n_tok, -1), yb.reshape(n_tok, -1), yc, h,
                              proj_pool[l], proj_fourier[l], proj_mem[l], w_gates, tm=MERGE_TILE[0], tn=MERGE_TILE[1])

        x1, h2, aff_t = _out_proj(merged, w_out[l], xf, norm_ffn_g[l], w_router[l], batch=b, tm=ROW_TILE)
        slot_rows, slot_cols, starts, bounds = _select(aff_t.reshape(b * e, s), cap=cap, tile=COMBINE_TILE,
                                                       block=DISPATCH_BLOCK)
        xin, aff_slot = _dispatch(bounds[:, :2 * cap // DISPATCH_BLOCK], slot_rows.reshape(b, e, s), aff_t,
                                  h2.reshape(b, s, d), n_experts=e, cap=cap, block=DISPATCH_BLOCK,
                                  window=DISPATCH_WINDOW)
        hidden = _expert_up(xin, w_expert_gate[l], w_expert_up[l], tf=EXPERT_UP_COLS)
        y = _expert_down(hidden, w_expert_down[l], aff_slot, tn=EXPERT_DOWN_COLS)
        last = l + 1 == depth
        xf = _combine(starts[:, :s // COMBINE_TILE + 1], slot_cols, y, x1, norm_final_g, batch=b, cap=cap,
                      tt=COMBINE_TILE, win=COMBINE_WINDOW, final_norm=last)
    return xf.reshape(b, s, d)
```

```python
import functools

import numpy as np
import jax
import jax.numpy as jnp
from jax import lax
from jax.experimental import pallas as pl
from jax.experimental.pallas import tpu as pltpu

F32 = jnp.float32
BF16 = jnp.bfloat16

EPS = 1e-6
POOL_WINDOWS = (2, 4, 8, 16)
CAPACITY_FACTOR = 2
MEM_HEADS = 4
LANES = 128
BF16_ROWS = 16
POOL_HALO = 32
ROW_CHUNK = 256
MERGE_ROW_CHUNK = 512
GATE_CAST_SLAB = 256
DISPATCH_BLOCK = 64
DISPATCH_WINDOW = 768
DISPATCH_GROUP = 2
COMBINE_TILE = 256
COMBINE_WINDOW = 64
MIB = 1024 * 1024
ROW_TILE = 512
KV_TILE = (1024, 512)
MERGE_TILE = (1024, 512)
EXPERT_UP_COLS = 512
EXPERT_DOWN_COLS = 1024


def _cparams(n_axes, vmem_mib):
    return pltpu.CompilerParams(
        dimension_semantics=("arbitrary",) * n_axes,
        vmem_limit_bytes=vmem_mib * MIB,
    )


def _kv_proj_kernel(x_ref, g_ref, w_ref, o_ref, wc_ref):
    @pl.when(pl.program_id(1) == 0)
    def _():
        wc_ref[...] = w_ref[...].astype(BF16)

    x = x_ref[...]
    inv = lax.rsqrt(jnp.mean(x * x, axis=-1, keepdims=True) + EPS)
    hb = ((x * inv) * g_ref[...]).astype(BF16)
    o_ref[...] = jnp.dot(hb, wc_ref[...], preferred_element_type=F32).astype(o_ref.dtype)


def _kv_proj(x, g, w, *, tm, tn):
    m, k = x.shape
    n = w.shape[1]
    return pl.pallas_call(
        _kv_proj_kernel,
        grid=(n // tn, m // tm),
        in_specs=[pl.BlockSpec((tm, k), lambda c, i: (i, 0)), pl.BlockSpec((1, k), lambda c, i: (0, 0)),
                  pl.BlockSpec((k, tn), lambda c, i: (0, c))],
        out_specs=pl.BlockSpec((tm, tn), lambda c, i: (i, c)),
        out_shape=jax.ShapeDtypeStruct((m, n), BF16),
        scratch_shapes=[pltpu.VMEM((k, tn), BF16)],
        compiler_params=_cparams(2, 48),
        name="kv_proj",
    )(x, g.reshape(1, k), w)


def _in_proj_kernel(x_ref, g_ref, w_ref, kv_ref, wg_ref, h_ref, zp_ref, zf_ref, yc_ref, wgb_ref, wc_ref, *, n_heads):
    @pl.when(pl.program_id(0) == 0)
    def _():
        wc_ref[...] = w_ref[...].astype(BF16)

    wgb_ref[...] = wg_ref[...].astype(BF16)

    wp = zp_ref.shape[1]
    wf = zf_ref.shape[1]
    wm = yc_ref.shape[1]
    dh = wm // n_heads
    scale = float(dh ** -0.5)
    chunks = [pl.ds(r0, ROW_CHUNK) for r0 in range(0, x_ref.shape[0], ROW_CHUNK)]
    queries = []
    for rs in chunks:
        x = x_ref[rs, :]
        inv = lax.rsqrt(jnp.mean(x * x, axis=-1, keepdims=True) + EPS)
        hb = ((x * inv) * g_ref[...]).astype(BF16)
        h_ref[rs, :] = hb
        for c0 in range(0, wp, wf):
            zp_ref[rs, c0:c0 + wf] = jnp.dot(hb, wc_ref[:, c0:c0 + wf], preferred_element_type=F32)
        zf_ref[rs, :] = jnp.dot(hb, wc_ref[:, wp:wp + wf], preferred_element_type=F32).astype(BF16)
        queries.append(jnp.dot(hb, wc_ref[:, wp + wf:wp + wf + wm], preferred_element_type=F32).astype(BF16))
    pairs = [(ci, hd) for ci in range(len(chunks)) for hd in range(n_heads)]
    scores = [lax.dot_general(queries[ci][:, hd * dh:(hd + 1) * dh], kv_ref[0, :, hd * dh:(hd + 1) * dh],
                              (((1,), (1,)), ((), ())), preferred_element_type=F32) * scale for ci, hd in pairs]
    probs = []
    for sc in scores:
        p = jnp.exp(sc - jnp.max(sc, axis=-1, keepdims=True))
        probs.append((p / jnp.sum(p, axis=-1, keepdims=True)).astype(BF16))
    for (ci, hd), p in zip(pairs, probs):
        o = jnp.dot(p, kv_ref[0, :, wm + hd * dh:wm + (hd + 1) * dh], preferred_element_type=F32)
        yc_ref[chunks[ci], hd * dh:(hd + 1) * dh] = o.astype(BF16)


def _in_proj(x, g, w_in, kv3, *, pool_width, four_width, n_heads, tm):
    m, d = x.shape
    bsz, mlen, kvw = kv3.shape
    mem_width = kvw // 2
    mix_width = pool_width + four_width + mem_width
    tps = m // bsz // tm
    gate_width = w_in.shape[1] - mix_width
    slab = GATE_CAST_SLAB
    n_slabs = gate_width // slab
    assert gate_width % slab == 0 and mix_width % slab == 0 and n_slabs <= m // tm
    row = lambda width: pl.BlockSpec((tm, width), lambda i: (i, 0))
    return pl.pallas_call(
        functools.partial(_in_proj_kernel, n_heads=n_heads),
        grid=(m // tm,),
        in_specs=[row(d), pl.BlockSpec((1, d), lambda i: (0, 0)),
                  pl.BlockSpec((d, mix_width), lambda i: (0, 0), pipeline_mode=pl.Buffered(1)),
                  pl.BlockSpec((1, mlen, kvw), lambda i: (i // tps, 0, 0)),
                  pl.BlockSpec((d, slab), lambda i: (0, mix_width // slab + jnp.minimum(i, n_slabs - 1)))],
        out_specs=[row(d), row(pool_width), row(four_width), row(mem_width),
                   pl.BlockSpec((d, slab), lambda i: (0, jnp.minimum(i, n_slabs - 1)))],
        out_shape=[jax.ShapeDtypeStruct((m, d), BF16), jax.ShapeDtypeStruct((m, pool_width), F32),
                   jax.ShapeDtypeStruct((m, four_width), BF16), jax.ShapeDtypeStruct((m, mem_width), BF16),
                   jax.ShapeDtypeStruct((d, gate_width), BF16)],
        scratch_shapes=[pltpu.VMEM((d, mix_width), BF16)],
        compiler_params=_cparams(1, 56),
        name="in_proj",
    )(x, g.reshape(1, d), w_in, kv3, w_in)


def _out_proj_kernel(m_ref, w_ref, r_ref, g_ref, wr_ref, x1_ref, h2_ref, afft_ref, wc_ref, *, chunk, n_experts):
    @pl.when(pl.program_id(0) == 0)
    def _():
        wc_ref[...] = w_ref[...].astype(BF16)

    wr = wr_ref[...].astype(BF16)
    for r0 in range(0, m_ref.shape[0], ROW_CHUNK):
        rs = pl.ds(r0, ROW_CHUNK)
        a = m_ref[rs, :]
        for c0 in range(0, wc_ref.shape[1], chunk):
            cols = slice(c0, c0 + chunk)
            x1_ref[rs, cols] = r_ref[rs, cols] + jnp.dot(a, wc_ref[:, cols], preferred_element_type=F32)

        x = x1_ref[rs, :]
        inv = lax.rsqrt(jnp.mean(x * x, axis=-1, keepdims=True) + EPS)
        hb = ((x * inv) * g_ref[...]).astype(BF16)
        h2_ref[rs, :] = hb
        logits = jnp.dot(hb, wr, preferred_element_type=F32)
        lane = lax.broadcasted_iota(jnp.int32, logits.shape, 1)
        logits = jnp.where(lane < n_experts, logits, -1e30)
        mx = jnp.max(logits, axis=-1, keepdims=True)
        p = jnp.exp(logits - mx)
        aff = p / jnp.sum(p, axis=-1, keepdims=True)
        afft_ref[0, :, rs] = aff.T[:n_experts, :]


def _out_proj(merged, w_out, resid, g, w_router, *, batch, tm):
    m, d = resid.shape
    e = w_router.shape[1]
    s = m // batch
    tps = s // tm
    wr = jnp.pad(w_router, ((0, 0), (0, LANES - e)))
    row = pl.BlockSpec((tm, d), lambda i: (i, 0))
    return pl.pallas_call(
        functools.partial(_out_proj_kernel, chunk=512, n_experts=e),
        grid=(m // tm,),
        in_specs=[row, pl.BlockSpec((d, d), lambda i: (0, 0), pipeline_mode=pl.Buffered(1)), row,
                  pl.BlockSpec((1, d), lambda i: (0, 0)), pl.BlockSpec((d, LANES), lambda i: (0, 0))],
        out_specs=[row, row, pl.BlockSpec((1, e, tm), lambda i: (i // tps, 0, i % tps))],
        out_shape=[jax.ShapeDtypeStruct((m, d), F32), jax.ShapeDtypeStruct((m, d), BF16),
                   jax.ShapeDtypeStruct((batch, e, s), F32)],
        scratch_shapes=[pltpu.VMEM((d, d), BF16)],
        compiler_params=_cparams(1, 56),
        name="out_proj",
    )(merged, w_out, resid, g.reshape(1, d), wr)


def _pool_kernel(u_ref, pw_ref, ps_ref, o_ref, a_ref, b_ref, *, chunk):
    s = u_ref.shape[1]
    c = pw_ref.shape[1]
    h = POOL_HALO
    rows = s + 2 * h
    zeros = jnp.zeros((h, c), F32)

    def level(src, dst, off_lo, off_hi, margin):
        for r0 in range(margin, rows - margin, chunk):
            n = min(chunk, rows - margin - r0)
            dst[pl.ds(r0, n), :] = src[pl.ds(r0 + off_lo, n), :] + src[pl.ds(r0 + off_hi, n), :]

    for g, w in enumerate(POOL_WINDOWS):
        cols = slice(g * c, (g + 1) * c)
        a_ref[pl.ds(0, h), :] = zeros
        a_ref[pl.ds(h + s, h), :] = zeros
        a_ref[pl.ds(h, s), :] = u_ref[0, :, cols]
        src, off_lo, off_hi = a_ref, -1, 0
        if w >= 4:
            level(a_ref, b_ref, -1, 0, 8)
            src, off_lo, off_hi = b_ref, -1, 1
        if w >= 8:
            level(b_ref, a_ref, -1, 1, 16)
            src, off_lo, off_hi = a_ref, -2, 2
        if w >= 16:
            level(a_ref, b_ref, -2, 2, 24)
            src, off_lo, off_hi = b_ref, -4, 4
        wg = pw_ref[g].astype(BF16)
        scale = ps_ref[:, cols]
        for r0 in range(0, s, chunk):
            pos = r0 + lax.broadcasted_iota(jnp.int32, (chunk, 1), 0)
            lo = jnp.maximum(pos - w // 2, 0)
            hi = jnp.minimum(pos + (w - w // 2), s)
            cnt = (hi - lo).astype(F32)
            tot = src[pl.ds(h + r0 + off_lo, chunk), :] + src[pl.ds(h + r0 + off_hi, chunk), :]
            pooled = tot / cnt - u_ref[0, pl.ds(r0, chunk), cols]
            y = jnp.dot(pooled.astype(BF16), wg, preferred_element_type=F32)
            o_ref[0, pl.ds(r0, chunk), cols] = (y * scale).astype(o_ref.dtype)


def _pool_mixer(z3, pool_w, pool_scale):
    b, s, _ = z3.shape
    g, c, _ = pool_w.shape
    width = g * c
    assert POOL_WINDOWS == (2, 4, 8, 16) and g == len(POOL_WINDOWS)
    return pl.pallas_call(
        functools.partial(_pool_kernel, chunk=256),
        grid=(b,),
        in_specs=[pl.BlockSpec((1, s, width), lambda i: (i, 0, 0)),
                  pl.BlockSpec((g, c, c), lambda i: (0, 0, 0)),
                  pl.BlockSpec((1, width), lambda i: (0, 0))],
        out_specs=pl.BlockSpec((1, s, width), lambda i: (i, 0, 0)),
        out_shape=jax.ShapeDtypeStruct((b, s, width), BF16),
        scratch_shapes=[pltpu.VMEM((s + 2 * POOL_HALO, c), F32), pltpu.VMEM((s + 2 * POOL_HALO, c), F32)],
        compiler_params=_cparams(1, 48),
        name="pool_mixer",
    )(z3, pool_w, pool_scale.reshape(1, width))


def _dft_tables(s, c):
    def tab(n):
        r = np.outer(np.arange(n), np.arange(n)) % n
        ang = r * (2.0 * np.pi / n)
        return np.cos(ang), np.sin(ang)
    cs, ss = tab(s)
    cc, sc = tab(c)
    alt = np.zeros((8, s))
    alt[0] = 1.0 - 2.0 * (np.arange(s) % 2)
    const = lambda a: jnp.asarray(a, dtype=F32).astype(BF16)
    return const(cs[:s // 2]), const(ss[:s // 2]), const(alt), const(np.concatenate([cc, sc], axis=1))


def _fourier_kernel(u_ref, cs_ref, ss_ref, alt_ref, ccsc_ref, fw_ref, o_ref, t1_ref, t2_ref, *, norm):
    s = u_ref.shape[1]
    half = s // 2
    ng, c, _ = fw_ref.shape
    for g in range(ng):
        ab = jnp.dot(u_ref[0, :, g * c:(g + 1) * c], ccsc_ref[...], preferred_element_type=F32)
        t1_ref[:, g * c:(g + 1) * c] = ab[:, :c].astype(BF16)
        t2_ref[:, g * c:(g + 1) * c] = ab[:, c:].astype(BF16)
    p = jnp.dot(cs_ref[...], t1_ref[...], preferred_element_type=F32)
    q = jnp.dot(ss_ref[...], t2_ref[...], preferred_element_type=F32)
    mid = jnp.dot(alt_ref[...], t1_ref[...], preferred_element_type=F32)
    low = ((p - q) * norm).astype(BF16)
    mirror = ((p + q) * norm).astype(BF16)
    mid = (mid * norm).astype(BF16)
    rr = lax.broadcasted_iota(jnp.int32, (half, half), 0)
    cc = lax.broadcasted_iota(jnp.int32, (half, half), 1)
    reverse = (rr + cc == half).astype(BF16)
    first = lax.broadcasted_iota(jnp.int32, (half, 1), 0) == 0
    y_mirror, y_mid = [], []
    for g in range(ng):
        cols = slice(g * c, (g + 1) * c)
        w = fw_ref[g].astype(BF16)
        o_ref[0, pl.ds(0, half), cols] = jnp.dot(low[:, cols], w, preferred_element_type=F32).astype(o_ref.dtype)
        y_mirror.append(jnp.dot(mirror[:, cols], w, preferred_element_type=F32).astype(BF16))
        y_mid.append(jnp.dot(mid[:, cols], w, preferred_element_type=F32)[0:1, :])
    y_rev = jnp.dot(reverse, jnp.concatenate(y_mirror, axis=1), preferred_element_type=F32)
    o_ref[0, pl.ds(half, half), :] = jnp.where(first, jnp.concatenate(y_mid, axis=1), y_rev).astype(o_ref.dtype)


def _fourier_mixer(z3, fourier_w):
    b, s, _ = z3.shape
    ng, c, _ = fourier_w.shape
    width = ng * c
    cs, ss, alt, ccsc = _dft_tables(s, c)
    const = lambda shape: pl.BlockSpec(shape, lambda i: (0,) * len(shape), pipeline_mode=pl.Buffered(1))
    return pl.pallas_call(
        functools.partial(_fourier_kernel, norm=float((s * c) ** -0.5)),
        grid=(b,),
        in_specs=[pl.BlockSpec((1, s, width), lambda i: (i, 0, 0)),
                  const((s // 2, s)), const((s // 2, s)), const((8, s)), const((c, 2 * c)),
                  pl.BlockSpec((ng, c, c), lambda i: (0, 0, 0))],
        out_specs=pl.BlockSpec((1, s, width), lambda i: (i, 0, 0)),
        out_shape=jax.ShapeDtypeStruct((b, s, width), BF16),
        scratch_shapes=[pltpu.VMEM((s, width), BF16), pltpu.VMEM((s, width), BF16)],
        compiler_params=_cparams(1, 48),
        name="fourier_mixer",
    )(z3, cs, ss, alt, ccsc, fourier_w)


def _merge_kernel(ya_ref, yb_ref, yc_ref, h_ref, pp_ref, pf_ref, pm_ref, cg0, cg1, cg2, o_ref, cpp, cpf, cpm):
    @pl.when(pl.program_id(1) == 0)
    def _():
        for src, dst in ((pp_ref, cpp), (pf_ref, cpf), (pm_ref, cpm)):
            dst[...] = src[...].astype(BF16)

    for r0 in range(0, h_ref.shape[0], MERGE_ROW_CHUNK):
        rs = pl.ds(r0, MERGE_ROW_CHUNK)
        h = h_ref[rs, :]

        def branch(y_ref, proj, gate_w):
            gate = jax.nn.sigmoid(jnp.dot(h, gate_w[...], preferred_element_type=F32))
            return gate * jnp.dot(y_ref[rs, :], proj[...], preferred_element_type=F32)

        acc = branch(ya_ref, cpp, cg0)
        acc = acc + branch(yb_ref, cpf, cg1)
        acc = acc + branch(yc_ref, cpm, cg2)
        o_ref[rs, :] = acc.astype(o_ref.dtype)


def _gated_merge(ya, yb, yc, h, proj_pool, proj_fourier, proj_mem, w_gates, *, tm, tn):
    m, d = h.shape
    nb = d // tn
    act = lambda width: pl.BlockSpec((tm, width), lambda n, i: (i, 0))
    wsp = lambda rows, off: pl.BlockSpec((rows, tn), lambda n, i: (0, n + off))
    return pl.pallas_call(
        _merge_kernel,
        grid=(d // tn, m // tm),
        in_specs=[act(ya.shape[1]), act(yb.shape[1]), act(yc.shape[1]), act(d),
                  wsp(proj_pool.shape[0], 0), wsp(proj_fourier.shape[0], 0), wsp(proj_mem.shape[0], 0),
                  wsp(d, 0), wsp(d, nb), wsp(d, 2 * nb)],
        out_specs=pl.BlockSpec((tm, tn), lambda n, i: (i, n)),
        out_shape=jax.ShapeDtypeStruct((m, d), BF16),
        scratch_shapes=[pltpu.VMEM((proj_pool.shape[0], tn), BF16), pltpu.VMEM((proj_fourier.shape[0], tn), BF16),
                        pltpu.VMEM((proj_mem.shape[0], tn), BF16)],
        compiler_params=_cparams(2, 56),
        name="gated_merge",
    )(ya, yb, yc, h, proj_pool, proj_fourier, proj_mem, w_gates, w_gates, w_gates)


def _select_kernel(aff_ref, tri_ref, slot_ref, slot_t_ref, starts_ref, bounds_ref, *, cap, tile, block):
    a = aff_ref[...]
    rows = a.shape[0]
    capf = float(cap)

    def count(pred):
        return jnp.sum(pred.astype(F32), axis=-1, keepdims=True)

    def body(i, t_bits):
        cand = t_bits | jnp.left_shift(jnp.int32(1), 30 - i)
        return jnp.where(count(a >= pltpu.bitcast(cand, F32)) >= capf, cand, t_bits)

    t = pltpu.bitcast(lax.fori_loop(0, 31, body, jnp.zeros((rows, 1), jnp.int32)), F32)
    gt = a > t
    eq = a == t
    need = capf - count(gt)
    tri = tri_ref[...]
    eq_rank = jnp.dot(eq.astype(BF16), tri, preferred_element_type=F32)
    sel = gt | (eq & (eq_rank < need))
    pos = jnp.dot(sel.astype(BF16), tri, preferred_element_type=F32)
    slot = jnp.where(sel, pos, -1.0)
    slot_ref[...] = slot
    slot_t_ref[...] = slot.T
    s = a.shape[1]
    lane = lax.broadcasted_iota(jnp.int32, (rows, LANES), 1)
    starts = jnp.where(lane == s // tile, capf, 0.0)
    for t in range(s // tile):
        starts = jnp.where(lane == t, pos[:, t * tile:t * tile + 1], starts)
    starts_ref[...] = starts.astype(jnp.int32)

    nblk = cap // block
    tok = lax.broadcasted_iota(jnp.int32, (1, s), 1).astype(F32)

    def token_of(j):
        return jnp.sum(jnp.where(slot == float(j), tok, 0.0), axis=-1, keepdims=True)

    bounds = jnp.zeros((rows, LANES), F32)
    for i in range(nblk):
        bounds = jnp.where(lane == i, token_of(i * block), bounds)
        bounds = jnp.where(lane == nblk + i, token_of((i + 1) * block - 1) + 1.0, bounds)
    bounds_ref[...] = bounds.astype(jnp.int32)


def _select(aff_rows, *, cap, tile, block):
    rows, s = aff_rows.shape
    assert rows == LANES
    idx = jnp.arange(s, dtype=jnp.int32)
    tri = (idx[:, None] < idx[None, :]).astype(BF16)
    full = lambda shape: pl.BlockSpec(shape, lambda i: (0, 0))
    assert 2 * cap // block <= LANES and s // tile + 1 <= LANES
    return pl.pallas_call(
        functools.partial(_select_kernel, cap=cap, tile=tile, block=block),
        grid=(1,),
        in_specs=[full((rows, s)), full((s, s))],
        out_specs=[full((rows, s)), full((s, rows)), full((rows, LANES)), full((rows, LANES))],
        out_shape=[jax.ShapeDtypeStruct((rows, s), F32), jax.ShapeDtypeStruct((s, rows), F32),
                   jax.ShapeDtypeStruct((rows, LANES), jnp.int32), jax.ShapeDtypeStruct((rows, LANES), jnp.int32)],
        compiler_params=_cparams(1, 48),
        name="expert_select",
    )(aff_rows, tri)


def _dispatch_kernel(bounds_ref, slot_ref, aff_ref, h_hbm, o_ref, oa_ref, hbuf, sem, *, window, group, n_steps):
    bi = pl.program_id(0)
    ji = pl.program_id(1)
    nblk = pl.num_programs(1)
    ne, block, d = o_ref.shape
    s = hbuf.shape[1]

    rows_per = s // n_steps

    def h_copy(batch, piece):
        buf = jnp.bitwise_and(batch, 1)
        return pltpu.make_async_copy(h_hbm.at[batch, pl.ds(piece * rows_per, rows_per), :],
                                     hbuf.at[buf, pl.ds(piece * rows_per, rows_per)], sem.at[buf])

    @pl.when(jnp.logical_and(bi == 0, ji == 0))
    def _():
        for piece in range(n_steps):
            h_copy(0, piece).start()

    @pl.when(ji == 0)
    def _():
        for piece in range(n_steps):
            h_copy(bi, piece).wait()

    @pl.when(bi + 1 < pl.num_programs(0))
    def _():
        h_copy(bi + 1, ji).start()

    cur = jnp.bitwise_and(bi, 1)

    lo = hi = None
    for k in range(ne):
        l = bounds_ref[bi * ne + k, ji]
        h = bounds_ref[bi * ne + k, nblk + ji]
        lo = l if lo is None else jnp.minimum(lo, l)
        hi = h if hi is None else jnp.maximum(hi, h)
    w0 = jnp.minimum(jnp.bitwise_and(lo, -BF16_ROWS), s - window)
    fits = hi - w0 <= window

    tok_all = lax.broadcasted_iota(jnp.int32, (1, s), 1).astype(F32)
    slot_ids = (ji * block + lax.broadcasted_iota(jnp.int32, (block, 1), 0)).astype(F32)

    def tokens_of(g0):
        cols = []
        for k in range(g0, g0 + group):
            hit = slot_ref[0, k:k + 1, :] == slot_ids
            cols.append(jnp.sum(jnp.where(hit, tok_all, 0.0), axis=-1, keepdims=True))
            oa_ref[k] = jnp.sum(jnp.where(hit, aff_ref[0, k:k + 1, :], 0.0), axis=-1, keepdims=True)
        return jnp.concatenate(cols, axis=0)

    def gather(width, first_tok, src):
        t = lax.broadcasted_iota(jnp.int32, (group * block, width), 1).astype(F32) + first_tok
        toks = tokens_of(0)
        for g0 in range(0, ne, group):
            onehot = (toks == t).astype(BF16)
            if g0 + group < ne:
                toks = tokens_of(g0 + group)
            out = jnp.dot(onehot, src, preferred_element_type=F32)
            o_ref[g0:g0 + group] = out.reshape(group, block, d).astype(o_ref.dtype)

    @pl.when(fits)
    def _():
        gather(window, w0.astype(F32), hbuf[cur, pl.ds(pl.multiple_of(w0, BF16_ROWS), window), :])

    @pl.when(jnp.logical_not(fits))
    def _():
        gather(s, 0.0, hbuf[cur])


def _dispatch(bounds, slot3, aff_t, h3, *, n_experts, cap, block, window):
    b, s, d = h3.shape
    assert window % BF16_ROWS == 0 and window <= s and cap % block == 0 and block % BF16_ROWS == 0
    nblk = cap // block
    grid_spec = pltpu.PrefetchScalarGridSpec(
        num_scalar_prefetch=1,
        grid=(b, nblk),
        in_specs=[pl.BlockSpec((1, n_experts, s), lambda i, j, bd: (i, 0, 0)),
                  pl.BlockSpec((1, n_experts, s), lambda i, j, bd: (i, 0, 0)),
                  pl.BlockSpec(memory_space=pl.ANY)],
        out_specs=[pl.BlockSpec((n_experts, block, d), lambda i, j, bd: (0, i * nblk + j, 0)),
                   pl.BlockSpec((n_experts, block, 1), lambda i, j, bd: (0, i * nblk + j, 0))],
        scratch_shapes=[pltpu.VMEM((2, s, d), BF16), pltpu.SemaphoreType.DMA((2,))],
    )
    return pl.pallas_call(
        functools.partial(_dispatch_kernel, window=window, group=DISPATCH_GROUP, n_steps=nblk),
        grid_spec=grid_spec,
        out_shape=[jax.ShapeDtypeStruct((n_experts, b * cap, d), BF16),
                   jax.ShapeDtypeStruct((n_experts, b * cap, 1), F32)],
        compiler_params=_cparams(2, 48),
        name="dispatch",
    )(bounds, slot3, aff_t, h3)


def _combine_kernel(starts_ref, slot_ref, y_hbm, x_ref, g_ref, o_ref, ybuf, sem, *, win, n_tiles, final_norm):
    bi = pl.program_id(0)
    ti = pl.program_id(1)
    tt, d = x_ref.shape
    _, ne, cap, _ = ybuf.shape

    per = ne // n_tiles

    def y_copy(batch, group):
        buf = jnp.bitwise_and(batch, 1)
        return pltpu.make_async_copy(y_hbm.at[pl.ds(group * per, per), pl.ds(batch * cap, cap), :],
                                     ybuf.at[buf, pl.ds(group * per, per)], sem.at[buf])

    @pl.when(jnp.logical_and(bi == 0, ti == 0))
    def _():
        for grp in range(n_tiles):
            y_copy(0, grp).start()

    @pl.when(ti == 0)
    def _():
        for grp in range(n_tiles):
            y_copy(bi, grp).wait()

    @pl.when(bi + 1 < pl.num_programs(0))
    def _():
        y_copy(bi + 1, ti).start()

    cur = jnp.bitwise_and(bi, 1)

    wins = []
    fits = None
    for k in range(ne):
        lo = starts_ref[bi * ne + k, ti]
        hi = starts_ref[bi * ne + k, ti + 1]
        w0 = jnp.minimum(jnp.bitwise_and(lo, -BF16_ROWS), cap - win)
        ok = hi - w0 <= win
        fits = ok if fits is None else jnp.logical_and(fits, ok)
        wins.append(w0)

    slots = slot_ref[...].astype(BF16)

    def onehot(width, first_slot):
        shift = width.bit_length() - 1
        n = ne * width
        lane_of = bi * ne + lax.shift_right_logical(lax.broadcasted_iota(jnp.int32, (LANES, n), 1), shift)
        spread = (lax.broadcasted_iota(jnp.int32, (LANES, n), 0) == lane_of).astype(BF16)
        slot_b = jnp.dot(slots, spread, preferred_element_type=F32)
        c = lax.broadcasted_iota(jnp.int32, (1, n), 1)
        target = jnp.bitwise_and(c, width - 1)
        if first_slot is not None:
            kk = lax.shift_right_logical(c, shift)
            for k in range(ne):
                target = target + jnp.where(kk == k, first_slot[k], 0)
        return (slot_b == target.astype(F32)).astype(BF16)

    def finish(contrib):
        x = x_ref[...] + contrib
        if final_norm:
            inv = lax.rsqrt(jnp.mean(x * x, axis=-1, keepdims=True) + EPS)
            x = (x * inv) * g_ref[...]
        o_ref[...] = x

    @pl.when(fits)
    def _():
        ywin = jnp.concatenate([ybuf[cur, k, pl.ds(pl.multiple_of(wins[k], BF16_ROWS), win), :] for k in range(ne)],
                               axis=0)
        finish(jnp.dot(onehot(win, wins), ywin, preferred_element_type=F32))

    @pl.when(jnp.logical_not(fits))
    def _():
        finish(jnp.dot(onehot(cap, None), ybuf[cur].reshape(ne * cap, d), preferred_element_type=F32))


def _combine(starts, slot_cols, y, x1, g, *, batch, cap, tt, win, final_norm):
    m, d = x1.shape
    ne = y.shape[0]
    s = m // batch
    tps = s // tt
    assert slot_cols.shape == (s, LANES) and batch * ne == LANES
    assert win % BF16_ROWS == 0 and win <= cap and win & (win - 1) == 0 and cap & (cap - 1) == 0 and ne % tps == 0
    grid_spec = pltpu.PrefetchScalarGridSpec(
        num_scalar_prefetch=1,
        grid=(batch, tps),
        in_specs=[pl.BlockSpec((tt, LANES), lambda b, t, st: (t, 0)),
                  pl.BlockSpec(memory_space=pl.ANY),
                  pl.BlockSpec((tt, d), lambda b, t, st: (b * tps + t, 0)),
                  pl.BlockSpec((1, d), lambda b, t, st: (0, 0))],
        out_specs=pl.BlockSpec((tt, d), lambda b, t, st: (b * tps + t, 0)),
        scratch_shapes=[pltpu.VMEM((2, ne, cap, d), BF16), pltpu.SemaphoreType.DMA((2,))],
    )
    return pl.pallas_call(
        functools.partial(_combine_kernel, win=win, n_tiles=tps, final_norm=final_norm),
        grid_spec=grid_spec,
        out_shape=jax.ShapeDtypeStruct((m, d), F32),
        compiler_params=_cparams(2, 56),
        name="combine",
    )(starts, slot_cols, y, x1, g.reshape(1, d))


def _expert_up_kernel(x_ref, wg_ref, wu_ref, o_ref, *, chunk):
    wg = wg_ref[0].astype(BF16)
    wu = wu_ref[0].astype(BF16)
    for r0 in range(0, x_ref.shape[1], chunk):
        x = x_ref[0, pl.ds(r0, chunk), :]
        gate = jnp.dot(x, wg, preferred_element_type=F32)
        up = jnp.dot(x, wu, preferred_element_type=F32)
        o_ref[0, pl.ds(r0, chunk), :] = (jax.nn.silu(gate) * up).astype(o_ref.dtype)


def _expert_up(xin, w_gate, w_up, *, tf):
    e, m, d = xin.shape
    f = w_gate.shape[2]
    return pl.pallas_call(
        functools.partial(_expert_up_kernel, chunk=512),
        grid=(e, f // tf),
        in_specs=[pl.BlockSpec((1, m, d), lambda i, j: (i, 0, 0)),
                  pl.BlockSpec((1, d, tf), lambda i, j: (i, 0, j)),
                  pl.BlockSpec((1, d, tf), lambda i, j: (i, 0, j))],
        out_specs=pl.BlockSpec((1, m, tf), lambda i, j: (i, 0, j)),
        out_shape=jax.ShapeDtypeStruct((e, m, f), BF16),
        compiler_params=_cparams(2, 56),
        name="expert_up",
    )(xin, w_gate, w_up)


def _expert_down_kernel(h_ref, wd_ref, a_ref, o_ref, *, chunk):
    wd = wd_ref[0].astype(BF16)
    for r0 in range(0, h_ref.shape[1], chunk):
        y = jnp.dot(h_ref[0, pl.ds(r0, chunk), :], wd, preferred_element_type=F32)
        o_ref[0, pl.ds(r0, chunk), :] = (y * a_ref[0, pl.ds(r0, chunk), :]).astype(o_ref.dtype)


def _expert_down(hidden, w_down, aff, *, tn):
    e, m, f = hidden.shape
    d = w_down.shape[2]
    return pl.pallas_call(
        functools.partial(_expert_down_kernel, chunk=512),
        grid=(e, d // tn),
        in_specs=[pl.BlockSpec((1, m, f), lambda i, j: (i, 0, 0)),
                  pl.BlockSpec((1, f, tn), lambda i, j: (i, 0, j)),
                  pl.BlockSpec((1, m, 1), lambda i, j: (i, 0, 0))],
        out_specs=pl.BlockSpec((1, m, tn), lambda i, j: (i, 0, j)),
        out_shape=jax.ShapeDtypeStruct((e, m, d), BF16),
        compiler_params=_cparams(2, 56),
        name="expert_down",
    )(hidden, w_down, aff)


def kernel(x, mem, norm_mix_g, norm_mem_g, w_in, pool_w, pool_scale, fourier_w, w_kv_mem, proj_pool, proj_fourier,
           proj_mem, w_out, norm_ffn_g, w_router, w_expert_gate, w_expert_up, w_expert_down, norm_final_g):
    b, s, d = x.shape
    depth = w_in.shape[0]
    n_tok = b * s
    pool_width = pool_w.shape[1] * pool_w.shape[2]
    four_width = fourier_w.shape[1] * fourier_w.shape[2]
    mem_width = proj_mem.shape[1]
    mix_width = pool_width + four_width + mem_width
    e = w_router.shape[2]
    cap = CAPACITY_FACTOR * s // e

    xf = x.reshape(n_tok, d)
    for l in range(depth):
        kv = _kv_proj(mem.reshape(-1, d), norm_mem_g[l], w_kv_mem[l], tm=KV_TILE[0], tn=KV_TILE[1])
        h, zp, zf, yc, w_gates = _in_proj(xf, norm_mix_g[l], w_in[l], kv.reshape(b, -1, 2 * mem_width),
                                          pool_width=pool_width, four_width=four_width, n_heads=MEM_HEADS,
                                          tm=ROW_TILE)
        ya = _pool_mixer(zp.reshape(b, s, pool_width), pool_w[l], pool_scale[l])
        yb = _fourier_mixer(zf.reshape(b, s, four_width), fourier_w[l])
        merged = _gated_merge(ya.reshape(n_tok, -1), yb.reshape(n_tok, -1), yc, h,
                              proj_pool[l], proj_fourier[l], proj_mem[l], w_gates, tm=MERGE_TILE[0], tn=MERGE_TILE[1])

        x1, h2, aff_t = _out_proj(merged, w_out[l], xf, norm_ffn_g[l], w_router[l], batch=b, tm=ROW_TILE)
        slot_rows, slot_cols, starts, bounds = _select(aff_t.reshape(b * e, s), cap=cap, tile=COMBINE_TILE,
                                                       block=DISPATCH_BLOCK)
        xin, aff_slot = _dispatch(bounds[:, :2 * cap // DISPATCH_BLOCK], slot_rows.reshape(b, e, s), aff_t,
                                  h2.reshape(b, s, d), n_experts=e, cap=cap, block=DISPATCH_BLOCK,
                                  window=DISPATCH_WINDOW)
        hidden = _expert_up(xin, w_expert_gate[l], w_expert_up[l], tf=EXPERT_UP_COLS)
        y = _expert_down(hidden, w_expert_down[l], aff_slot, tn=EXPERT_DOWN_COLS)
        last = l + 1 == depth
        xf = _combine(starts[:, :s // COMBINE_TILE + 1], slot_cols, y, x1, norm_final_g, batch=b, cap=cap,
                      tt=COMBINE_TILE, win=COMBINE_WINDOW, final_norm=last)
    return xf.reshape(b, s, d)
```
